```python
import math
import jax
import jax.numpy as jnp
from jax import lax
import numpy as np

D_MODEL = 2048
BATCH = 8
SEQ = 2048
DEPTH = 2

GRID_W = 64
CTX_LEN = 256
EPS = 1e-6
N_MOD = 6

D_MIX = D_MODEL
HY_W = D_MIX // 4
MLA_NOPE = 128
MLA_ROPE = 64
MLA_V = 128
MLA_HEADS = (D_MIX - HY_W) // 2 // MLA_V
MLA_Q_RANK = 3 * D_MODEL // 8
MLA_KV_RANK = D_MODEL // 8
GQA_HEAD_DIM = 128
GQA_HEADS = (D_MIX - HY_W) // 2 // GQA_HEAD_DIM
GQA_KV_HEADS = 2

HY_ORDER = 2
HY_SHORT = 3
HY_BANDS = 16
HY_POS_DIM = 1 + 2 * HY_BANDS
HY_FILTER_HIDDEN = 64
HY_TARGET = 1e-2
HY_FAST = 0.3
HY_SLOW = 1.5

ROPE_THETA = 10000.0
Q_BLOCK = 128

N_EXPERTS = 64
TOP_K = 6
N_GROUPS = 8
TOPK_GROUPS = 4
EXPERT_HIDDEN = D_MODEL // 4
ROUTED_SCALE = 2.5
MOE_BLOCK = 256

HY_COLS = 3 * HY_W
GQA_Q_COLS = GQA_HEADS * GQA_HEAD_DIM
GQA_KV_COLS = GQA_KV_HEADS * GQA_HEAD_DIM
KV_START = HY_COLS + MLA_Q_RANK + GQA_Q_COLS
IN_COLS = KV_START + MLA_KV_RANK + MLA_ROPE + 2 * GQA_KV_COLS
Q_SPLITS = (HY_COLS, HY_COLS + MLA_Q_RANK)
KV_SPLITS = (MLA_KV_RANK, MLA_KV_RANK + MLA_ROPE, MLA_KV_RANK + MLA_ROPE + GQA_KV_COLS)

kernel_name = "hybrid_dit_hyena_mla_gqa_moe"

F32 = jnp.float32


def rmsnorm(x, g):
    xf = x.astype(F32)
    y = xf * lax.rsqrt(jnp.mean(xf * xf, axis=-1, keepdims=True) + EPS)
    return (y * g.astype(F32)).astype(x.dtype)


def modulate(h, shift, scale):
    return h * (1 + scale) + shift


def adaln(s, w, b):
    m = (s @ w + b).reshape(s.shape[0], N_MOD, -1)
    return tuple(m[:, i, None, :] for i in range(N_MOD))


def swiglu(x, wg, wu, wd):
    return (jax.nn.silu(x @ wg) * (x @ wu)) @ wd


def axial_angles(n, rot_dim):
    rows = n // GRID_W
    row = jnp.repeat(jnp.arange(rows, dtype=F32), GRID_W)
    col = jnp.tile(jnp.arange(GRID_W, dtype=F32), rows)
    axis_dim = rot_dim // 2
    inv = ROPE_THETA ** (-jnp.arange(0, axis_dim, 2, dtype=F32) / axis_dim)
    return (row[:, None] * inv, col[:, None] * inv)


def rotate_half(x, ang):
    x1, x2 = jnp.split(x.astype(F32), 2, axis=-1)
    cos = jnp.cos(ang)[:, None, :]
    sin = jnp.sin(ang)[:, None, :]
    return jnp.concatenate([x1 * cos - x2 * sin, x1 * sin + x2 * cos], axis=-1)


def rope_2d(x, ang):
    ang_row, ang_col = ang
    xr, xc = jnp.split(x, 2, axis=-1)
    return jnp.concatenate([rotate_half(xr, ang_row), rotate_half(xc, ang_col)], axis=-1).astype(x.dtype)


def attend(q, k, v, scale):
    bsz, lq, h, dk = q.shape
    g = k.shape[2]
    qg = q.reshape(bsz, lq, g, h // g, dk)
    s = jnp.einsum('bqgrd,bkgd->bgrqk', qg, k, preferred_element_type=F32) * scale
    p = jax.nn.softmax(s, axis=-1)
    o = jnp.einsum('bgrqk,bkgd->bqgrd', p.astype(v.dtype), v)
    return o.reshape(bsz, lq, h, v.shape[-1])


def attend_blocked(q, k, v, scale):
    bsz, lq, h, dk = q.shape
    nb = lq // Q_BLOCK
    qb = q.reshape(bsz, nb, Q_BLOCK, h, dk).transpose(1, 0, 2, 3, 4)
    ob = lax.map(lambda qi: attend(qi, k, v, scale), qb)
    return ob.transpose(1, 0, 2, 3, 4).reshape(bsz, lq, h, v.shape[-1])


def short_conv(u, w):
    up = jnp.pad(u, ((0, 0), (1, 1), (0, 0)))
    return up[:, :-2] * w[0] + up[:, 1:-1] * w[1] + up[:, 2:] * w[2]


def hyena_filters(n, w1, b1, w2, b2, w3, freq):
    t = jnp.linspace(0.0, 1.0, n, dtype=F32)[:, None]
    ang = 2.0 * math.pi * jnp.arange(n, dtype=F32)[:, None] / n
    bands = jnp.linspace(1e-4, HY_BANDS - 1, HY_BANDS, dtype=F32)
    feats = jnp.concatenate([t, jnp.cos(ang * bands), jnp.sin(ang * bands)], axis=-1)
    fr = freq.astype(F32)
    hdn = jnp.sin(fr * (feats @ w1.astype(F32) + b1.astype(F32)))
    hdn = jnp.sin(fr * (hdn @ w2.astype(F32) + b2.astype(F32)))
    h = (hdn @ w3.astype(F32)).reshape(n, HY_ORDER, 2, HY_W)
    deltas = jnp.abs(jnp.linspace(math.log(HY_TARGET) / HY_SLOW, math.log(HY_TARGET) / HY_FAST, HY_W, dtype=F32))
    h = h * jnp.exp(-t * deltas)[:, None, None, :]
    h_fwd, h_bwd = h[:, :, 0], h[:, :, 1]
    k = jnp.concatenate([h_fwd, jnp.zeros_like(h_fwd[:1]), h_bwd[:0:-1]], axis=0)
    k = k * lax.rsqrt(jnp.sum(k * k, axis=0, keepdims=True) + EPS)
    return jnp.fft.rfft(k, n=2 * n, axis=0)


def long_conv(z, kf, skip):
    n = z.shape[1]
    zf32 = z.astype(F32)
    zf = jnp.fft.rfft(zf32, n=2 * n, axis=1)
    y = jnp.fft.irfft(zf * kf[None], n=2 * n, axis=1)[:, :n]
    return (y + zf32 * skip.astype(F32)).astype(z.dtype)


def hyena_mixer(u, conv_w, w1, b1, w2, b2, w3, freq, skip):
    n = u.shape[1]
    v, x1, x2 = jnp.split(short_conv(u, conv_w), 3, axis=-1)
    kf = hyena_filters(n, w1, b1, w2, b2, w3, freq)
    z = x1 * long_conv(v, kf[:, 0], skip[0])
    return x2 * long_conv(z, kf[:, 1], skip[1])


def mla_queries(cq, q_norm, w_uq, ang):
    bsz, n, _ = cq.shape
    q = (rmsnorm(cq, q_norm) @ w_uq).reshape(bsz, n, MLA_HEADS, MLA_NOPE + MLA_ROPE)
    if ang is None:
        return q
    return jnp.concatenate([q[..., :MLA_NOPE], rope_2d(q[..., MLA_NOPE:], ang)], axis=-1)


def mla_keys_values(ckv, k_rope, kv_norm, w_ukv, ang):
    bsz, n, _ = ckv.shape
    kv = (rmsnorm(ckv, kv_norm) @ w_ukv).reshape(bsz, n, MLA_HEADS, MLA_NOPE + MLA_V)
    k_nope, v = kv[..., :MLA_NOPE], kv[..., MLA_NOPE:]
    kr = k_rope[:, :, None, :]
    if ang is not None:
        kr = rope_2d(kr, ang)
    k = jnp.concatenate([k_nope, jnp.broadcast_to(kr, (bsz, n, MLA_HEADS, MLA_ROPE))], axis=-1)
    return k, v


def gqa_queries(q, q_norm, ang):
    bsz, n, _ = q.shape
    q = rmsnorm(q.reshape(bsz, n, GQA_HEADS, GQA_HEAD_DIM), q_norm)
    return q if ang is None else rope_2d(q, ang)


def gqa_keys_values(k, v, k_norm, ang):
    bsz, n, _ = k.shape
    k = rmsnorm(k.reshape(bsz, n, GQA_KV_HEADS, GQA_HEAD_DIM), k_norm)
    if ang is not None:
        k = rope_2d(k, ang)
    return k, v.reshape(bsz, n, GQA_KV_HEADS, GQA_HEAD_DIM)


def token_mixer(h_lat, h_ctx, ctx_out, w_in, w_out, hy_conv, hy_w1, hy_b1, hy_w2, hy_b2, hy_w3,
                hy_freq, hy_skip, mla_q_norm, mla_kv_norm, mla_w_uq, mla_w_ukv, gqa_q_norm, gqa_k_norm):
    bsz, n, _ = h_lat.shape
    ang_mla = axial_angles(n, MLA_ROPE)
    ang_gqa = axial_angles(n, GQA_HEAD_DIM)
    mla_scale = (MLA_NOPE + MLA_ROPE) ** -0.5
    gqa_scale = GQA_HEAD_DIM ** -0.5

    p_lat = h_lat @ w_in
    hy_lat, mq_lat, gq_lat = jnp.split(p_lat[..., :KV_START], Q_SPLITS, axis=-1)
    ckv_lat, kr_lat, gk_lat, gv_lat = jnp.split(p_lat[..., KV_START:], KV_SPLITS, axis=-1)
    if ctx_out:
        p_ctx = h_ctx @ w_in
        hy_ctx, mq_ctx, gq_ctx = jnp.split(p_ctx[..., :KV_START], Q_SPLITS, axis=-1)
        kv_ctx = p_ctx[..., KV_START:]
    else:
        kv_ctx = h_ctx @ w_in[:, KV_START:]
    ckv_ctx, kr_ctx, gk_ctx, gv_ctx = jnp.split(kv_ctx, KV_SPLITS, axis=-1)

    mk_ctx, mv_ctx = mla_keys_values(ckv_ctx, kr_ctx, mla_kv_norm, mla_w_ukv, None)
    mk_lat, mv_lat = mla_keys_values(ckv_lat, kr_lat, mla_kv_norm, mla_w_ukv, ang_mla)
    gk_c, gv_c = gqa_keys_values(gk_ctx, gv_ctx, gqa_k_norm, None)
    gk_l, gv_l = gqa_keys_values(gk_lat, gv_lat, gqa_k_norm, ang_gqa)

    mo = attend_blocked(mla_queries(mq_lat, mla_q_norm, mla_w_uq, ang_mla),
                        jnp.concatenate([mk_ctx, mk_lat], axis=1),
                        jnp.concatenate([mv_ctx, mv_lat], axis=1), mla_scale)
    go = attend_blocked(gqa_queries(gq_lat, gqa_q_norm, ang_gqa),
                        jnp.concatenate([gk_c, gk_l], axis=1),
                        jnp.concatenate([gv_c, gv_l], axis=1), gqa_scale)
    ho = hyena_mixer(hy_lat, hy_conv, hy_w1, hy_b1, hy_w2, hy_b2, hy_w3, hy_freq, hy_skip)
    out_lat = jnp.concatenate([ho, mo.reshape(bsz, n, -1), go.reshape(bsz, n, -1)], axis=-1) @ w_out
    if not ctx_out:
        return out_lat, None

    n_ctx = h_ctx.shape[1]
    mo_c = attend(mla_queries(mq_ctx, mla_q_norm, mla_w_uq, None), mk_ctx, mv_ctx, mla_scale)
    go_c = attend(gqa_queries(gq_ctx, gqa_q_norm, None), gk_c, gv_c, gqa_scale)
    ho_c = hyena_mixer(hy_ctx, hy_conv, hy_w1, hy_b1, hy_w2, hy_b2, hy_w3, hy_freq, hy_skip)
    out_ctx = jnp.concatenate([ho_c, mo_c.reshape(bsz, n_ctx, -1), go_c.reshape(bsz, n_ctx, -1)], axis=-1) @ w_out
    return out_lat, out_ctx


def moe_ffn(h, router_w, router_bias, w_gate, w_up, w_down, sh_gate, sh_up, sh_down):
    n_tok, d = h.shape
    scores = jax.nn.sigmoid((h @ router_w).astype(F32))
    biased = scores + router_bias.astype(F32)
    grp = biased.reshape(n_tok, N_GROUPS, N_EXPERTS // N_GROUPS)
    grp_score = lax.top_k(grp, 2)[0].sum(-1)
    _, g_idx = lax.top_k(grp_score, TOPK_GROUPS)
    g_mask = jax.nn.one_hot(g_idx, N_GROUPS, dtype=F32).sum(1) > 0
    e_mask = jnp.repeat(g_mask, N_EXPERTS // N_GROUPS, axis=1)
    _, e_idx = lax.top_k(jnp.where(e_mask, biased, -jnp.inf), TOP_K)
    gates = jnp.take_along_axis(scores, e_idx, axis=1)
    gates = gates / jnp.sum(gates, axis=-1, keepdims=True) * ROUTED_SCALE

    n_pairs = n_tok * TOP_K
    e_flat = e_idx.reshape(-1)
    tok_flat = jnp.repeat(jnp.arange(n_tok, dtype=jnp.int32), TOP_K)
    g_flat = gates.reshape(-1)
    order = jnp.argsort(e_flat)
    e_sorted = e_flat[order]
    counts = jnp.zeros((N_EXPERTS,), jnp.int32).at[e_flat].add(1)
    starts = jnp.cumsum(counts) - counts
    padded = (counts + MOE_BLOCK - 1) // MOE_BLOCK * MOE_BLOCK
    p_ends = jnp.cumsum(padded)
    p_starts = p_ends - padded
    dest = p_starts[e_sorted] + jnp.arange(n_pairs, dtype=jnp.int32) - starts[e_sorted]
    n_blocks = -(-n_pairs // MOE_BLOCK) + N_EXPERTS
    n_rows = n_blocks * MOE_BLOCK
    buf_tok = jnp.full((n_rows,), n_tok, jnp.int32).at[dest].set(tok_flat[order])
    buf_gate = jnp.zeros((n_rows,), F32).at[dest].set(g_flat[order])
    blk_start = jnp.arange(n_blocks, dtype=jnp.int32) * MOE_BLOCK
    blk_expert = jnp.minimum(jnp.searchsorted(p_ends, blk_start, side='right'), N_EXPERTS - 1)
    h_pad = jnp.concatenate([h, jnp.zeros((1, d), h.dtype)], axis=0)

    def expert_block(acc, blk):
        toks, wts, e = blk
        yb = swiglu(h_pad[toks], w_gate[e], w_up[e], w_down[e])
        return acc.at[toks].add(yb * wts[:, None].astype(yb.dtype)), None

    acc, _ = lax.scan(expert_block, jnp.zeros((n_tok + 1, d), h.dtype),
                      (buf_tok.reshape(n_blocks, MOE_BLOCK), buf_gate.reshape(n_blocks, MOE_BLOCK), blk_expert))
    return swiglu(h, sh_gate, sh_up, sh_down) + acc[:n_tok]


def setup_inputs(seed: int = 0) -> dict:
    key = jax.random.key(seed)
    ks = jax.random.split(key, 31)
    L = DEPTH

    def nrm(k, shape, scale):
        return jax.random.normal(k, shape, F32) * scale

    return {
        "x": nrm(ks[0], (BATCH, SEQ, D_MODEL), 1.0),
        "c": nrm(ks[1], (BATCH, D_MODEL), 1.0),
        "ctx": nrm(ks[2], (BATCH, CTX_LEN, D_MODEL), 1.0),
        "c_ctx": nrm(ks[3], (D_MODEL,), 1.0),
        "ada_w": nrm(ks[4], (L, D_MODEL, N_MOD * D_MODEL), 0.5 * D_MODEL ** -0.5),
        "ada_b": nrm(ks[5], (L, N_MOD * D_MODEL), 0.02),
        "norm_g": 1.0 + nrm(ks[6], (L, 4, D_MODEL), 0.05),
        "w_in": nrm(ks[7], (L, D_MODEL, IN_COLS), D_MODEL ** -0.5),
        "w_out": nrm(ks[8], (L, D_MIX, D_MODEL), D_MIX ** -0.5),
        "hy_conv": nrm(ks[9], (L, HY_SHORT, HY_COLS), HY_SHORT ** -0.5),
        "hy_w1": nrm(ks[10], (L, HY_POS_DIM, HY_FILTER_HIDDEN), HY_POS_DIM ** -0.5),
        "hy_b1": nrm(ks[11], (L, HY_FILTER_HIDDEN), 0.1),
        "hy_w2": nrm(ks[12], (L, HY_FILTER_HIDDEN, HY_FILTER_HIDDEN), HY_FILTER_HIDDEN ** -0.5),
        "hy_b2": nrm(ks[13], (L, HY_FILTER_HIDDEN), 0.1),
        "hy_w3": nrm(ks[14], (L, HY_FILTER_HIDDEN, HY_ORDER * 2 * HY_W), HY_FILTER_HIDDEN ** -0.5),
        "hy_freq": 1.0 + nrm(ks[15], (L, HY_FILTER_HIDDEN), 0.1),
        "hy_skip": nrm(ks[16], (L, HY_ORDER, HY_W), 0.1),
        "mla_q_norm": 1.0 + nrm(ks[17], (L, MLA_Q_RANK), 0.05),
        "mla_kv_norm": 1.0 + nrm(ks[18], (L, MLA_KV_RANK), 0.05),
        "mla_w_uq": nrm(ks[19], (L, MLA_Q_RANK, MLA_HEADS * (MLA_NOPE + MLA_ROPE)), MLA_Q_RANK ** -0.5),
        "mla_w_ukv": nrm(ks[20], (L, MLA_KV_RANK, MLA_HEADS * (MLA_NOPE + MLA_V)), MLA_KV_RANK ** -0.5),
        "gqa_q_norm": 1.0 + nrm(ks[21], (L, GQA_HEAD_DIM), 0.05),
        "gqa_k_norm": 1.0 + nrm(ks[22], (L, GQA_HEAD_DIM), 0.05),
        "router_w": nrm(ks[23], (L, D_MODEL, N_EXPERTS), D_MODEL ** -0.5),
        "router_bias": nrm(ks[24], (L, N_EXPERTS), 0.01),
        "exp_w_gate": nrm(ks[25], (L, N_EXPERTS, D_MODEL, EXPERT_HIDDEN), D_MODEL ** -0.5),
        "exp_w_up": nrm(ks[26], (L, N_EXPERTS, D_MODEL, EXPERT_HIDDEN), D_MODEL ** -0.5),
        "exp_w_down": nrm(ks[27], (L, N_EXPERTS, EXPERT_HIDDEN, D_MODEL), EXPERT_HIDDEN ** -0.5),
        "sh_w_gate": nrm(ks[28], (L, D_MODEL, EXPERT_HIDDEN), D_MODEL ** -0.5),
        "sh_w_up": nrm(ks[29], (L, D_MODEL, EXPERT_HIDDEN), D_MODEL ** -0.5),
        "sh_w_down": nrm(ks[30], (L, EXPERT_HIDDEN, D_MODEL), EXPERT_HIDDEN ** -0.5),
    }


def reference(x, c, ctx, c_ctx, ada_w, ada_b, norm_g, w_in, w_out, hy_conv, hy_w1, hy_b1, hy_w2, hy_b2,
              hy_w3, hy_freq, hy_skip, mla_q_norm, mla_kv_norm, mla_w_uq, mla_w_ukv, gqa_q_norm, gqa_k_norm,
              router_w, router_bias, exp_w_gate, exp_w_up, exp_w_down, sh_w_gate, sh_w_up, sh_w_down):
    bsz, n, d = x.shape
    n_ctx = ctx.shape[1]
    s_lat = jax.nn.silu(c)
    s_ctx = jax.nn.silu(c_ctx)[None]
    for l in range(DEPTH):
        last = l == DEPTH - 1
        sh_a, sc_a, g_a, sh_f, sc_f, g_f = adaln(s_lat, ada_w[l], ada_b[l])
        csh_a, csc_a, cg_a, csh_f, csc_f, cg_f = adaln(s_ctx, ada_w[l], ada_b[l])

        h_lat = modulate(rmsnorm(x, norm_g[l, 0]), sh_a, sc_a)
        h_ctx = modulate(rmsnorm(ctx, norm_g[l, 0]), csh_a, csc_a)
        mix_lat, mix_ctx = token_mixer(h_lat, h_ctx, not last, w_in[l], w_out[l], hy_conv[l], hy_w1[l],
                                       hy_b1[l], hy_w2[l], hy_b2[l], hy_w3[l], hy_freq[l], hy_skip[l],
                                       mla_q_norm[l], mla_kv_norm[l], mla_w_uq[l], mla_w_ukv[l],
                                       gqa_q_norm[l], gqa_k_norm[l])
        x = x + g_a * rmsnorm(mix_lat, norm_g[l, 1])

        f_lat = modulate(rmsnorm(x, norm_g[l, 2]), sh_f, sc_f).reshape(bsz * n, d)
        moe_args = (router_w[l], router_bias[l], exp_w_gate[l], exp_w_up[l], exp_w_down[l],
                    sh_w_gate[l], sh_w_up[l], sh_w_down[l])
        if last:
            y = moe_ffn(f_lat, *moe_args).reshape(bsz, n, d)
        else:
            ctx = ctx + cg_a * rmsnorm(mix_ctx, norm_g[l, 1])
            f_ctx = modulate(rmsnorm(ctx, norm_g[l, 2]), csh_f, csc_f).reshape(bsz * n_ctx, d)
            y_all = moe_ffn(jnp.concatenate([f_lat, f_ctx], axis=0), *moe_args)
            y = y_all[:bsz * n].reshape(bsz, n, d)
            ctx = ctx + cg_f * rmsnorm(y_all[bsz * n:].reshape(bsz, n_ctx, d), norm_g[l, 3])
        x = x + g_f * rmsnorm(y, norm_g[l, 3])
    return x
```

```python
import functools
import math

import jax
import jax.numpy as jnp
from jax import lax
from jax.experimental import pallas as pl
from jax.experimental.pallas import tpu as pltpu

F32 = jnp.float32
BF16 = jnp.bfloat16

D_MODEL = 2048
GRID_W = 64
EPS = 1e-6
N_MOD = 6
HY_W = D_MODEL // 4
HY_ORDER = 2
HY_BANDS = 16
HY_TARGET = 1e-2
HY_FAST = 0.3
HY_SLOW = 1.5
MLA_NOPE = 128
MLA_ROPE = 64
MLA_V = 128
MLA_HEADS = 6
MLA_Q_RANK = 768
MLA_KV_RANK = 256
GQA_HEAD_DIM = 128
GQA_HEADS = 6
GQA_KV_HEADS = 2
ROPE_THETA = 10000.0
N_EXPERTS = 64
TOP_K = 6
N_GROUPS = 8
TOPK_GROUPS = 4
EXPERT_HIDDEN = D_MODEL // 4
ROUTED_SCALE = 2.5

LANES = 128
VMEM_LIMIT_BYTES = 56 * 1024 * 1024

COL_HY = 0
COL_MQ = 3 * HY_W
COL_GQ = COL_MQ + MLA_Q_RANK
COL_CKV = COL_GQ + GQA_HEADS * GQA_HEAD_DIM
COL_GK = COL_CKV + MLA_KV_RANK
COL_GV = COL_GK + GQA_KV_HEADS * GQA_HEAD_DIM
COL_KR = COL_GV + GQA_KV_HEADS * GQA_HEAD_DIM
IN_COLS_PAD = COL_KR + LANES
MLA_QK_PAD = 2 * LANES
ROUTER_PAD = LANES
MOE_BLOCK = 256

NT_DIMS = (((1,), (1,)), ((), ()))


def _cparams(n_axes):
    return pltpu.CompilerParams(dimension_semantics=("arbitrary",) * n_axes,
                                vmem_limit_bytes=VMEM_LIMIT_BYTES)


def _resident(shape):
    nd = len(shape)
    return pl.BlockSpec(shape, lambda *_: (0,) * nd, pipeline_mode=pl.Buffered(1))


def _rms(x):
    return x * lax.rsqrt(jnp.mean(x * x, axis=-1, keepdims=True) + EPS)


def _silu(x):
    return x / (1.0 + jnp.exp(-x))


def _dot(a, b):
    return jnp.dot(a, b, preferred_element_type=F32)


def _dot_nt(a, b):
    return lax.dot_general(a, b, NT_DIMS, preferred_element_type=F32)


def _rope(x, c, s1, s2, shift):
    w = x.shape[-1]
    return x * c + pltpu.roll(x, w - shift, 1) * s1 + pltpu.roll(x, shift, 1) * s2


def _adaln_kernel(s_ref, w_ref, b_ref, o_ref):
    s = _silu(s_ref[...]).astype(BF16)
    o_ref[0] = _dot(s, w_ref[0].astype(BF16)) + b_ref[0]


def adaln_all(s_in, ada_w, ada_b):
    depth, d, n_out = ada_w.shape
    rows = s_in.shape[0]
    tn = 1024
    return pl.pallas_call(
        _adaln_kernel,
        grid=(depth, n_out // tn),
        in_specs=[pl.BlockSpec((rows, d), lambda l, j: (0, 0)),
                  pl.BlockSpec((1, d, tn), lambda l, j: (l, 0, j)),
                  pl.BlockSpec((1, 1, tn), lambda l, j: (l, 0, j))],
        out_specs=pl.BlockSpec((1, rows, tn), lambda l, j: (l, 0, j)),
        out_shape=jax.ShapeDtypeStruct((depth, rows, n_out), F32),
        compiler_params=_cparams(2),
        name="adaln",
    )(s_in, ada_w, ada_b.reshape(depth, 1, n_out))


def _in_proj_kernel(x_ref, g_ref, mod_ref, w_ref, o_ref):
    h = _rms(x_ref[...]) * g_ref[...]
    h = h * (1.0 + mod_ref[0, 1:2, :]) + mod_ref[0, 0:1, :]
    o_ref[...] = _dot(h.astype(BF16), w_ref[...]).astype(o_ref.dtype)


def in_proj(x_all, g, mod, w, tm, mod_idx):
    t, d = x_all.shape
    n_out = w.shape[1]
    return pl.pallas_call(
        _in_proj_kernel,
        grid=(t // tm,),
        in_specs=[pl.BlockSpec((tm, d), lambda i: (i, 0)),
                  pl.BlockSpec((1, d), lambda i: (0, 0)),
                  pl.BlockSpec((1, N_MOD, d), lambda i: (mod_idx(i), 0, 0)),
                  _resident((d, n_out))],
        out_specs=pl.BlockSpec((tm, n_out), lambda i: (i, 0)),
        out_shape=jax.ShapeDtypeStruct((t, n_out), BF16),
        compiler_params=_cparams(1),
        name="in_proj",
    )(x_all, g, mod, w)


def _mla_q_kernel(cq_ref, qn_ref, w_ref, c_ref, s1_ref, s2_ref, o_ref, *, scale):
    hn = (_rms(cq_ref[...].astype(F32)) * qn_ref[...]).astype(BF16)
    q = _dot(hn, w_ref[...]) * scale
    c, s1, s2 = c_ref[...], s1_ref[...], s2_ref[...]
    for h in range(MLA_HEADS):
        lo = h * MLA_QK_PAD
        o_ref[:, lo:lo + MLA_NOPE] = q[:, lo:lo + MLA_NOPE].astype(BF16)
        r = q[:, lo + MLA_NOPE:lo + MLA_QK_PAD]
        o_ref[:, lo + MLA_NOPE:lo + MLA_QK_PAD] = _rope(r, c, s1, s2, MLA_ROPE // 4).astype(BF16)


def mla_q(p, q_norm, w_uq, tabs, tm, rope_idx, scale):
    t = p.shape[0]
    n_out = MLA_HEADS * MLA_QK_PAD
    tab_spec = pl.BlockSpec((tm, LANES), lambda i: (rope_idx(i), 0))
    return pl.pallas_call(
        functools.partial(_mla_q_kernel, scale=scale),
        grid=(t // tm,),
        in_specs=[pl.BlockSpec((tm, MLA_Q_RANK), lambda i: (i, COL_MQ // MLA_Q_RANK)),
                  pl.BlockSpec((1, MLA_Q_RANK), lambda i: (0, 0)),
                  _resident((MLA_Q_RANK, n_out)),
                  tab_spec, tab_spec, tab_spec],
        out_specs=pl.BlockSpec((tm, n_out), lambda i: (i, 0)),
        out_shape=jax.ShapeDtypeStruct((t, n_out), BF16),
        compiler_params=_cparams(1),
        name="mla_q",
    )(p, q_norm, w_uq, *tabs)


def _mla_kv_kernel(ckv_ref, kr_ref, kvn_ref, w_ref, c_ref, s1_ref, s2_ref, k_ref, v_ref):
    hn = (_rms(ckv_ref[...].astype(F32)) * kvn_ref[...]).astype(BF16)
    kv = _dot(hn, w_ref[...])
    kr = _rope(kr_ref[...].astype(F32), c_ref[...], s1_ref[...], s2_ref[...],
               MLA_ROPE // 4).astype(BF16)
    for h in range(MLA_HEADS):
        lo = h * MLA_QK_PAD
        k_ref[:, lo:lo + MLA_NOPE] = kv[:, h * MLA_NOPE:(h + 1) * MLA_NOPE].astype(BF16)
        k_ref[:, lo + MLA_NOPE:lo + MLA_QK_PAD] = kr
    v_ref[...] = kv[:, MLA_HEADS * MLA_NOPE:].astype(BF16)


def mla_kv(p, kv_norm, w_ukv, tabs, tm, rope_idx):
    t = p.shape[0]
    nk = MLA_HEADS * MLA_QK_PAD
    nv = MLA_HEADS * MLA_V
    tab_spec = pl.BlockSpec((tm, LANES), lambda i: (rope_idx(i), 0))
    return pl.pallas_call(
        _mla_kv_kernel,
        grid=(t // tm,),
        in_specs=[pl.BlockSpec((tm, MLA_KV_RANK), lambda i: (i, COL_CKV // MLA_KV_RANK)),
                  pl.BlockSpec((tm, LANES), lambda i: (i, COL_KR // LANES)),
                  pl.BlockSpec((1, MLA_KV_RANK), lambda i: (0, 0)),
                  _resident((MLA_KV_RANK, MLA_HEADS * (MLA_NOPE + MLA_V))),
                  tab_spec, tab_spec, tab_spec],
        out_specs=[pl.BlockSpec((tm, nk), lambda i: (i, 0)),
                   pl.BlockSpec((tm, nv), lambda i: (i, 0))],
        out_shape=[jax.ShapeDtypeStruct((t, nk), BF16),
                   jax.ShapeDtypeStruct((t, nv), BF16)],
        compiler_params=_cparams(1),
        name="mla_kv",
    )(p, p, kv_norm, w_ukv, *tabs)


def _gqa_qk_kernel(q_ref, k_ref, qn_ref, kn_ref, c_ref, s1_ref, s2_ref, qo_ref, ko_ref, *, scale):
    c, s1, s2 = c_ref[...], s1_ref[...], s2_ref[...]
    for h in range(GQA_HEADS):
        sl = slice(h * GQA_HEAD_DIM, (h + 1) * GQA_HEAD_DIM)
        x = _rms(q_ref[:, sl].astype(F32)) * qn_ref[...]
        qo_ref[:, sl] = (_rope(x, c, s1, s2, GQA_HEAD_DIM // 4) * scale).astype(BF16)
    for g in range(GQA_KV_HEADS):
        sl = slice(g * GQA_HEAD_DIM, (g + 1) * GQA_HEAD_DIM)
        x = _rms(k_ref[:, sl].astype(F32)) * kn_ref[...]
        ko_ref[:, sl] = _rope(x, c, s1, s2, GQA_HEAD_DIM // 4).astype(BF16)


def gqa_qk(p, q_norm, k_norm, tabs, tm, rope_idx, scale):
    t = p.shape[0]
    nq = GQA_HEADS * GQA_HEAD_DIM
    nk = GQA_KV_HEADS * GQA_HEAD_DIM
    tab_spec = pl.BlockSpec((tm, LANES), lambda i: (rope_idx(i), 0))
    return pl.pallas_call(
        functools.partial(_gqa_qk_kernel, scale=scale),
        grid=(t // tm,),
        in_specs=[pl.BlockSpec((tm, nq), lambda i: (i, COL_GQ // nq)),
                  pl.BlockSpec((tm, nk), lambda i: (i, COL_GK // nk)),
                  pl.BlockSpec((1, GQA_HEAD_DIM), lambda i: (0, 0)),
                  pl.BlockSpec((1, GQA_HEAD_DIM), lambda i: (0, 0)),
                  tab_spec, tab_spec, tab_spec],
        out_specs=[pl.BlockSpec((tm, nq), lambda i: (i, 0)),
                   pl.BlockSpec((tm, nk), lambda i: (i, 0))],
        out_shape=[jax.ShapeDtypeStruct((t, nq), BF16),
                   jax.ShapeDtypeStruct((t, nk), BF16)],
        compiler_params=_cparams(1),
        name="gqa_qk",
    )(p, p, q_norm, k_norm, *tabs)


def _attn_kernel(*refs, n_parts):
    q_ref = refs[0]
    k_refs = refs[1:1 + n_parts]
    v_refs = refs[1 + n_parts:1 + 2 * n_parts]
    o_ref = refs[1 + 2 * n_parts]
    q = q_ref[...]
    scores = [_dot_nt(q, k[...]) for k in k_refs]
    m = functools.reduce(jnp.maximum, [jnp.max(s, axis=-1, keepdims=True) for s in scores])
    probs = [jnp.exp(s - m) for s in scores]
    denom = functools.reduce(jnp.add, [jnp.sum(p, axis=-1, keepdims=True) for p in probs])
    o = functools.reduce(jnp.add, [_dot(p.astype(BF16), v[...]) for p, v in zip(probs, v_refs)])
    o_ref[...] = (o / denom).astype(o_ref.dtype)


def attention(q, k, v, *, bsz, n_q, tq, q_row0, kv_parts, heads, kv_group, dk, dv, v_col0, name):
    nq_t = n_q // tq
    in_specs = [pl.BlockSpec((tq, dk), lambda b, h, i: (q_row0 // tq + b * nq_t + i, h))]
    for row0, n_rows in kv_parts:
        in_specs.append(pl.BlockSpec(
            (n_rows, dk), lambda b, h, i, row0=row0, n_rows=n_rows: (row0 // n_rows + b, h // kv_group)))
    for row0, n_rows in kv_parts:
        in_specs.append(pl.BlockSpec(
            (n_rows, dv),
            lambda b, h, i, row0=row0, n_rows=n_rows: (row0 // n_rows + b, v_col0 + h // kv_group)))
    n_parts = len(kv_parts)
    return pl.pallas_call(
        functools.partial(_attn_kernel, n_parts=n_parts),
        grid=(bsz, heads, nq_t),
        in_specs=in_specs,
        out_specs=pl.BlockSpec((tq, dv), lambda b, h, i: (b * nq_t + i, h)),
        out_shape=jax.ShapeDtypeStruct((bsz * n_q, heads * dv), BF16),
        compiler_params=_cparams(3),
        name=name,
    )(q, *([k] * n_parts), *([v] * n_parts))


def dft_tables(n):
    f = jnp.arange(n, dtype=jnp.int32)[:, None]
    s = jnp.arange(n, dtype=jnp.int32)[None, :]
    phase = ((2 * f + 1) * s) % (4 * n)
    ang = phase.astype(F32) * (2.0 * math.pi / (4 * n))
    return jnp.cos(ang).astype(BF16), jnp.sin(ang).astype(BF16)


def hyena_time_filters(n, w1, b1, w2, b2, w3, freq):
    t = jnp.linspace(0.0, 1.0, n, dtype=F32)[:, None]
    ang = 2.0 * math.pi * jnp.arange(n, dtype=F32)[:, None] / n
    bands = jnp.linspace(1e-4, HY_BANDS - 1, HY_BANDS, dtype=F32)
    feats = jnp.concatenate([t, jnp.cos(ang * bands), jnp.sin(ang * bands)], axis=-1)
    fr = freq.astype(F32)
    hp = lax.Precision.HIGHEST
    hdn = jnp.sin(fr * (jnp.dot(feats, w1, precision=hp) + b1))
    hdn = jnp.sin(fr * (jnp.dot(hdn, w2, precision=hp) + b2))
    h = jnp.dot(hdn, w3, precision=hp).reshape(n, HY_ORDER, 2, HY_W)
    deltas = jnp.abs(jnp.linspace(math.log(HY_TARGET) / HY_SLOW, math.log(HY_TARGET) / HY_FAST,
                                  HY_W, dtype=F32))
    h = h * jnp.exp(-t * deltas)[:, None, None, :]
    h_fwd = h[:, :, 0]
    h_bwd = h[:, :, 1].at[0].set(0.0)
    r = lax.rsqrt(jnp.sum(h_fwd * h_fwd, axis=0) + jnp.sum(h_bwd * h_bwd, axis=0) + EPS)
    kf = (h_fwd * r).reshape(n, HY_ORDER * HY_W)
    kb = (h_bwd * r).reshape(n, HY_ORDER * HY_W)
    return (kf + kb).T, (kb - kf).T


def _spectra_kernel(ks_ref, kd_ref, c_ref, s_ref, kr_ref, ki_ref, *, inv_n):
    kr_ref[...] = _dot_nt(ks_ref[...].astype(BF16), c_ref[...]) * inv_n
    ki_ref[...] = _dot_nt(kd_ref[...].astype(BF16), s_ref[...]) * inv_n


def hyena_spectra(ksum_t, kdiff_t, cmat, smat):
    rows, n = ksum_t.shape
    tr = 256
    spec = pl.BlockSpec((tr, n), lambda i: (i, 0))
    return pl.pallas_call(
        functools.partial(_spectra_kernel, inv_n=1.0 / n),
        grid=(rows // tr,),
        in_specs=[spec, spec, _resident((n, n)), _resident((n, n))],
        out_specs=[spec, spec],
        out_shape=[jax.ShapeDtypeStruct((rows, n), F32)] * 2,
        compiler_params=_cparams(1),
        name="hyena_spectra",
    )(ksum_t, kdiff_t, cmat, smat)


def _hyena_kernel(v_ref, x1_ref, x2_ref, wv_ref, w1_ref, w2_ref, skip_ref,
                  k1r_ref, k1i_ref, k2r_ref, k2i_ref, c_ref, s_ref, o_ref, *, n, fc):
    def short_conv(u_ref, w_ref):
        u = u_ref[0].astype(F32)
        pos = lax.broadcasted_iota(jnp.int32, u.shape, 1)
        prev = jnp.where(pos == 0, 0.0, pltpu.roll(u, 1, 1))
        nxt = jnp.where(pos == n - 1, 0.0, pltpu.roll(u, n - 1, 1))
        w = w_ref[...]
        return prev * w[:, 0:1] + u * w[:, 1:2] + nxt * w[:, 2:3]

    def long_conv(z, kr_ref, ki_ref, skip):
        zb = z.astype(BF16)
        y = z * skip
        for j in range(n // fc):
            sl = slice(j * fc, (j + 1) * fc)
            cj = c_ref[sl, :]
            sj = s_ref[sl, :]
            a = _dot_nt(zb, cj)
            b = _dot_nt(zb, sj)
            kr = kr_ref[:, sl]
            ki = ki_ref[:, sl]
            yr = (a * kr + b * ki).astype(BF16)
            yi = (a * ki - b * kr).astype(BF16)
            y = y + _dot(yr, cj) - _dot(yi, sj)
        return y

    v = short_conv(v_ref, wv_ref)
    x1 = short_conv(x1_ref, w1_ref)
    z = x1 * long_conv(v, k1r_ref, k1i_ref, skip_ref[:, 0:1])
    x2 = short_conv(x2_ref, w2_ref)
    o_ref[0] = (x2 * long_conv(z, k2r_ref, k2i_ref, skip_ref[:, 1:2])).astype(o_ref.dtype)


def hyena_mixer(u_t, conv_w_t, skip_t, kr, ki, cmat, smat):
    bsz, _, n = u_t.shape
    ch = 128
    nc = HY_W // ch
    fc = min(512, n)

    def u_spec(k):
        return pl.BlockSpec((1, ch, n), lambda c, b: (b, k * nc + c, 0))

    def w_spec(k):
        return pl.BlockSpec((ch, 3), lambda c, b: (k * nc + c, 0))

    def k_spec(order):
        return pl.BlockSpec((ch, n), lambda c, b: (order * nc + c, 0), pipeline_mode=pl.Buffered(1))

    return pl.pallas_call(
        functools.partial(_hyena_kernel, n=n, fc=fc),
        grid=(nc, bsz),
        in_specs=[u_spec(0), u_spec(1), u_spec(2), w_spec(0), w_spec(1), w_spec(2),
                  pl.BlockSpec((ch, HY_ORDER), lambda c, b: (c, 0)),
                  k_spec(0), k_spec(0), k_spec(1), k_spec(1),
                  _resident((n, n)), _resident((n, n))],
        out_specs=pl.BlockSpec((1, ch, n), lambda c, b: (b, c, 0)),
        out_shape=jax.ShapeDtypeStruct((bsz, HY_W, n), BF16),
        compiler_params=_cparams(2),
        name="hyena_n%d" % n,
    )(u_t, u_t, u_t, conv_w_t, conv_w_t, conv_w_t, skip_t, kr, ki, kr, ki, cmat, smat)


def _out_proj_kernel(ho_ref, mo_ref, go_ref, wh_ref, wm_ref, wg_ref, x_ref, g1_ref, g2_ref,
                     mod_ref, rwh_ref, rwl_ref, xo_ref, f_ref, lg_ref):
    mix = _dot(ho_ref[...], wh_ref[...]) + _dot(mo_ref[...], wm_ref[...]) + _dot(go_ref[...], wg_ref[...])
    xn = x_ref[...] + mod_ref[0, 2:3, :] * (_rms(mix) * g1_ref[...])
    xo_ref[...] = xn
    f = (_rms(xn) * g2_ref[...]) * (1.0 + mod_ref[0, 4:5, :]) + mod_ref[0, 3:4, :]
    fh = f.astype(BF16)
    fl = (f - fh.astype(F32)).astype(BF16)
    f_ref[...] = fh
    lg_ref[...] = _dot(fh, rwh_ref[...]) + _dot(fl, rwh_ref[...]) + _dot(fh, rwl_ref[...])


def out_proj(ho, mo, go, wh, wm, wg, x_all, g1, g2, mod, rwh, rwl, tm, mod_idx):
    t = ho.shape[0]
    d = D_MODEL
    row = lambda w: pl.BlockSpec((tm, w), lambda i: (i, 0))
    vec = pl.BlockSpec((1, d), lambda i: (0, 0))
    return pl.pallas_call(
        _out_proj_kernel,
        grid=(t // tm,),
        in_specs=[row(ho.shape[1]), row(mo.shape[1]), row(go.shape[1]),
                  _resident(wh.shape), _resident(wm.shape), _resident(wg.shape),
                  row(d), vec, vec,
                  pl.BlockSpec((1, N_MOD, d), lambda i: (mod_idx(i), 0, 0)),
                  _resident(rwh.shape), _resident(rwl.shape)],
        out_specs=[row(d), row(d), row(ROUTER_PAD)],
        out_shape=[jax.ShapeDtypeStruct((t, d), F32),
                   jax.ShapeDtypeStruct((t, d), BF16),
                   jax.ShapeDtypeStruct((t, ROUTER_PAD), F32)],
        compiler_params=_cparams(1),
        name="out_proj",
    )(ho, mo, go, wh, wm, wg, x_all, g1, g2, mod, rwh, rwl)


def _moe_kernel(be_ref, bv_ref, x_ref, gate_ref, wg_ref, wu_ref, wd_ref, o_ref):
    i = pl.program_id(0)

    @pl.when(bv_ref[i] != 0)
    def _():
        x = x_ref[...]
        a = _silu(_dot(x, wg_ref[0])) * _dot(x, wu_ref[0])
        y = _dot(a.astype(BF16), wd_ref[0])
        o_ref[...] = (y * gate_ref[...]).astype(o_ref.dtype)

    @pl.when(bv_ref[i] == 0)
    def _():
        o_ref[...] = jnp.zeros(o_ref.shape, o_ref.dtype)


def moe_experts(blk_expert, blk_valid, xs, gates, wg, wu, wd):
    n_rows, d = xs.shape
    hid = wg.shape[2]
    grid_spec = pltpu.PrefetchScalarGridSpec(
        num_scalar_prefetch=2,
        grid=(n_rows // MOE_BLOCK,),
        in_specs=[pl.BlockSpec((MOE_BLOCK, d), lambda i, be, bv: (i, 0)),
                  pl.BlockSpec((MOE_BLOCK, 1), lambda i, be, bv: (i, 0)),
                  pl.BlockSpec((1, d, hid), lambda i, be, bv: (be[i], 0, 0)),
                  pl.BlockSpec((1, d, hid), lambda i, be, bv: (be[i], 0, 0)),
                  pl.BlockSpec((1, hid, d), lambda i, be, bv: (be[i], 0, 0))],
        out_specs=pl.BlockSpec((MOE_BLOCK, d), lambda i, be, bv: (i, 0)),
    )
    return pl.pallas_call(
        _moe_kernel,
        grid_spec=grid_spec,
        out_shape=jax.ShapeDtypeStruct((n_rows, d), BF16),
        compiler_params=_cparams(1),
        name="moe_experts",
    )(blk_expert, blk_valid, xs, gates, wg, wu, wd)


def _shared_kernel(f_ref, yr_ref, x_ref, sg_ref, su_ref, sd_ref, g3_ref, mod_ref, o_ref):
    f = f_ref[...]
    a = _silu(_dot(f, sg_ref[...])) * _dot(f, su_ref[...])
    y = _dot(a.astype(BF16), sd_ref[...]) + yr_ref[...].astype(F32)
    o_ref[...] = x_ref[...] + mod_ref[0, 5:6, :] * (_rms(y) * g3_ref[...])


def shared_ffn(f, y_routed, x_all, sg, su, sd, g3, mod, tm, mod_idx):
    t, d = f.shape
    row = pl.BlockSpec((tm, d), lambda i: (i, 0))
    return pl.pallas_call(
        _shared_kernel,
        grid=(t // tm,),
        in_specs=[row, row, row, _resident(sg.shape), _resident(su.shape), _resident(sd.shape),
                  pl.BlockSpec((1, d), lambda i: (0, 0)),
                  pl.BlockSpec((1, N_MOD, d), lambda i: (mod_idx(i), 0, 0))],
        out_specs=row,
        out_shape=jax.ShapeDtypeStruct((t, d), F32),
        compiler_params=_cparams(1),
        name="shared_ffn",
    )(f, y_routed, x_all, sg, su, sd, g3, mod)


def rope_tables(n, rot_dim, tm):
    rows = n // GRID_W
    row = jnp.repeat(jnp.arange(rows, dtype=F32), GRID_W)
    col = jnp.tile(jnp.arange(GRID_W, dtype=F32), rows)
    axis_dim = rot_dim // 2
    inv = ROPE_THETA ** (-jnp.arange(0, axis_dim, 2, dtype=F32) / axis_dim)
    ar, ac = row[:, None] * inv, col[:, None] * inv
    zero = jnp.zeros_like(ar)
    c = jnp.concatenate([jnp.cos(ar), jnp.cos(ar), jnp.cos(ac), jnp.cos(ac)], axis=-1)
    s1 = jnp.concatenate([-jnp.sin(ar), zero, -jnp.sin(ac), zero], axis=-1)
    s2 = jnp.concatenate([zero, jnp.sin(ar), zero, jnp.sin(ac)], axis=-1)

    def finish(tab, fill):
        tab = jnp.pad(tab, ((0, 0), (0, LANES - rot_dim)), constant_values=fill)
        return jnp.concatenate([tab, jnp.full((tm, LANES), fill, F32)], axis=0)

    return finish(c, 1.0), finish(s1, 0.0), finish(s2, 0.0)


def pack_w_in(w):
    hy = 3 * HY_W
    kv0 = hy + MLA_Q_RANK + GQA_HEADS * GQA_HEAD_DIM
    ckv = w[:, kv0:kv0 + MLA_KV_RANK]
    kr = w[:, kv0 + MLA_KV_RANK:kv0 + MLA_KV_RANK + MLA_ROPE]
    gk0 = kv0 + MLA_KV_RANK + MLA_ROPE
    gkv = w[:, gk0:]
    kr = jnp.pad(kr, ((0, 0), (0, LANES - MLA_ROPE)))
    return jnp.concatenate([w[:, :kv0], ckv, gkv, kr], axis=1).astype(BF16)


def pack_w_uq(w):
    w = w.reshape(MLA_Q_RANK, MLA_HEADS, MLA_NOPE + MLA_ROPE)
    w = jnp.pad(w, ((0, 0), (0, 0), (0, MLA_QK_PAD - MLA_NOPE - MLA_ROPE)))
    return w.reshape(MLA_Q_RANK, MLA_HEADS * MLA_QK_PAD).astype(BF16)


def pack_w_ukv(w):
    w = w.reshape(MLA_KV_RANK, MLA_HEADS, MLA_NOPE + MLA_V)
    k = w[:, :, :MLA_NOPE].reshape(MLA_KV_RANK, MLA_HEADS * MLA_NOPE)
    v = w[:, :, MLA_NOPE:].reshape(MLA_KV_RANK, MLA_HEADS * MLA_V)
    return jnp.concatenate([k, v], axis=1).astype(BF16)


def route(logits, router_bias):
    n_tok = logits.shape[0]
    scores = jax.nn.sigmoid(logits)
    biased = scores + router_bias.astype(F32)
    grp = biased.reshape(n_tok, N_GROUPS, N_EXPERTS // N_GROUPS)
    grp_score = lax.top_k(grp, 2)[0].sum(-1)
    _, g_idx = lax.top_k(grp_score, TOPK_GROUPS)
    g_mask = jax.nn.one_hot(g_idx, N_GROUPS, dtype=F32).sum(1) > 0
    e_mask = jnp.repeat(g_mask, N_EXPERTS // N_GROUPS, axis=1)
    _, e_idx = lax.top_k(jnp.where(e_mask, biased, -jnp.inf), TOP_K)
    gates = jnp.take_along_axis(scores, e_idx, axis=1)
    gates = gates / jnp.sum(gates, axis=-1, keepdims=True) * ROUTED_SCALE
    return e_idx, gates


def dispatch_plan(e_idx, gates):
    n_tok = e_idx.shape[0]
    n_pairs = n_tok * TOP_K
    e_flat = e_idx.reshape(-1).astype(jnp.int32)
    order = jnp.argsort(e_flat).astype(jnp.int32)
    e_sorted = e_flat[order]
    experts = jnp.arange(N_EXPERTS, dtype=jnp.int32)
    starts = jnp.searchsorted(e_sorted, experts, side="left").astype(jnp.int32)
    ends = jnp.searchsorted(e_sorted, experts, side="right").astype(jnp.int32)
    counts = ends - starts
    padded = (counts + MOE_BLOCK - 1) // MOE_BLOCK * MOE_BLOCK
    p_ends = jnp.cumsum(padded)
    p_starts = p_ends - padded
    n_blocks = -(-n_pairs // MOE_BLOCK) + N_EXPERTS
    n_rows = n_blocks * MOE_BLOCK
    blk_start = jnp.arange(n_blocks, dtype=jnp.int32) * MOE_BLOCK
    blk_expert = jnp.minimum(jnp.searchsorted(p_ends, blk_start, side="right"),
                             N_EXPERTS - 1).astype(jnp.int32)
    blk_valid = (blk_start < p_ends[-1]).astype(jnp.int32)
    r = jnp.arange(n_rows, dtype=jnp.int32)
    r_exp = jnp.repeat(blk_expert, MOE_BLOCK)
    off = r - p_starts[r_exp]
    slot_valid = (off < counts[r_exp]) & (jnp.repeat(blk_valid, MOE_BLOCK) > 0)
    j = jnp.clip(starts[r_exp] + off, 0, n_pairs - 1)
    pair = order[j]
    slot_tok = jnp.where(slot_valid, pair // TOP_K, n_tok)
    slot_gate = jnp.where(slot_valid, gates.reshape(-1)[pair], 0.0)
    dest = p_starts[e_sorted] + jnp.arange(n_pairs, dtype=jnp.int32) - starts[e_sorted]
    rank = jnp.argsort(order).astype(jnp.int32)
    pos = dest[rank].reshape(n_tok, TOP_K)
    return blk_expert, blk_valid, slot_tok, slot_gate.reshape(n_rows, 1), pos


def _pick_tile(*sizes):
    for tile in (512, 256, 128):
        if all(s % tile == 0 for s in sizes):
            return tile
    raise ValueError("row counts must be multiples of 128: %r" % (sizes,))


def kernel(x, c, ctx, c_ctx, ada_w, ada_b, norm_g, w_in, w_out, hy_conv, hy_w1, hy_b1, hy_w2, hy_b2,
           hy_w3, hy_freq, hy_skip, mla_q_norm, mla_kv_norm, mla_w_uq, mla_w_ukv, gqa_q_norm, gqa_k_norm,
           router_w, router_bias, exp_w_gate, exp_w_up, exp_w_down, sh_w_gate, sh_w_up, sh_w_down):
    bsz, n, d = x.shape
    n_ctx = ctx.shape[1]
    depth = ada_w.shape[0]
    assert d == D_MODEL and n % GRID_W == 0
    t_lat, t_ctx = bsz * n, bsz * n_ctx
    assert t_lat % n_ctx == 0
    tm = _pick_tile(n, t_ctx)
    tq = _pick_tile(n)
    lat_tiles = t_lat // tm
    tiles_per_seq = n // tm

    def mod_idx(i):
        return jnp.minimum(i // tiles_per_seq, bsz)

    def rope_idx(i):
        return jnp.where(i < lat_tiles, i % tiles_per_seq, tiles_per_seq)

    x_all = jnp.concatenate([x.reshape(t_lat, d), ctx.reshape(t_ctx, d)], axis=0)
    mod_rows = -(-(bsz + 1) // 16) * 16
    s_in = jnp.concatenate([c, c_ctx[None], jnp.zeros((mod_rows - bsz - 1, d), F32)], axis=0)
    mod_all = adaln_all(s_in, ada_w, ada_b).reshape(depth, mod_rows, N_MOD, d)

    tabs_mla = rope_tables(n, MLA_ROPE, tm)
    tabs_gqa = rope_tables(n, GQA_HEAD_DIM, tm)
    mla_scale = (MLA_NOPE + MLA_ROPE) ** -0.5
    gqa_scale = GQA_HEAD_DIM ** -0.5
    dft_lat = dft_tables(n)
    dft_ctx = dft_tables(n_ctx)

    for l in range(depth):
        last = l == depth - 1
        mod = mod_all[l]
        vec = lambda a: a.reshape(1, -1)

        p = in_proj(x_all, vec(norm_g[l, 0]), mod, pack_w_in(w_in[l]), tm, mod_idx)
        q_m = mla_q(p, vec(mla_q_norm[l]), pack_w_uq(mla_w_uq[l]), tabs_mla, tm, rope_idx, mla_scale)
        k_m, v_m = mla_kv(p, vec(mla_kv_norm[l]), pack_w_ukv(mla_w_ukv[l]), tabs_mla, tm, rope_idx)
        q_g, k_g = gqa_qk(p, vec(gqa_q_norm[l]), vec(gqa_k_norm[l]), tabs_gqa, tm, rope_idx, gqa_scale)

        lat_parts = [(t_lat, n_ctx), (0, n)]
        mo = attention(q_m, k_m, v_m, bsz=bsz, n_q=n, tq=tq, q_row0=0, kv_parts=lat_parts,
                       heads=MLA_HEADS, kv_group=1, dk=MLA_QK_PAD, dv=MLA_V, v_col0=0, name="mla_attn")
        go = attention(q_g, k_g, p, bsz=bsz, n_q=n, tq=tq, q_row0=0, kv_parts=lat_parts,
                       heads=GQA_HEADS, kv_group=GQA_HEADS // GQA_KV_HEADS, dk=GQA_HEAD_DIM,
                       dv=GQA_HEAD_DIM, v_col0=COL_GV // LANES, name="gqa_attn")

        conv_w_t = hy_conv[l].T
        skip_t = hy_skip[l].T
        filt = (hy_w1[l], hy_b1[l], hy_w2[l], hy_b2[l], hy_w3[l], hy_freq[l])
        kr, ki = hyena_spectra(*hyena_time_filters(n, *filt), *dft_lat)
        u_t = p[:t_lat, :3 * HY_W].reshape(bsz, n, 3 * HY_W).transpose(0, 2, 1)
        ho = hyena_mixer(u_t, conv_w_t, skip_t, kr, ki, *dft_lat).transpose(0, 2, 1).reshape(t_lat, HY_W)

        if not last:
            ctx_parts = [(t_lat, n_ctx)]
            mo_c = attention(q_m, k_m, v_m, bsz=bsz, n_q=n_ctx, tq=n_ctx, q_row0=t_lat, kv_parts=ctx_parts,
                             heads=MLA_HEADS, kv_group=1, dk=MLA_QK_PAD, dv=MLA_V, v_col0=0,
                             name="mla_attn_ctx")
            go_c = attention(q_g, k_g, p, bsz=bsz, n_q=n_ctx, tq=n_ctx, q_row0=t_lat, kv_parts=ctx_parts,
                             heads=GQA_HEADS, kv_group=GQA_HEADS // GQA_KV_HEADS, dk=GQA_HEAD_DIM,
                             dv=GQA_HEAD_DIM, v_col0=COL_GV // LANES, name="gqa_attn_ctx")
            kr_c, ki_c = hyena_spectra(*hyena_time_filters(n_ctx, *filt), *dft_ctx)
            u_c = p[t_lat:, :3 * HY_W].reshape(bsz, n_ctx, 3 * HY_W).transpose(0, 2, 1)
            ho_c = hyena_mixer(u_c, conv_w_t, skip_t, kr_c, ki_c, *dft_ctx)
            ho_c = ho_c.transpose(0, 2, 1).reshape(t_ctx, HY_W)
            ho = jnp.concatenate([ho, ho_c], axis=0)
            mo = jnp.concatenate([mo, mo_c], axis=0)
            go = jnp.concatenate([go, go_c], axis=0)
            x_cur = x_all
        else:
            x_cur = x_all[:t_lat]

        wo = w_out[l].astype(BF16)
        rw = jnp.pad(router_w[l], ((0, 0), (0, ROUTER_PAD - N_EXPERTS)))
        rw_hi = rw.astype(BF16)
        rw_lo = (rw - rw_hi.astype(F32)).astype(BF16)
        x_mid, f, logits = out_proj(ho, mo, go, wo[:HY_W], wo[HY_W:HY_W + MLA_HEADS * MLA_V],
                                    wo[HY_W + MLA_HEADS * MLA_V:], x_cur, vec(norm_g[l, 1]),
                                    vec(norm_g[l, 2]), mod, rw_hi, rw_lo, tm, mod_idx)

        n_tok = f.shape[0]
        e_idx, gates = route(logits[:, :N_EXPERTS], router_bias[l])
        blk_expert, blk_valid, slot_tok, slot_gate, pos = dispatch_plan(e_idx, gates)
        f_pad = jnp.concatenate([f, jnp.zeros((1, d), f.dtype)], axis=0)
        ys = moe_experts(blk_expert, blk_valid, f_pad[slot_tok], slot_gate,
                         exp_w_gate[l].astype(BF16), exp_w_up[l].astype(BF16), exp_w_down[l].astype(BF16))
        y_routed = ys[pos].astype(F32).sum(axis=1)
        x_all = shared_ffn(f, y_routed, x_mid, sh_w_gate[l].astype(BF16), sh_w_up[l].astype(BF16),
                           sh_w_down[l].astype(BF16), vec(norm_g[l, 3]), mod, tm, mod_idx)

    return x_all[:t_lat].reshape(bsz, n, d)
```

```python
import functools
import math

import jax
import jax.numpy as jnp
from jax import lax
from jax.experimental import pallas as pl
from jax.experimental.pallas import tpu as pltpu

F32 = jnp.float32
BF16 = jnp.bfloat16

D_MODEL = 2048
GRID_W = 64
EPS = 1e-6
N_MOD = 6
HY_W = D_MODEL // 4
HY_ORDER = 2
HY_BANDS = 16
HY_TARGET = 1e-2
HY_FAST = 0.3
HY_SLOW = 1.5
MLA_NOPE = 128
MLA_ROPE = 64
MLA_V = 128
MLA_HEADS = 6
MLA_Q_RANK = 768
MLA_KV_RANK = 256
GQA_HEAD_DIM = 128
GQA_HEADS = 6
GQA_KV_HEADS = 2
ROPE_THETA = 10000.0
N_EXPERTS = 64
TOP_K = 6
N_GROUPS = 8
TOPK_GROUPS = 4
EXPERT_HIDDEN = D_MODEL // 4
ROUTED_SCALE = 2.5

LANES = 128
VMEM_LIMIT_BYTES = 56 * 1024 * 1024

COL_HY = 0
COL_MQ = 3 * HY_W
COL_GQ = COL_MQ + MLA_Q_RANK
COL_CKV = COL_GQ + GQA_HEADS * GQA_HEAD_DIM
COL_GK = COL_CKV + MLA_KV_RANK
COL_GV = COL_GK + GQA_KV_HEADS * GQA_HEAD_DIM
COL_KR = COL_GV + GQA_KV_HEADS * GQA_HEAD_DIM
IN_COLS_PAD = COL_KR + LANES
MLA_QK_PAD = 2 * LANES
MOE_BLOCK = 256

NT_DIMS = (((1,), (1,)), ((), ()))


def _cparams(n_axes):
    return pltpu.CompilerParams(dimension_semantics=("arbitrary",) * n_axes,
                                vmem_limit_bytes=VMEM_LIMIT_BYTES)


def _resident(shape):
    nd = len(shape)
    return pl.BlockSpec(shape, lambda *_: (0,) * nd, pipeline_mode=pl.Buffered(1))


def _rms(x):
    return x * lax.rsqrt(jnp.mean(x * x, axis=-1, keepdims=True) + EPS)


def _silu(x):
    return x / (1.0 + jnp.exp(-x))


def _dot(a, b):
    return jnp.dot(a, b, preferred_element_type=F32)


def _dot_nt(a, b):
    return lax.dot_general(a, b, NT_DIMS, preferred_element_type=F32)


def _rope(x, c, s1, s2, shift):
    w = x.shape[-1]
    return x * c + pltpu.roll(x, w - shift, 1) * s1 + pltpu.roll(x, shift, 1) * s2


def _adaln_kernel(s_ref, w_ref, b_ref, o_ref):
    s = _silu(s_ref[...]).astype(BF16)
    o_ref[0] = _dot(s, w_ref[0].astype(BF16)) + b_ref[0]


def adaln_all(s_in, ada_w, ada_b):
    depth, d, n_out = ada_w.shape
    rows = s_in.shape[0]
    tn = 1024
    return pl.pallas_call(
        _adaln_kernel,
        grid=(depth, n_out // tn),
        in_specs=[pl.BlockSpec((rows, d), lambda l, j: (0, 0)),
                  pl.BlockSpec((1, d, tn), lambda l, j: (l, 0, j)),
                  pl.BlockSpec((1, 1, tn), lambda l, j: (l, 0, j))],
        out_specs=pl.BlockSpec((1, rows, tn), lambda l, j: (l, 0, j)),
        out_shape=jax.ShapeDtypeStruct((depth, rows, n_out), F32),
        compiler_params=_cparams(2),
        name="adaln",
    )(s_in, ada_w, ada_b.reshape(depth, 1, n_out))


def _in_proj_kernel(x_ref, g_ref, mod_ref, w_ref, o_ref):
    h = _rms(x_ref[...]) * g_ref[...]
    h = h * (1.0 + mod_ref[0, 1:2, :]) + mod_ref[0, 0:1, :]
    o_ref[...] = _dot(h.astype(BF16), w_ref[...]).astype(o_ref.dtype)


def in_proj(x_all, g, mod, w, tm, mod_idx):
    t, d = x_all.shape
    n_out = w.shape[1]
    return pl.pallas_call(
        _in_proj_kernel,
        grid=(t // tm,),
        in_specs=[pl.BlockSpec((tm, d), lambda i: (i, 0)),
                  pl.BlockSpec((1, d), lambda i: (0, 0)),
                  pl.BlockSpec((1, N_MOD, d), lambda i: (mod_idx(i), 0, 0)),
                  _resident((d, n_out))],
        out_specs=pl.BlockSpec((tm, n_out), lambda i: (i, 0)),
        out_shape=jax.ShapeDtypeStruct((t, n_out), BF16),
        compiler_params=_cparams(1),
        name="in_proj",
    )(x_all, g, mod, w)


def _mla_q_kernel(cq_ref, qn_ref, w_ref, c_ref, s1_ref, s2_ref, o_ref, *, scale):
    hn = (_rms(cq_ref[...].astype(F32)) * qn_ref[...]).astype(BF16)
    q = _dot(hn, w_ref[...]) * scale
    c, s1, s2 = c_ref[...], s1_ref[...], s2_ref[...]
    for h in range(MLA_HEADS):
        lo = h * MLA_QK_PAD
        o_ref[:, lo:lo + MLA_NOPE] = q[:, lo:lo + MLA_NOPE].astype(BF16)
        r = q[:, lo + MLA_NOPE:lo + MLA_QK_PAD]
        o_ref[:, lo + MLA_NOPE:lo + MLA_QK_PAD] = _rope(r, c, s1, s2, MLA_ROPE // 4).astype(BF16)


def mla_q(p, q_norm, w_uq, tabs, tm, rope_idx, scale):
    t = p.shape[0]
    n_out = MLA_HEADS * MLA_QK_PAD
    tab_spec = pl.BlockSpec((tm, LANES), lambda i: (rope_idx(i), 0))
    return pl.pallas_call(
        functools.partial(_mla_q_kernel, scale=scale),
        grid=(t // tm,),
        in_specs=[pl.BlockSpec((tm, MLA_Q_RANK), lambda i: (i, COL_MQ // MLA_Q_RANK)),
                  pl.BlockSpec((1, MLA_Q_RANK), lambda i: (0, 0)),
                  _resident((MLA_Q_RANK, n_out)),
                  tab_spec, tab_spec, tab_spec],
        out_specs=pl.BlockSpec((tm, n_out), lambda i: (i, 0)),
        out_shape=jax.ShapeDtypeStruct((t, n_out), BF16),
        compiler_params=_cparams(1),
        name="mla_q",
    )(p, q_norm, w_uq, *tabs)


def _mla_kv_kernel(ckv_ref, kr_ref, kvn_ref, w_ref, c_ref, s1_ref, s2_ref, k_ref, v_ref):
    hn = (_rms(ckv_ref[...].astype(F32)) * kvn_ref[...]).astype(BF16)
    kv = _dot(hn, w_ref[...])
    kr = _rope(kr_ref[...].astype(F32), c_ref[...], s1_ref[...], s2_ref[...],
               MLA_ROPE // 4).astype(BF16)
    for h in range(MLA_HEADS):
        lo = h * MLA_QK_PAD
        k_ref[:, lo:lo + MLA_NOPE] = kv[:, h * MLA_NOPE:(h + 1) * MLA_NOPE].astype(BF16)
        k_ref[:, lo + MLA_NOPE:lo + MLA_QK_PAD] = kr
    v_ref[...] = kv[:, MLA_HEADS * MLA_NOPE:].astype(BF16)


def mla_kv(p, kv_norm, w_ukv, tabs, tm, rope_idx):
    t = p.shape[0]
    nk = MLA_HEADS * MLA_QK_PAD
    nv = MLA_HEADS * MLA_V
    tab_spec = pl.BlockSpec((tm, LANES), lambda i: (rope_idx(i), 0))
    return pl.pallas_call(
        _mla_kv_kernel,
        grid=(t // tm,),
        in_specs=[pl.BlockSpec((tm, MLA_KV_RANK), lambda i: (i, COL_CKV // MLA_KV_RANK)),
                  pl.BlockSpec((tm, LANES), lambda i: (i, COL_KR // LANES)),
                  pl.BlockSpec((1, MLA_KV_RANK), lambda i: (0, 0)),
                  _resident((MLA_KV_RANK, MLA_HEADS * (MLA_NOPE + MLA_V))),
                  tab_spec, tab_spec, tab_spec],
        out_specs=[pl.BlockSpec((tm, nk), lambda i: (i, 0)),
                   pl.BlockSpec((tm, nv), lambda i: (i, 0))],
        out_shape=[jax.ShapeDtypeStruct((t, nk), BF16),
                   jax.ShapeDtypeStruct((t, nv), BF16)],
        compiler_params=_cparams(1),
        name="mla_kv",
    )(p, p, kv_norm, w_ukv, *tabs)


def _gqa_qk_kernel(q_ref, k_ref, qn_ref, kn_ref, c_ref, s1_ref, s2_ref, qo_ref, ko_ref, *, scale):
    c, s1, s2 = c_ref[...], s1_ref[...], s2_ref[...]
    for h in range(GQA_HEADS):
        sl = slice(h * GQA_HEAD_DIM, (h + 1) * GQA_HEAD_DIM)
        x = _rms(q_ref[:, sl].astype(F32)) * qn_ref[...]
        qo_ref[:, sl] = (_rope(x, c, s1, s2, GQA_HEAD_DIM // 4) * scale).astype(BF16)
    for g in range(GQA_KV_HEADS):
        sl = slice(g * GQA_HEAD_DIM, (g + 1) * GQA_HEAD_DIM)
        x = _rms(k_ref[:, sl].astype(F32)) * kn_ref[...]
        ko_ref[:, sl] = _rope(x, c, s1, s2, GQA_HEAD_DIM // 4).astype(BF16)


def gqa_qk(p, q_norm, k_norm, tabs, tm, rope_idx, scale):
    t = p.shape[0]
    nq = GQA_HEADS * GQA_HEAD_DIM
    nk = GQA_KV_HEADS * GQA_HEAD_DIM
    tab_spec = pl.BlockSpec((tm, LANES), lambda i: (rope_idx(i), 0))
    return pl.pallas_call(
        functools.partial(_gqa_qk_kernel, scale=scale),
        grid=(t // tm,),
        in_specs=[pl.BlockSpec((tm, nq), lambda i: (i, COL_GQ // nq)),
                  pl.BlockSpec((tm, nk), lambda i: (i, COL_GK // nk)),
                  pl.BlockSpec((1, GQA_HEAD_DIM), lambda i: (0, 0)),
                  pl.BlockSpec((1, GQA_HEAD_DIM), lambda i: (0, 0)),
                  tab_spec, tab_spec, tab_spec],
        out_specs=[pl.BlockSpec((tm, nq), lambda i: (i, 0)),
                   pl.BlockSpec((tm, nk), lambda i: (i, 0))],
        out_shape=[jax.ShapeDtypeStruct((t, nq), BF16),
                   jax.ShapeDtypeStruct((t, nk), BF16)],
        compiler_params=_cparams(1),
        name="gqa_qk",
    )(p, p, q_norm, k_norm, *tabs)


def _attn_kernel(*refs, n_parts):
    q_ref = refs[0]
    k_refs = refs[1:1 + n_parts]
    v_refs = refs[1 + n_parts:1 + 2 * n_parts]
    o_ref = refs[1 + 2 * n_parts]
    q = q_ref[...]
    scores = [_dot_nt(q, k[...]) for k in k_refs]
    m = functools.reduce(jnp.maximum, [jnp.max(s, axis=-1, keepdims=True) for s in scores])
    probs = [jnp.exp(s - m) for s in scores]
    denom = functools.reduce(jnp.add, [jnp.sum(p, axis=-1, keepdims=True) for p in probs])
    o = functools.reduce(jnp.add, [_dot(p.astype(BF16), v[...]) for p, v in zip(probs, v_refs)])
    o_ref[...] = (o / denom).astype(o_ref.dtype)


def attention(q, k, v, *, bsz, n_q, tq, q_row0, kv_parts, heads, kv_group, dk, dv, v_col0, name):
    nq_t = n_q // tq
    in_specs = [pl.BlockSpec((tq, dk), lambda b, h, i: (q_row0 // tq + b * nq_t + i, h))]
    for row0, n_rows in kv_parts:
        in_specs.append(pl.BlockSpec(
            (n_rows, dk), lambda b, h, i, row0=row0, n_rows=n_rows: (row0 // n_rows + b, h // kv_group)))
    for row0, n_rows in kv_parts:
        in_specs.append(pl.BlockSpec(
            (n_rows, dv),
            lambda b, h, i, row0=row0, n_rows=n_rows: (row0 // n_rows + b, v_col0 + h // kv_group)))
    n_parts = len(kv_parts)
    return pl.pallas_call(
        functools.partial(_attn_kernel, n_parts=n_parts),
        grid=(bsz, heads, nq_t),
        in_specs=in_specs,
        out_specs=pl.BlockSpec((tq, dv), lambda b, h, i: (b * nq_t + i, h)),
        out_shape=jax.ShapeDtypeStruct((bsz * n_q, heads * dv), BF16),
        compiler_params=_cparams(3),
        name=name,
    )(q, *([k] * n_parts), *([v] * n_parts))


def dft_tables(n):
    f = jnp.arange(n, dtype=jnp.int32)[:, None]
    s = jnp.arange(n, dtype=jnp.int32)[None, :]
    phase = ((2 * f + 1) * s) % (4 * n)
    ang = phase.astype(F32) * (2.0 * math.pi / (4 * n))
    return jnp.cos(ang).astype(BF16), jnp.sin(ang).astype(BF16)


def hyena_time_filters(n, w1, b1, w2, b2, w3, freq):
    t = jnp.linspace(0.0, 1.0, n, dtype=F32)[:, None]
    ang = 2.0 * math.pi * jnp.arange(n, dtype=F32)[:, None] / n
    bands = jnp.linspace(1e-4, HY_BANDS - 1, HY_BANDS, dtype=F32)
    feats = jnp.concatenate([t, jnp.cos(ang * bands), jnp.sin(ang * bands)], axis=-1)
    fr = freq.astype(F32)
    hp = lax.Precision.HIGHEST
    hdn = jnp.sin(fr * (jnp.dot(feats, w1, precision=hp) + b1))
    hdn = jnp.sin(fr * (jnp.dot(hdn, w2, precision=hp) + b2))
    h = jnp.dot(hdn, w3, precision=hp).reshape(n, HY_ORDER, 2, HY_W)
    deltas = jnp.abs(jnp.linspace(math.log(HY_TARGET) / HY_SLOW, math.log(HY_TARGET) / HY_FAST,
                                  HY_W, dtype=F32))
    h = h * jnp.exp(-t * deltas)[:, None, None, :]
    h_fwd = h[:, :, 0]
    h_bwd = h[:, :, 1].at[0].set(0.0)
    r = lax.rsqrt(jnp.sum(h_fwd * h_fwd, axis=0) + jnp.sum(h_bwd * h_bwd, axis=0) + EPS)
    kf = (h_fwd * r).reshape(n, HY_ORDER * HY_W)
    kb = (h_bwd * r).reshape(n, HY_ORDER * HY_W)
    return (kf + kb).T, (kb - kf).T


def _spectra_kernel(ks_ref, kd_ref, c_ref, s_ref, kr_ref, ki_ref, *, inv_n):
    kr_ref[...] = _dot_nt(ks_ref[...].astype(BF16), c_ref[...]) * inv_n
    ki_ref[...] = _dot_nt(kd_ref[...].astype(BF16), s_ref[...]) * inv_n


def hyena_spectra(ksum_t, kdiff_t, cmat, smat):
    rows, n = ksum_t.shape
    tr = 256
    spec = pl.BlockSpec((tr, n), lambda i: (i, 0))
    return pl.pallas_call(
        functools.partial(_spectra_kernel, inv_n=1.0 / n),
        grid=(rows // tr,),
        in_specs=[spec, spec, _resident((n, n)), _resident((n, n))],
        out_specs=[spec, spec],
        out_shape=[jax.ShapeDtypeStruct((rows, n), F32)] * 2,
        compiler_params=_cparams(1),
        name="hyena_spectra",
    )(ksum_t, kdiff_t, cmat, smat)


def _hyena_kernel(v_ref, x1_ref, x2_ref, wv_ref, w1_ref, w2_ref, skip_ref,
                  k1r_ref, k1i_ref, k2r_ref, k2i_ref, c_ref, s_ref, o_ref, *, n, fc):
    def short_conv(u_ref, w_ref):
        u = u_ref[0].astype(F32)
        pos = lax.broadcasted_iota(jnp.int32, u.shape, 1)
        prev = jnp.where(pos == 0, 0.0, pltpu.roll(u, 1, 1))
        nxt = jnp.where(pos == n - 1, 0.0, pltpu.roll(u, n - 1, 1))
        w = w_ref[...]
        return prev * w[:, 0:1] + u * w[:, 1:2] + nxt * w[:, 2:3]

    def long_conv(z, kr_ref, ki_ref, skip):
        zb = z.astype(BF16)
        y = z * skip
        for j in range(n // fc):
            sl = slice(j * fc, (j + 1) * fc)
            cj = c_ref[sl, :]
            sj = s_ref[sl, :]
            a = _dot_nt(zb, cj)
            b = _dot_nt(zb, sj)
            kr = kr_ref[:, sl]
            ki = ki_ref[:, sl]
            yr = (a * kr + b * ki).astype(BF16)
            yi = (a * ki - b * kr).astype(BF16)
            y = y + _dot(yr, cj) - _dot(yi, sj)
        return y

    v = short_conv(v_ref, wv_ref)
    x1 = short_conv(x1_ref, w1_ref)
    z = x1 * long_conv(v, k1r_ref, k1i_ref, skip_ref[:, 0:1])
    x2 = short_conv(x2_ref, w2_ref)
    o_ref[0] = (x2 * long_conv(z, k2r_ref, k2i_ref, skip_ref[:, 1:2])).astype(o_ref.dtype)


def hyena_mixer(u_t, conv_w_t, skip_t, kr, ki, cmat, smat):
    bsz, _, n = u_t.shape
    ch = 128
    nc = HY_W // ch
    fc = min(512, n)

    def u_spec(k):
        return pl.BlockSpec((1, ch, n), lambda c, b: (b, k * nc + c, 0))

    def w_spec(k):
        return pl.BlockSpec((ch, 3), lambda c, b: (k * nc + c, 0))

    def k_spec(order):
        return pl.BlockSpec((ch, n), lambda c, b: (order * nc + c, 0), pipeline_mode=pl.Buffered(1))

    return pl.pallas_call(
        functools.partial(_hyena_kernel, n=n, fc=fc),
        grid=(nc, bsz),
        in_specs=[u_spec(0), u_spec(1), u_spec(2), w_spec(0), w_spec(1), w_spec(2),
                  pl.BlockSpec((ch, HY_ORDER), lambda c, b: (c, 0)),
                  k_spec(0), k_spec(0), k_spec(1), k_spec(1),
                  _resident((n, n)), _resident((n, n))],
        out_specs=pl.BlockSpec((1, ch, n), lambda c, b: (b, c, 0)),
        out_shape=jax.ShapeDtypeStruct((bsz, HY_W, n), BF16),
        compiler_params=_cparams(2),
        name="hyena_n%d" % n,
    )(u_t, u_t, u_t, conv_w_t, conv_w_t, conv_w_t, skip_t, kr, ki, kr, ki, cmat, smat)


def _out_proj_kernel(ho_ref, mo_ref, go_ref, wh_ref, wm_ref, wg_ref, x_ref, g1_ref, g2_ref,
                     mod_ref, rwh_ref, rwl_ref, xo_ref, f_ref, lg_ref):
    mix = _dot(ho_ref[...], wh_ref[...]) + _dot(mo_ref[...], wm_ref[...]) + _dot(go_ref[...], wg_ref[...])
    xn = x_ref[...] + mod_ref[0, 2:3, :] * (_rms(mix) * g1_ref[...])
    xo_ref[...] = xn
    f = (_rms(xn) * g2_ref[...]) * (1.0 + mod_ref[0, 4:5, :]) + mod_ref[0, 3:4, :]
    fh = f.astype(BF16)
    fl = (f - fh.astype(F32)).astype(BF16)
    f_ref[...] = fh
    lg_ref[...] = _dot_nt(rwh_ref[...], fh) + _dot_nt(rwh_ref[...], fl) + _dot_nt(rwl_ref[...], fh)


def out_proj(ho, mo, go, wh, wm, wg, x_all, g1, g2, mod, rwh, rwl, tm, mod_idx):
    t = ho.shape[0]
    d = D_MODEL
    row = lambda w: pl.BlockSpec((tm, w), lambda i: (i, 0))
    vec = pl.BlockSpec((1, d), lambda i: (0, 0))
    return pl.pallas_call(
        _out_proj_kernel,
        grid=(t // tm,),
        in_specs=[row(ho.shape[1]), row(mo.shape[1]), row(go.shape[1]),
                  _resident(wh.shape), _resident(wm.shape), _resident(wg.shape),
                  row(d), vec, vec,
                  pl.BlockSpec((1, N_MOD, d), lambda i: (mod_idx(i), 0, 0)),
                  _resident(rwh.shape), _resident(rwl.shape)],
        out_specs=[row(d), row(d), pl.BlockSpec((N_EXPERTS, tm), lambda i: (0, i))],
        out_shape=[jax.ShapeDtypeStruct((t, d), F32),
                   jax.ShapeDtypeStruct((t, d), BF16),
                   jax.ShapeDtypeStruct((N_EXPERTS, t), F32)],
        compiler_params=_cparams(1),
        name="out_proj",
    )(ho, mo, go, wh, wm, wg, x_all, g1, g2, mod, rwh, rwl)


def _moe_kernel(be_ref, bsrc_ref, bflag_ref, x_ref, wg_ref, wu_ref, wd_ref, o_ref, wg_s, wu_s, wd_s):
    i = pl.program_id(0)

    @pl.when(bflag_ref[i] == 2)
    def _():
        wg_s[...] = wg_ref[0].astype(BF16)
        wu_s[...] = wu_ref[0].astype(BF16)
        wd_s[...] = wd_ref[0].astype(BF16)

    @pl.when(bflag_ref[i] != 0)
    def _():
        x = x_ref[...]
        a = _silu(_dot(x, wg_s[...])) * _dot(x, wu_s[...])
        o_ref[...] = _dot(a.astype(BF16), wd_s[...]).astype(o_ref.dtype)

    @pl.when(bflag_ref[i] == 0)
    def _():
        o_ref[...] = jnp.zeros(o_ref.shape, o_ref.dtype)


def moe_experts(blk_expert, blk_src, blk_flag, xs, wg, wu, wd):
    n_rows, d = xs.shape
    hid = wg.shape[2]
    grid_spec = pltpu.PrefetchScalarGridSpec(
        num_scalar_prefetch=3,
        grid=(n_rows // MOE_BLOCK,),
        in_specs=[pl.BlockSpec((MOE_BLOCK, d), lambda i, be, bs, bf: (bs[i], 0)),
                  pl.BlockSpec((1, d, hid), lambda i, be, bs, bf: (be[i], 0, 0)),
                  pl.BlockSpec((1, d, hid), lambda i, be, bs, bf: (be[i], 0, 0)),
                  pl.BlockSpec((1, hid, d), lambda i, be, bs, bf: (be[i], 0, 0))],
        out_specs=pl.BlockSpec((MOE_BLOCK, d), lambda i, be, bs, bf: (i, 0)),
        scratch_shapes=[pltpu.VMEM((d, hid), BF16), pltpu.VMEM((d, hid), BF16),
                        pltpu.VMEM((hid, d), BF16)],
    )
    return pl.pallas_call(
        _moe_kernel,
        grid_spec=grid_spec,
        out_shape=jax.ShapeDtypeStruct((n_rows, d), BF16),
        compiler_params=_cparams(1),
        name="moe_experts",
    )(blk_expert, blk_src, blk_flag, xs, wg, wu, wd)


def _shared_kernel(f_ref, yg_ref, gate_ref, x_ref, sg_ref, su_ref, sd_ref, g3_ref, mod_ref, o_ref):
    f = f_ref[...]
    a = _silu(_dot(f, sg_ref[...])) * _dot(f, su_ref[...])
    y = _dot(a.astype(BF16), sd_ref[...])
    gates = gate_ref[...]
    for k in range(TOP_K):
        y = y + gates[:, k:k + 1] * yg_ref[k].astype(F32)
    o_ref[...] = x_ref[...] + mod_ref[0, 5:6, :] * (_rms(y) * g3_ref[...])


def shared_ffn(f, y_gathered, gates, x_all, sg, su, sd, g3, mod, tm, mod_idx):
    t, d = f.shape
    row = pl.BlockSpec((tm, d), lambda i: (i, 0))
    return pl.pallas_call(
        _shared_kernel,
        grid=(t // tm,),
        in_specs=[row, pl.BlockSpec((TOP_K, tm, d), lambda i: (0, i, 0)),
                  pl.BlockSpec((tm, gates.shape[1]), lambda i: (i, 0)), row,
                  _resident(sg.shape), _resident(su.shape), _resident(sd.shape),
                  pl.BlockSpec((1, d), lambda i: (0, 0)),
                  pl.BlockSpec((1, N_MOD, d), lambda i: (mod_idx(i), 0, 0))],
        out_specs=row,
        out_shape=jax.ShapeDtypeStruct((t, d), F32),
        compiler_params=_cparams(1),
        name="shared_ffn",
    )(f, y_gathered, gates, x_all, sg, su, sd, g3, mod)


def _first_max(vals, idx, n):
    m = jnp.max(vals, axis=0, keepdims=True)
    return m, jnp.min(jnp.where(vals == m, idx, float(n)), axis=0, keepdims=True)


def _router_kernel(lg_ref, bias_ref, e_ref, g_ref, r_ref, cnt_ref, carry_ref):
    @pl.when(pl.program_id(0) == 0)
    def _():
        carry_ref[...] = jnp.zeros(carry_ref.shape, F32)

    lg = lg_ref[...]
    tt = lg.shape[1]
    gsz = N_EXPERTS // N_GROUPS
    neg = -jnp.inf
    scores = 1.0 / (1.0 + jnp.exp(-lg))
    biased = scores + bias_ref[...]
    sub = lax.broadcasted_iota(jnp.int32, (gsz, tt), 0).astype(F32)

    grp_rows = []
    for g in range(N_GROUPS):
        blk = biased[g * gsz:(g + 1) * gsz, :]
        m1, i1 = _first_max(blk, sub, gsz)
        m2 = jnp.max(jnp.where(sub == i1, neg, blk), axis=0, keepdims=True)
        grp_rows.append(m1 + m2)
    cur = jnp.concatenate(grp_rows, axis=0)
    gidx = lax.broadcasted_iota(jnp.int32, (N_GROUPS, tt), 0).astype(F32)
    gsel = jnp.zeros((N_GROUPS, tt), F32)
    for _ in range(TOPK_GROUPS):
        _, gi = _first_max(cur, gidx, N_GROUPS)
        hit = gidx == gi
        gsel = jnp.where(hit, 1.0, gsel)
        cur = jnp.where(hit, neg, cur)
    emask = jnp.concatenate([jnp.broadcast_to(gsel[g:g + 1, :], (gsz, tt)) for g in range(N_GROUPS)], axis=0)

    cand = jnp.where(emask > 0.0, biased, neg)
    eidx = lax.broadcasted_iota(jnp.int32, (N_EXPERTS, tt), 0).astype(F32)
    chosen = jnp.zeros((N_EXPERTS, tt), F32)
    e_rows, g_rows = [], []
    for _ in range(TOP_K):
        _, ei = _first_max(cand, eidx, N_EXPERTS)
        hit = eidx == ei
        e_rows.append(ei)
        g_rows.append(jnp.sum(jnp.where(hit, scores, 0.0), axis=0, keepdims=True))
        chosen = jnp.where(hit, 1.0, chosen)
        cand = jnp.where(hit, neg, cand)
    gsum = functools.reduce(jnp.add, g_rows)
    g_rows = [g / gsum * ROUTED_SCALE for g in g_rows]

    before = (lax.broadcasted_iota(jnp.int32, (tt, tt), 0) < lax.broadcasted_iota(jnp.int32, (tt, tt), 1))
    prefix = _dot(chosen.astype(BF16), jnp.where(before, 1.0, 0.0).astype(BF16))
    rank_all = prefix + carry_ref[...]
    r_rows = [jnp.sum(jnp.where(eidx == ei, rank_all, 0.0), axis=0, keepdims=True) for ei in e_rows]
    carry_ref[...] = carry_ref[...] + jnp.sum(chosen, axis=1, keepdims=True)
    cnt_ref[...] = carry_ref[...]

    pad = [jnp.zeros((8 - TOP_K, tt), F32)]
    e_ref[...] = jnp.concatenate(e_rows + pad, axis=0).astype(jnp.int32)
    g_ref[...] = jnp.concatenate(g_rows + pad, axis=0)
    r_ref[...] = jnp.concatenate(r_rows + pad, axis=0).astype(jnp.int32)


def router(logits_t, bias, tt):
    n_exp, t = logits_t.shape
    col = pl.BlockSpec((8, tt), lambda i: (0, i))
    return pl.pallas_call(
        _router_kernel,
        grid=(t // tt,),
        in_specs=[pl.BlockSpec((n_exp, tt), lambda i: (0, i)),
                  pl.BlockSpec((n_exp, 1), lambda i: (0, 0))],
        out_specs=[col, col, col, pl.BlockSpec((n_exp, 1), lambda i: (0, 0))],
        out_shape=[jax.ShapeDtypeStruct((8, t), jnp.int32), jax.ShapeDtypeStruct((8, t), F32),
                   jax.ShapeDtypeStruct((8, t), jnp.int32), jax.ShapeDtypeStruct((n_exp, 1), F32)],
        scratch_shapes=[pltpu.VMEM((n_exp, 1), F32)],
        compiler_params=_cparams(1),
        name="router",
    )(logits_t, bias.reshape(n_exp, 1).astype(F32))


def rope_tables(n, rot_dim, tm):
    rows = n // GRID_W
    row = jnp.repeat(jnp.arange(rows, dtype=F32), GRID_W)
    col = jnp.tile(jnp.arange(GRID_W, dtype=F32), rows)
    axis_dim = rot_dim // 2
    inv = ROPE_THETA ** (-jnp.arange(0, axis_dim, 2, dtype=F32) / axis_dim)
    ar, ac = row[:, None] * inv, col[:, None] * inv
    zero = jnp.zeros_like(ar)
    c = jnp.concatenate([jnp.cos(ar), jnp.cos(ar), jnp.cos(ac), jnp.cos(ac)], axis=-1)
    s1 = jnp.concatenate([-jnp.sin(ar), zero, -jnp.sin(ac), zero], axis=-1)
    s2 = jnp.concatenate([zero, jnp.sin(ar), zero, jnp.sin(ac)], axis=-1)

    def finish(tab, fill):
        tab = jnp.pad(tab, ((0, 0), (0, LANES - rot_dim)), constant_values=fill)
        return jnp.concatenate([tab, jnp.full((tm, LANES), fill, F32)], axis=0)

    return finish(c, 1.0), finish(s1, 0.0), finish(s2, 0.0)


def pack_w_in(w):
    hy = 3 * HY_W
    kv0 = hy + MLA_Q_RANK + GQA_HEADS * GQA_HEAD_DIM
    ckv = w[:, kv0:kv0 + MLA_KV_RANK]
    kr = w[:, kv0 + MLA_KV_RANK:kv0 + MLA_KV_RANK + MLA_ROPE]
    gk0 = kv0 + MLA_KV_RANK + MLA_ROPE
    gkv = w[:, gk0:]
    kr = jnp.pad(kr, ((0, 0), (0, LANES - MLA_ROPE)))
    return jnp.concatenate([w[:, :kv0], ckv, gkv, kr], axis=1).astype(BF16)


def pack_w_uq(w):
    w = w.reshape(MLA_Q_RANK, MLA_HEADS, MLA_NOPE + MLA_ROPE)
    w = jnp.pad(w, ((0, 0), (0, 0), (0, MLA_QK_PAD - MLA_NOPE - MLA_ROPE)))
    return w.reshape(MLA_Q_RANK, MLA_HEADS * MLA_QK_PAD).astype(BF16)


def pack_w_ukv(w):
    w = w.reshape(MLA_KV_RANK, MLA_HEADS, MLA_NOPE + MLA_V)
    k = w[:, :, :MLA_NOPE].reshape(MLA_KV_RANK, MLA_HEADS * MLA_NOPE)
    v = w[:, :, MLA_NOPE:].reshape(MLA_KV_RANK, MLA_HEADS * MLA_V)
    return jnp.concatenate([k, v], axis=1).astype(BF16)


def dispatch_plan(e_t, rank_t, counts):
    n_tok = e_t.shape[1]
    n_pairs = n_tok * TOP_K
    counts = counts.astype(jnp.int32)
    padded = (counts + MOE_BLOCK - 1) // MOE_BLOCK * MOE_BLOCK
    p_ends = jnp.cumsum(padded)
    p_starts = p_ends - padded
    n_blocks = -(-n_pairs // MOE_BLOCK) + N_EXPERTS
    n_rows = n_blocks * MOE_BLOCK
    blk = jnp.arange(n_blocks, dtype=jnp.int32)
    n_used = p_ends[-1] // MOE_BLOCK
    blk_src = jnp.minimum(blk, n_used - 1)
    blk_expert = jnp.sum((blk_src * MOE_BLOCK)[:, None] >= p_ends[None, :], axis=1).astype(jnp.int32)
    first = jnp.concatenate([jnp.ones((1,), bool), blk_expert[1:] != blk_expert[:-1]])
    blk_flag = jnp.where(blk < n_used, jnp.where(first, 2, 1), 0).astype(jnp.int32)

    experts = jnp.arange(N_EXPERTS, dtype=jnp.int32)
    onehot = e_t[:, :, None] == experts
    pos = jnp.sum(jnp.where(onehot, p_starts, 0), axis=-1) + rank_t

    stride = n_tok + 1
    tok = jnp.arange(n_tok, dtype=jnp.int32)
    real_keys = (e_t * stride + tok[None, :]).reshape(-1)
    cum_fill = jnp.cumsum(padded - counts)
    filler = jnp.arange(n_rows - n_pairs, dtype=jnp.int32)
    filler_exp = jnp.sum(filler[:, None] >= cum_fill[None, :], axis=1).astype(jnp.int32)
    filler_keys = filler_exp * stride + n_tok
    slot_tok = jnp.sort(jnp.concatenate([real_keys, filler_keys])) % stride
    return blk_expert, blk_src, blk_flag, slot_tok, pos


def _pick_tile(*sizes):
    for tile in (512, 256, 128):
        if all(s % tile == 0 for s in sizes):
            return tile
    raise ValueError("row counts must be multiples of 128: %r" % (sizes,))


def kernel(x, c, ctx, c_ctx, ada_w, ada_b, norm_g, w_in, w_out, hy_conv, hy_w1, hy_b1, hy_w2, hy_b2,
           hy_w3, hy_freq, hy_skip, mla_q_norm, mla_kv_norm, mla_w_uq, mla_w_ukv, gqa_q_norm, gqa_k_norm,
           router_w, router_bias, exp_w_gate, exp_w_up, exp_w_down, sh_w_gate, sh_w_up, sh_w_down):
    bsz, n, d = x.shape
    n_ctx = ctx.shape[1]
    depth = ada_w.shape[0]
    assert d == D_MODEL and n % GRID_W == 0
    t_lat, t_ctx = bsz * n, bsz * n_ctx
    assert t_lat % n_ctx == 0
    tm = _pick_tile(n, t_ctx)
    tq = _pick_tile(n)
    lat_tiles = t_lat // tm
    tiles_per_seq = n // tm

    tm_ffn = min(tm, 256)

    def mod_idx(tile):
        return lambda i: jnp.minimum(i // (n // tile), bsz)

    def rope_idx(i):
        return jnp.where(i < lat_tiles, i % tiles_per_seq, tiles_per_seq)

    x_all = jnp.concatenate([x.reshape(t_lat, d), ctx.reshape(t_ctx, d)], axis=0)
    mod_rows = -(-(bsz + 1) // 16) * 16
    s_in = jnp.concatenate([c, c_ctx[None], jnp.zeros((mod_rows - bsz - 1, d), F32)], axis=0)
    mod_all = adaln_all(s_in, ada_w, ada_b).reshape(depth, mod_rows, N_MOD, d)

    tabs_mla = rope_tables(n, MLA_ROPE, tm)
    tabs_gqa = rope_tables(n, GQA_HEAD_DIM, tm)
    mla_scale = (MLA_NOPE + MLA_ROPE) ** -0.5
    gqa_scale = GQA_HEAD_DIM ** -0.5
    dft_lat = dft_tables(n)
    dft_ctx = dft_tables(n_ctx)

    for l in range(depth):
        last = l == depth - 1
        mod = mod_all[l]
        vec = lambda a: a.reshape(1, -1)

        p = in_proj(x_all, vec(norm_g[l, 0]), mod, pack_w_in(w_in[l]), tm, mod_idx(tm))
        q_m = mla_q(p, vec(mla_q_norm[l]), pack_w_uq(mla_w_uq[l]), tabs_mla, tm, rope_idx, mla_scale)
        k_m, v_m = mla_kv(p, vec(mla_kv_norm[l]), pack_w_ukv(mla_w_ukv[l]), tabs_mla, tm, rope_idx)
        q_g, k_g = gqa_qk(p, vec(gqa_q_norm[l]), vec(gqa_k_norm[l]), tabs_gqa, tm, rope_idx, gqa_scale)

        lat_parts = [(t_lat, n_ctx), (0, n)]
        mo = attention(q_m, k_m, v_m, bsz=bsz, n_q=n, tq=tq, q_row0=0, kv_parts=lat_parts,
                       heads=MLA_HEADS, kv_group=1, dk=MLA_QK_PAD, dv=MLA_V, v_col0=0, name="mla_attn")
        go = attention(q_g, k_g, p, bsz=bsz, n_q=n, tq=tq, q_row0=0, kv_parts=lat_parts,
                       heads=GQA_HEADS, kv_group=GQA_HEADS // GQA_KV_HEADS, dk=GQA_HEAD_DIM,
                       dv=GQA_HEAD_DIM, v_col0=COL_GV // LANES, name="gqa_attn")

        conv_w_t = hy_conv[l].T
        skip_t = hy_skip[l].T
        filt = (hy_w1[l], hy_b1[l], hy_w2[l], hy_b2[l], hy_w3[l], hy_freq[l])
        kr, ki = hyena_spectra(*hyena_time_filters(n, *filt), *dft_lat)
        u_t = p[:t_lat, :3 * HY_W].reshape(bsz, n, 3 * HY_W).transpose(0, 2, 1)
        ho = hyena_mixer(u_t, conv_w_t, skip_t, kr, ki, *dft_lat).transpose(0, 2, 1).reshape(t_lat, HY_W)

        if not last:
            ctx_parts = [(t_lat, n_ctx)]
            mo_c = attention(q_m, k_m, v_m, bsz=bsz, n_q=n_ctx, tq=n_ctx, q_row0=t_lat, kv_parts=ctx_parts,
                             heads=MLA_HEADS, kv_group=1, dk=MLA_QK_PAD, dv=MLA_V, v_col0=0,
                             name="mla_attn_ctx")
            go_c = attention(q_g, k_g, p, bsz=bsz, n_q=n_ctx, tq=n_ctx, q_row0=t_lat, kv_parts=ctx_parts,
                             heads=GQA_HEADS, kv_group=GQA_HEADS // GQA_KV_HEADS, dk=GQA_HEAD_DIM,
                             dv=GQA_HEAD_DIM, v_col0=COL_GV // LANES, name="gqa_attn_ctx")
            kr_c, ki_c = hyena_spectra(*hyena_time_filters(n_ctx, *filt), *dft_ctx)
            u_c = p[t_lat:, :3 * HY_W].reshape(bsz, n_ctx, 3 * HY_W).transpose(0, 2, 1)
            ho_c = hyena_mixer(u_c, conv_w_t, skip_t, kr_c, ki_c, *dft_ctx)
            ho_c = ho_c.transpose(0, 2, 1).reshape(t_ctx, HY_W)
            ho = jnp.concatenate([ho, ho_c], axis=0)
            mo = jnp.concatenate([mo, mo_c], axis=0)
            go = jnp.concatenate([go, go_c], axis=0)
            x_cur = x_all
        else:
            x_cur = x_all[:t_lat]

        wo = w_out[l].astype(BF16)
        rw_t = router_w[l].T
        rw_hi = rw_t.astype(BF16)
        rw_lo = (rw_t - rw_hi.astype(F32)).astype(BF16)
        x_mid, f, logits_t = out_proj(ho, mo, go, wo[:HY_W], wo[HY_W:HY_W + MLA_HEADS * MLA_V],
                                      wo[HY_W + MLA_HEADS * MLA_V:], x_cur, vec(norm_g[l, 1]),
                                      vec(norm_g[l, 2]), mod, rw_hi, rw_lo, tm, mod_idx(tm))

        e_t, gate_t, rank_t, counts = router(logits_t, router_bias[l], tm)
        blk_expert, blk_src, blk_flag, slot_tok, pos = dispatch_plan(e_t[:TOP_K], rank_t[:TOP_K], counts[:, 0])
        f_pad = jnp.concatenate([f, jnp.zeros((1, d), f.dtype)], axis=0)
        ys = moe_experts(blk_expert, blk_src, blk_flag, f_pad[slot_tok],
                         exp_w_gate[l], exp_w_up[l], exp_w_down[l])
        x_all = shared_ffn(f, ys[pos], gate_t.T, x_mid, sh_w_gate[l].astype(BF16), sh_w_up[l].astype(BF16),
                           sh_w_down[l].astype(BF16), vec(norm_g[l, 3]), mod, tm_ffn, mod_idx(tm_ffn))

    return x_all[:t_lat].reshape(bsz, n, d)
```

```python
import functools
import math

import jax
import jax.numpy as jnp
from jax import lax
from jax.experimental import pallas as pl
from jax.experimental.pallas import tpu as pltpu

F32 = jnp.float32
BF16 = jnp.bfloat16

D_MODEL = 2048
GRID_W = 64
EPS = 1e-6
N_MOD = 6
HY_W = D_MODEL // 4
HY_ORDER = 2
HY_BANDS = 16
HY_TARGET = 1e-2
HY_FAST = 0.3
HY_SLOW = 1.5
MLA_NOPE = 128
MLA_ROPE = 64
MLA_V = 128
MLA_HEADS = 6
MLA_Q_RANK = 768
MLA_KV_RANK = 256
GQA_HEAD_DIM = 128
GQA_HEADS = 6
GQA_KV_HEADS = 2
ROPE_THETA = 10000.0
N_EXPERTS = 64
TOP_K = 6
N_GROUPS = 8
TOPK_GROUPS = 4
EXPERT_HIDDEN = D_MODEL // 4
ROUTED_SCALE = 2.5

LANES = 128
VMEM_LIMIT_BYTES = 56 * 1024 * 1024

COL_HY = 0
COL_MQ = 3 * HY_W
COL_GQ = COL_MQ + MLA_Q_RANK
COL_CKV = COL_GQ + GQA_HEADS * GQA_HEAD_DIM
COL_GK = COL_CKV + MLA_KV_RANK
COL_GV = COL_GK + GQA_KV_HEADS * GQA_HEAD_DIM
COL_KR = COL_GV + GQA_KV_HEADS * GQA_HEAD_DIM
IN_COLS_PAD = COL_KR + LANES
MLA_QK_PAD = 2 * LANES
MOE_BLOCK = 256

NT_DIMS = (((1,), (1,)), ((), ()))


def _cparams(n_axes):
    return pltpu.CompilerParams(dimension_semantics=("arbitrary",) * n_axes,
                                vmem_limit_bytes=VMEM_LIMIT_BYTES)


def _resident(shape):
    nd = len(shape)
    return pl.BlockSpec(shape, lambda *_: (0,) * nd, pipeline_mode=pl.Buffered(1))


def _rms(x):
    return x * lax.rsqrt(jnp.mean(x * x, axis=-1, keepdims=True) + EPS)


def _silu(x):
    return x / (1.0 + jnp.exp(-x))


def _dot(a, b):
    return jnp.dot(a, b, preferred_element_type=F32)


def _dot_nt(a, b):
    return lax.dot_general(a, b, NT_DIMS, preferred_element_type=F32)


def _rope(x, c, s1, s2, shift):
    w = x.shape[-1]
    return x * c + pltpu.roll(x, w - shift, 1) * s1 + pltpu.roll(x, shift, 1) * s2


def _adaln_kernel(s_ref, w_ref, b_ref, o_ref):
    s = _silu(s_ref[...]).astype(BF16)
    o_ref[0] = _dot(s, w_ref[0].astype(BF16)) + b_ref[0]


def adaln_all(s_in, ada_w, ada_b):
    depth, d, n_out = ada_w.shape
    rows = s_in.shape[0]
    tn = 1024
    return pl.pallas_call(
        _adaln_kernel,
        grid=(depth, n_out // tn),
        in_specs=[pl.BlockSpec((rows, d), lambda l, j: (0, 0)),
                  pl.BlockSpec((1, d, tn), lambda l, j: (l, 0, j)),
                  pl.BlockSpec((1, 1, tn), lambda l, j: (l, 0, j))],
        out_specs=pl.BlockSpec((1, rows, tn), lambda l, j: (l, 0, j)),
        out_shape=jax.ShapeDtypeStruct((depth, rows, n_out), F32),
        compiler_params=_cparams(2),
        name="adaln",
    )(s_in, ada_w, ada_b.reshape(depth, 1, n_out))


def _in_proj_kernel(x_ref, g_ref, mod_ref, w_ref, o_ref):
    h = _rms(x_ref[...]) * g_ref[...]
    h = h * (1.0 + mod_ref[0, 1:2, :]) + mod_ref[0, 0:1, :]
    o_ref[...] = _dot(h.astype(BF16), w_ref[...]).astype(o_ref.dtype)


def in_proj(x_all, g, mod, w, tm, mod_idx):
    t, d = x_all.shape
    n_out = w.shape[1]
    return pl.pallas_call(
        _in_proj_kernel,
        grid=(t // tm,),
        in_specs=[pl.BlockSpec((tm, d), lambda i: (i, 0)),
                  pl.BlockSpec((1, d), lambda i: (0, 0)),
                  pl.BlockSpec((1, N_MOD, d), lambda i: (mod_idx(i), 0, 0)),
                  _resident((d, n_out))],
        out_specs=pl.BlockSpec((tm, n_out), lambda i: (i, 0)),
        out_shape=jax.ShapeDtypeStruct((t, n_out), BF16),
        compiler_params=_cparams(1),
        name="in_proj",
    )(x_all, g, mod, w)


def _mla_q_kernel(cq_ref, qn_ref, w_ref, c_ref, s1_ref, s2_ref, o_ref, *, scale):
    hn = (_rms(cq_ref[...].astype(F32)) * qn_ref[...]).astype(BF16)
    q = _dot(hn, w_ref[...]) * scale
    c, s1, s2 = c_ref[...], s1_ref[...], s2_ref[...]
    for h in range(MLA_HEADS):
        lo = h * MLA_QK_PAD
        o_ref[:, lo:lo + MLA_NOPE] = q[:, lo:lo + MLA_NOPE].astype(BF16)
        r = q[:, lo + MLA_NOPE:lo + MLA_QK_PAD]
        o_ref[:, lo + MLA_NOPE:lo + MLA_QK_PAD] = _rope(r, c, s1, s2, MLA_ROPE // 4).astype(BF16)


def mla_q(p, q_norm, w_uq, tabs, tm, rope_idx, scale):
    t = p.shape[0]
    n_out = MLA_HEADS * MLA_QK_PAD
    tab_spec = pl.BlockSpec((tm, LANES), lambda i: (rope_idx(i), 0))
    return pl.pallas_call(
        functools.partial(_mla_q_kernel, scale=scale),
        grid=(t // tm,),
        in_specs=[pl.BlockSpec((tm, MLA_Q_RANK), lambda i: (i, COL_MQ // MLA_Q_RANK)),
                  pl.BlockSpec((1, MLA_Q_RANK), lambda i: (0, 0)),
                  _resident((MLA_Q_RANK, n_out)),
                  tab_spec, tab_spec, tab_spec],
        out_specs=pl.BlockSpec((tm, n_out), lambda i: (i, 0)),
        out_shape=jax.ShapeDtypeStruct((t, n_out), BF16),
        compiler_params=_cparams(1),
        name="mla_q",
    )(p, q_norm, w_uq, *tabs)


def _mla_kv_kernel(ckv_ref, kr_ref, kvn_ref, w_ref, c_ref, s1_ref, s2_ref, k_ref, v_ref):
    hn = (_rms(ckv_ref[...].astype(F32)) * kvn_ref[...]).astype(BF16)
    kv = _dot(hn, w_ref[...])
    kr = _rope(kr_ref[...].astype(F32), c_ref[...], s1_ref[...], s2_ref[...],
               MLA_ROPE // 4).astype(BF16)
    ones = jnp.ones((kv.shape[0], MLA_V), BF16)
    v0 = MLA_HEADS * MLA_NOPE
    for h in range(MLA_HEADS):
        lo = h * MLA_QK_PAD
        k_ref[:, lo:lo + MLA_NOPE] = kv[:, h * MLA_NOPE:(h + 1) * MLA_NOPE].astype(BF16)
        k_ref[:, lo + MLA_NOPE:lo + MLA_QK_PAD] = kr
        v_ref[:, 2 * h * MLA_V:(2 * h + 1) * MLA_V] = kv[:, v0 + h * MLA_V:v0 + (h + 1) * MLA_V].astype(BF16)
        v_ref[:, (2 * h + 1) * MLA_V:(2 * h + 2) * MLA_V] = ones


def mla_kv(p, kv_norm, w_ukv, tabs, tm, rope_idx):
    t = p.shape[0]
    nk = MLA_HEADS * MLA_QK_PAD
    nv = MLA_HEADS * 2 * MLA_V
    tab_spec = pl.BlockSpec((tm, LANES), lambda i: (rope_idx(i), 0))
    return pl.pallas_call(
        _mla_kv_kernel,
        grid=(t // tm,),
        in_specs=[pl.BlockSpec((tm, MLA_KV_RANK), lambda i: (i, COL_CKV // MLA_KV_RANK)),
                  pl.BlockSpec((tm, LANES), lambda i: (i, COL_KR // LANES)),
                  pl.BlockSpec((1, MLA_KV_RANK), lambda i: (0, 0)),
                  _resident((MLA_KV_RANK, MLA_HEADS * (MLA_NOPE + MLA_V))),
                  tab_spec, tab_spec, tab_spec],
        out_specs=[pl.BlockSpec((tm, nk), lambda i: (i, 0)),
                   pl.BlockSpec((tm, nv), lambda i: (i, 0))],
        out_shape=[jax.ShapeDtypeStruct((t, nk), BF16),
                   jax.ShapeDtypeStruct((t, nv), BF16)],
        compiler_params=_cparams(1),
        name="mla_kv",
    )(p, p, kv_norm, w_ukv, *tabs)


def _gqa_qkv_kernel(q_ref, k_ref, v_ref, qn_ref, kn_ref, c_ref, s1_ref, s2_ref, qo_ref, ko_ref, vo_ref, *, scale):
    c, s1, s2 = c_ref[...], s1_ref[...], s2_ref[...]
    for h in range(GQA_HEADS):
        sl = slice(h * GQA_HEAD_DIM, (h + 1) * GQA_HEAD_DIM)
        x = _rms(q_ref[:, sl].astype(F32)) * qn_ref[...]
        qo_ref[:, sl] = (_rope(x, c, s1, s2, GQA_HEAD_DIM // 4) * scale).astype(BF16)
    ones = jnp.ones((v_ref.shape[0], GQA_HEAD_DIM), BF16)
    for g in range(GQA_KV_HEADS):
        sl = slice(g * GQA_HEAD_DIM, (g + 1) * GQA_HEAD_DIM)
        x = _rms(k_ref[:, sl].astype(F32)) * kn_ref[...]
        ko_ref[:, sl] = _rope(x, c, s1, s2, GQA_HEAD_DIM // 4).astype(BF16)
        vo_ref[:, 2 * g * GQA_HEAD_DIM:(2 * g + 1) * GQA_HEAD_DIM] = v_ref[:, sl]
        vo_ref[:, (2 * g + 1) * GQA_HEAD_DIM:(2 * g + 2) * GQA_HEAD_DIM] = ones


def gqa_qkv(p, q_norm, k_norm, tabs, tm, rope_idx, scale):
    t = p.shape[0]
    nq = GQA_HEADS * GQA_HEAD_DIM
    nk = GQA_KV_HEADS * GQA_HEAD_DIM
    tab_spec = pl.BlockSpec((tm, LANES), lambda i: (rope_idx(i), 0))
    return pl.pallas_call(
        functools.partial(_gqa_qkv_kernel, scale=scale),
        grid=(t // tm,),
        in_specs=[pl.BlockSpec((tm, nq), lambda i: (i, COL_GQ // nq)),
                  pl.BlockSpec((tm, nk), lambda i: (i, COL_GK // nk)),
                  pl.BlockSpec((tm, nk), lambda i: (i, COL_GV // nk)),
                  pl.BlockSpec((1, GQA_HEAD_DIM), lambda i: (0, 0)),
                  pl.BlockSpec((1, GQA_HEAD_DIM), lambda i: (0, 0)),
                  tab_spec, tab_spec, tab_spec],
        out_specs=[pl.BlockSpec((tm, nq), lambda i: (i, 0)),
                   pl.BlockSpec((tm, nk), lambda i: (i, 0)),
                   pl.BlockSpec((tm, 2 * nk), lambda i: (i, 0))],
        out_shape=[jax.ShapeDtypeStruct((t, nq), BF16),
                   jax.ShapeDtypeStruct((t, nk), BF16),
                   jax.ShapeDtypeStruct((t, 2 * nk), BF16)],
        compiler_params=_cparams(1),
        name="gqa_qkv",
    )(p, p, p, q_norm, k_norm, *tabs)


ATTN_SPLIT_ROWS = 256
ATTN_EXP_ROWS = 16


def _attn_kernel(*refs, n_parts):
    q_ref = refs[0]
    k_refs = refs[1:1 + n_parts]
    v_refs = refs[1 + n_parts:1 + 2 * n_parts]
    o_ref, s_scr, p_scr, m_scr = refs[1 + 2 * n_parts:]
    tq = q_ref.shape[0]
    dv = o_ref.shape[1]
    bounds = [0]
    for k in k_refs:
        bounds.append(bounds[-1] + k.shape[0])
    split = min(ATTN_SPLIT_ROWS, tq)
    groups = [slice(r, r + split) for r in range(0, tq, split)]

    for rs in groups:
        q = q_ref[rs, :]
        m = None
        for j, k in enumerate(k_refs):
            s = _dot_nt(q, k[...])
            s_scr[rs, bounds[j]:bounds[j + 1]] = s
            mj = jnp.max(s, axis=-1, keepdims=True)
            m = mj if m is None else jnp.maximum(m, mj)
        m_scr[rs, :] = m
    for r in range(0, tq, ATTN_EXP_ROWS):
        rows = slice(r, r + ATTN_EXP_ROWS)
        p_scr[rows, :] = jnp.exp2(s_scr[rows, :] - m_scr[rows, :]).astype(BF16)
    for rs in groups:
        o2 = functools.reduce(jnp.add, [_dot(p_scr[rs, bounds[j]:bounds[j + 1]], v[...])
                                        for j, v in enumerate(v_refs)])
        o_ref[rs, :] = (o2[:, :dv] / o2[:, dv:]).astype(o_ref.dtype)


def attention(q, k, v, *, bsz, n_q, tq, q_row0, kv_parts, heads, kv_group, dk, dv, name):
    assert tq % min(ATTN_SPLIT_ROWS, tq) == 0 and tq % ATTN_EXP_ROWS == 0
    nq_t = n_q // tq
    n_keys = sum(n_rows for _, n_rows in kv_parts)
    in_specs = [pl.BlockSpec((tq, dk), lambda b, h, i: (q_row0 // tq + b * nq_t + i, h))]
    for row0, n_rows in kv_parts:
        in_specs.append(pl.BlockSpec(
            (n_rows, dk), lambda b, h, i, row0=row0, n_rows=n_rows: (row0 // n_rows + b, h // kv_group)))
    for row0, n_rows in kv_parts:
        in_specs.append(pl.BlockSpec(
            (n_rows, 2 * dv), lambda b, h, i, row0=row0, n_rows=n_rows: (row0 // n_rows + b, h // kv_group)))
    n_parts = len(kv_parts)
    return pl.pallas_call(
        functools.partial(_attn_kernel, n_parts=n_parts),
        grid=(bsz, heads, nq_t),
        in_specs=in_specs,
        out_specs=pl.BlockSpec((tq, dv), lambda b, h, i: (b * nq_t + i, h)),
        out_shape=jax.ShapeDtypeStruct((bsz * n_q, heads * dv), BF16),
        scratch_shapes=[pltpu.VMEM((tq, n_keys), F32), pltpu.VMEM((tq, n_keys), BF16),
                        pltpu.VMEM((tq, 1), F32)],
        compiler_params=_cparams(3),
        name=name,
    )(q, *([k] * n_parts), *([v] * n_parts))


def dft_tables(n, half_shift):
    f = jnp.arange(n, dtype=jnp.int32)[:, None]
    s = jnp.arange(n, dtype=jnp.int32)[None, :]
    if half_shift:
        phase = ((2 * f + 1) * (2 * s + 1)) % (8 * n)
        ang = phase.astype(F32) * (2.0 * math.pi / (8 * n))
    else:
        phase = ((2 * f + 1) * s) % (4 * n)
        ang = phase.astype(F32) * (2.0 * math.pi / (4 * n))
    return jnp.cos(ang).astype(BF16), jnp.sin(ang).astype(BF16)


def hyena_time_filters(n, w1, b1, w2, b2, w3, freq):
    t = jnp.linspace(0.0, 1.0, n, dtype=F32)[:, None]
    ang = 2.0 * math.pi * jnp.arange(n, dtype=F32)[:, None] / n
    bands = jnp.linspace(1e-4, HY_BANDS - 1, HY_BANDS, dtype=F32)
    feats = jnp.concatenate([t, jnp.cos(ang * bands), jnp.sin(ang * bands)], axis=-1)
    fr = freq.astype(F32)
    hp = lax.Precision.HIGHEST
    hdn = jnp.sin(fr * (jnp.dot(feats, w1, precision=hp) + b1))
    hdn = jnp.sin(fr * (jnp.dot(hdn, w2, precision=hp) + b2))
    h = jnp.dot(hdn, w3, precision=hp).reshape(n, HY_ORDER, 2, HY_W)
    deltas = jnp.abs(jnp.linspace(math.log(HY_TARGET) / HY_SLOW, math.log(HY_TARGET) / HY_FAST,
                                  HY_W, dtype=F32))
    h = h * jnp.exp(-t * deltas)[:, None, None, :]
    h_fwd = h[:, :, 0]
    h_bwd = h[:, :, 1].at[0].set(0.0)
    r = lax.rsqrt(jnp.sum(h_fwd * h_fwd, axis=0) + jnp.sum(h_bwd * h_bwd, axis=0) + EPS)
    kf = (h_fwd * r).reshape(n, HY_ORDER * HY_W)
    kb = (h_bwd * r).reshape(n, HY_ORDER * HY_W)
    return kf + kb, kb - kf


HY_CHUNK = 256


def _spectra_kernel(ks_ref, kd_ref, c_ref, s_ref, kr_ref, ki_ref, *, inv_n):
    kr_ref[...] = _dot(c_ref[...], ks_ref[...].astype(BF16)) * inv_n
    ki_ref[...] = _dot(s_ref[...], kd_ref[...].astype(BF16)) * inv_n


def hyena_spectra(ksum, kdiff, cmat, smat):
    n, cols = ksum.shape
    spec = pl.BlockSpec((n, HY_CHUNK), lambda i: (0, i))
    return pl.pallas_call(
        functools.partial(_spectra_kernel, inv_n=1.0 / n),
        grid=(cols // HY_CHUNK,),
        in_specs=[spec, spec, _resident((n, n)), _resident((n, n))],
        out_specs=[spec, spec],
        out_shape=[jax.ShapeDtypeStruct((n, cols), F32)] * 2,
        compiler_params=_cparams(1),
        name="hyena_spectra",
    )(ksum, kdiff, cmat, smat)


def _hyena_kernel(v_ref, x1_ref, x2_ref, wv_ref, w1_ref, w2_ref, skip_ref,
                  k1r_ref, k1i_ref, k2r_ref, k2i_ref, c_ref, s_ref, o_ref, zb_scr, y_scr, *, n, fc):
    lane_groups = [slice(h * LANES, (h + 1) * LANES) for h in range(o_ref.shape[1] // LANES)]

    def short_conv(u_ref, w_ref, cs):
        u = u_ref[:, cs].astype(F32)
        pos = lax.broadcasted_iota(jnp.int32, u.shape, 0)
        prev = jnp.where(pos == 0, 0.0, pltpu.roll(u, 1, 0))
        nxt = jnp.where(pos == n - 1, 0.0, pltpu.roll(u, n - 1, 0))
        return prev * w_ref[0:1, cs] + u * w_ref[1:2, cs] + nxt * w_ref[2:3, cs]

    def long_conv(kr_ref, ki_ref):
        zb = zb_scr[...]
        for j in range(n // fc):
            sl = slice(j * fc, (j + 1) * fc)
            a = _dot(c_ref[sl, :], zb)
            b = _dot(s_ref[sl, :], zb)
            kr = kr_ref[sl, :]
            ki = ki_ref[sl, :]
            yr = (a * kr + b * ki).astype(BF16)
            yi = (a * ki - b * kr).astype(BF16)
            y_scr[...] += _dot(c_ref[:, sl], yr) - _dot(s_ref[:, sl], yi)

    for cs in lane_groups:
        v = short_conv(v_ref, wv_ref, cs)
        zb_scr[:, cs] = v.astype(BF16)
        y_scr[:, cs] = v * skip_ref[0:1, cs]
    long_conv(k1r_ref, k1i_ref)
    for cs in lane_groups:
        z = short_conv(x1_ref, w1_ref, cs) * y_scr[:, cs]
        zb_scr[:, cs] = z.astype(BF16)
        y_scr[:, cs] = z * skip_ref[1:2, cs]
    long_conv(k2r_ref, k2i_ref)
    for cs in lane_groups:
        o_ref[:, cs] = (short_conv(x2_ref, w2_ref, cs) * y_scr[:, cs]).astype(o_ref.dtype)


def hyena_mixer(p, conv_w, skip, kr, ki, cmat, smat, *, bsz, n, row0):
    ch = HY_CHUNK
    nc = HY_W // ch
    fc = min(512, n)
    blk0 = row0 // n

    def u_spec(k):
        return pl.BlockSpec((n, ch), lambda c, b: (blk0 + b, k * nc + c))

    def w_spec(k):
        return pl.BlockSpec((3, ch), lambda c, b: (0, k * nc + c))

    def k_spec(order):
        return pl.BlockSpec((n, ch), lambda c, b: (0, order * nc + c), pipeline_mode=pl.Buffered(1))

    return pl.pallas_call(
        functools.partial(_hyena_kernel, n=n, fc=fc),
        grid=(nc, bsz),
        in_specs=[u_spec(0), u_spec(1), u_spec(2), w_spec(0), w_spec(1), w_spec(2),
                  pl.BlockSpec((HY_ORDER, ch), lambda c, b: (0, c)),
                  k_spec(0), k_spec(0), k_spec(1), k_spec(1),
                  _resident((n, n)), _resident((n, n))],
        out_specs=pl.BlockSpec((n, ch), lambda c, b: (b, c)),
        out_shape=jax.ShapeDtypeStruct((bsz * n, HY_W), BF16),
        scratch_shapes=[pltpu.VMEM((n, ch), BF16), pltpu.VMEM((n, ch), F32)],
        compiler_params=_cparams(2),
        name="hyena_n%d" % n,
    )(p, p, p, conv_w, conv_w, conv_w, skip, kr, ki, kr, ki, cmat, smat)


def _out_proj_kernel(ho_ref, mo_ref, go_ref, wh_ref, wm_ref, wg_ref, x_ref, g1_ref, g2_ref,
                     mod_ref, rwh_ref, rwl_ref, xo_ref, f_ref, lg_ref):
    mix = _dot(ho_ref[...], wh_ref[...]) + _dot(mo_ref[...], wm_ref[...]) + _dot(go_ref[...], wg_ref[...])
    xn = x_ref[...] + mod_ref[0, 2:3, :] * (_rms(mix) * g1_ref[...])
    xo_ref[...] = xn
    f = (_rms(xn) * g2_ref[...]) * (1.0 + mod_ref[0, 4:5, :]) + mod_ref[0, 3:4, :]
    fh = f.astype(BF16)
    fl = (f - fh.astype(F32)).astype(BF16)
    f_ref[...] = fh
    lg_ref[...] = _dot_nt(rwh_ref[...], fh) + _dot_nt(rwh_ref[...], fl) + _dot_nt(rwl_ref[...], fh)


def out_proj(ho, mo, go, wh, wm, wg, x_all, g1, g2, mod, rwh, rwl, tm, mod_idx):
    t = ho.shape[0]
    d = D_MODEL
    row = lambda w: pl.BlockSpec((tm, w), lambda i: (i, 0))
    vec = pl.BlockSpec((1, d), lambda i: (0, 0))
    return pl.pallas_call(
        _out_proj_kernel,
        grid=(t // tm,),
        in_specs=[row(ho.shape[1]), row(mo.shape[1]), row(go.shape[1]),
                  _resident(wh.shape), _resident(wm.shape), _resident(wg.shape),
                  row(d), vec, vec,
                  pl.BlockSpec((1, N_MOD, d), lambda i: (mod_idx(i), 0, 0)),
                  _resident(rwh.shape), _resident(rwl.shape)],
        out_specs=[row(d), row(d), pl.BlockSpec((N_EXPERTS, tm), lambda i: (0, i))],
        out_shape=[jax.ShapeDtypeStruct((t, d), F32),
                   jax.ShapeDtypeStruct((t, d), BF16),
                   jax.ShapeDtypeStruct((N_EXPERTS, t), F32)],
        compiler_params=_cparams(1),
        name="out_proj",
    )(ho, mo, go, wh, wm, wg, x_all, g1, g2, mod, rwh, rwl)


def _moe_kernel(be_ref, bsrc_ref, bflag_ref, x_ref, wg_ref, wu_ref, wd_ref, o_ref, wg_s, wu_s, wd_s):
    i = pl.program_id(0)

    @pl.when(bflag_ref[i] == 2)
    def _():
        wg_s[...] = wg_ref[0, 0].astype(BF16)
        wu_s[...] = wu_ref[0, 0].astype(BF16)
        wd_s[...] = wd_ref[0, 0].astype(BF16)

    @pl.when(bflag_ref[i] != 0)
    def _():
        x = x_ref[...]
        a = _silu(_dot(x, wg_s[...])) * _dot(x, wu_s[...])
        o_ref[...] = _dot(a.astype(BF16), wd_s[...]).astype(o_ref.dtype)

    @pl.when(bflag_ref[i] == 0)
    def _():
        o_ref[...] = jnp.zeros(o_ref.shape, o_ref.dtype)


def moe_experts(blk_expert, blk_src, blk_flag, xs, wg, wu, wd, layer):
    n_rows, d = xs.shape
    hid = wg.shape[3]
    grid_spec = pltpu.PrefetchScalarGridSpec(
        num_scalar_prefetch=3,
        grid=(n_rows // MOE_BLOCK,),
        in_specs=[pl.BlockSpec((MOE_BLOCK, d), lambda i, be, bs, bf: (bs[i], 0)),
                  pl.BlockSpec((1, 1, d, hid), lambda i, be, bs, bf: (layer, be[i], 0, 0)),
                  pl.BlockSpec((1, 1, d, hid), lambda i, be, bs, bf: (layer, be[i], 0, 0)),
                  pl.BlockSpec((1, 1, hid, d), lambda i, be, bs, bf: (layer, be[i], 0, 0))],
        out_specs=pl.BlockSpec((MOE_BLOCK, d), lambda i, be, bs, bf: (i, 0)),
        scratch_shapes=[pltpu.VMEM((d, hid), BF16), pltpu.VMEM((d, hid), BF16),
                        pltpu.VMEM((hid, d), BF16)],
    )
    return pl.pallas_call(
        _moe_kernel,
        grid_spec=grid_spec,
        out_shape=jax.ShapeDtypeStruct((n_rows, d), BF16),
        compiler_params=_cparams(1),
        name="moe_experts",
    )(blk_expert, blk_src, blk_flag, xs, wg, wu, wd)


def _shared_kernel(f_ref, yg_ref, gate_ref, x_ref, sg_ref, su_ref, sd_ref, g3_ref, mod_ref, o_ref):
    f = f_ref[...]
    a = _silu(_dot(f, sg_ref[...])) * _dot(f, su_ref[...])
    y = _dot(a.astype(BF16), sd_ref[...])
    gates = gate_ref[...]
    for k in range(TOP_K):
        y = y + gates[:, k:k + 1] * yg_ref[k].astype(F32)
    o_ref[...] = x_ref[...] + mod_ref[0, 5:6, :] * (_rms(y) * g3_ref[...])


def shared_ffn(f, y_gathered, gates, x_all, sg, su, sd, g3, mod, tm, mod_idx):
    t, d = f.shape
    row = pl.BlockSpec((tm, d), lambda i: (i, 0))
    return pl.pallas_call(
        _shared_kernel,
        grid=(t // tm,),
        in_specs=[row, pl.BlockSpec((TOP_K, tm, d), lambda i: (0, i, 0)),
                  pl.BlockSpec((tm, gates.shape[1]), lambda i: (i, 0)), row,
                  _resident(sg.shape), _resident(su.shape), _resident(sd.shape),
                  pl.BlockSpec((1, d), lambda i: (0, 0)),
                  pl.BlockSpec((1, N_MOD, d), lambda i: (mod_idx(i), 0, 0))],
        out_specs=row,
        out_shape=jax.ShapeDtypeStruct((t, d), F32),
        compiler_params=_cparams(1),
        name="shared_ffn",
    )(f, y_gathered, gates, x_all, sg, su, sd, g3, mod)


def _first_max(vals, idx, n):
    m = jnp.max(vals, axis=0, keepdims=True)
    return m, jnp.min(jnp.where(vals == m, idx, float(n)), axis=0, keepdims=True)


def _router_kernel(lg_ref, bias_ref, e_ref, g_ref, r_ref, cnt_ref, carry_ref):
    @pl.when(pl.program_id(0) == 0)
    def _():
        carry_ref[...] = jnp.zeros(carry_ref.shape, F32)

    lg = lg_ref[...]
    tt = lg.shape[1]
    gsz = N_EXPERTS // N_GROUPS
    neg = -jnp.inf
    scores = 1.0 / (1.0 + jnp.exp(-lg))
    biased = scores + bias_ref[...]
    sub = lax.broadcasted_iota(jnp.int32, (gsz, tt), 0).astype(F32)

    grp_rows = []
    for g in range(N_GROUPS):
        blk = biased[g * gsz:(g + 1) * gsz, :]
        m1, i1 = _first_max(blk, sub, gsz)
        m2 = jnp.max(jnp.where(sub == i1, neg, blk), axis=0, keepdims=True)
        grp_rows.append(m1 + m2)
    cur = jnp.concatenate(grp_rows, axis=0)
    gidx = lax.broadcasted_iota(jnp.int32, (N_GROUPS, tt), 0).astype(F32)
    gsel = jnp.zeros((N_GROUPS, tt), F32)
    for _ in range(TOPK_GROUPS):
        _, gi = _first_max(cur, gidx, N_GROUPS)
        hit = gidx == gi
        gsel = jnp.where(hit, 1.0, gsel)
        cur = jnp.where(hit, neg, cur)
    emask = jnp.concatenate([jnp.broadcast_to(gsel[g:g + 1, :], (gsz, tt)) for g in range(N_GROUPS)], axis=0)

    cand = jnp.where(emask > 0.0, biased, neg)
    eidx = lax.broadcasted_iota(jnp.int32, (N_EXPERTS, tt), 0).astype(F32)
    chosen = jnp.zeros((N_EXPERTS, tt), F32)
    e_rows, g_rows = [], []
    for _ in range(TOP_K):
        _, ei = _first_max(cand, eidx, N_EXPERTS)
        hit = eidx == ei
        e_rows.append(ei)
        g_rows.append(jnp.sum(jnp.where(hit, scores, 0.0), axis=0, keepdims=True))
        chosen = jnp.where(hit, 1.0, chosen)
        cand = jnp.where(hit, neg, cand)
    gsum = functools.reduce(jnp.add, g_rows)
    g_rows = [g / gsum * ROUTED_SCALE for g in g_rows]

    before = (lax.broadcasted_iota(jnp.int32, (tt, tt), 0) < lax.broadcasted_iota(jnp.int32, (tt, tt), 1))
    prefix = _dot(chosen.astype(BF16), jnp.where(before, 1.0, 0.0).astype(BF16))
    rank_all = prefix + carry_ref[...]
    r_rows = [jnp.sum(jnp.where(eidx == ei, rank_all, 0.0), axis=0, keepdims=True) for ei in e_rows]
    carry_ref[...] = carry_ref[...] + jnp.sum(chosen, axis=1, keepdims=True)
    cnt_ref[...] = carry_ref[...]

    pad = [jnp.zeros((8 - TOP_K, tt), F32)]
    e_ref[...] = jnp.concatenate(e_rows + pad, axis=0).astype(jnp.int32)
    g_ref[...] = jnp.concatenate(g_rows + pad, axis=0)
    r_ref[...] = jnp.concatenate(r_rows + pad, axis=0).astype(jnp.int32)


def router(logits_t, bias, tt):
    n_exp, t = logits_t.shape
    col = pl.BlockSpec((8, tt), lambda i: (0, i))
    return pl.pallas_call(
        _router_kernel,
        grid=(t // tt,),
        in_specs=[pl.BlockSpec((n_exp, tt), lambda i: (0, i)),
                  pl.BlockSpec((n_exp, 1), lambda i: (0, 0))],
        out_specs=[col, col, col, pl.BlockSpec((n_exp, 1), lambda i: (0, 0))],
        out_shape=[jax.ShapeDtypeStruct((8, t), jnp.int32), jax.ShapeDtypeStruct((8, t), F32),
                   jax.ShapeDtypeStruct((8, t), jnp.int32), jax.ShapeDtypeStruct((n_exp, 1), F32)],
        scratch_shapes=[pltpu.VMEM((n_exp, 1), F32)],
        compiler_params=_cparams(1),
        name="router",
    )(logits_t, bias.reshape(n_exp, 1).astype(F32))


def rope_tables(n, rot_dim, tm):
    rows = n // GRID_W
    row = jnp.repeat(jnp.arange(rows, dtype=F32), GRID_W)
    col = jnp.tile(jnp.arange(GRID_W, dtype=F32), rows)
    axis_dim = rot_dim // 2
    inv = ROPE_THETA ** (-jnp.arange(0, axis_dim, 2, dtype=F32) / axis_dim)
    ar, ac = row[:, None] * inv, col[:, None] * inv
    zero = jnp.zeros_like(ar)
    c = jnp.concatenate([jnp.cos(ar), jnp.cos(ar), jnp.cos(ac), jnp.cos(ac)], axis=-1)
    s1 = jnp.concatenate([-jnp.sin(ar), zero, -jnp.sin(ac), zero], axis=-1)
    s2 = jnp.concatenate([zero, jnp.sin(ar), zero, jnp.sin(ac)], axis=-1)

    def finish(tab, fill):
        tab = jnp.pad(tab, ((0, 0), (0, LANES - rot_dim)), constant_values=fill)
        return jnp.concatenate([tab, jnp.full((tm, LANES), fill, F32)], axis=0)

    return finish(c, 1.0), finish(s1, 0.0), finish(s2, 0.0)


def pack_w_in(w):
    hy = 3 * HY_W
    kv0 = hy + MLA_Q_RANK + GQA_HEADS * GQA_HEAD_DIM
    ckv = w[:, kv0:kv0 + MLA_KV_RANK]
    kr = w[:, kv0 + MLA_KV_RANK:kv0 + MLA_KV_RANK + MLA_ROPE]
    gk0 = kv0 + MLA_KV_RANK + MLA_ROPE
    gkv = w[:, gk0:]
    kr = jnp.pad(kr, ((0, 0), (0, LANES - MLA_ROPE)))
    return jnp.concatenate([w[:, :kv0], ckv, gkv, kr], axis=1).astype(BF16)


def pack_w_uq(w):
    w = w.reshape(MLA_Q_RANK, MLA_HEADS, MLA_NOPE + MLA_ROPE)
    w = jnp.pad(w, ((0, 0), (0, 0), (0, MLA_QK_PAD - MLA_NOPE - MLA_ROPE)))
    return w.reshape(MLA_Q_RANK, MLA_HEADS * MLA_QK_PAD).astype(BF16)


def pack_w_ukv(w):
    w = w.reshape(MLA_KV_RANK, MLA_HEADS, MLA_NOPE + MLA_V)
    k = w[:, :, :MLA_NOPE].reshape(MLA_KV_RANK, MLA_HEADS * MLA_NOPE)
    v = w[:, :, MLA_NOPE:].reshape(MLA_KV_RANK, MLA_HEADS * MLA_V)
    return jnp.concatenate([k, v], axis=1).astype(BF16)


def dispatch_plan(e_t, rank_t, counts):
    n_tok = e_t.shape[1]
    n_pairs = n_tok * TOP_K
    counts = counts.astype(jnp.int32)
    padded = (counts + MOE_BLOCK - 1) // MOE_BLOCK * MOE_BLOCK
    p_ends = jnp.cumsum(padded)
    p_starts = p_ends - padded
    n_blocks = -(-n_pairs // MOE_BLOCK) + N_EXPERTS
    n_rows = n_blocks * MOE_BLOCK
    blk = jnp.arange(n_blocks, dtype=jnp.int32)
    n_used = p_ends[-1] // MOE_BLOCK
    blk_src = jnp.minimum(blk, n_used - 1)
    blk_expert = jnp.sum((blk_src * MOE_BLOCK)[:, None] >= p_ends[None, :], axis=1).astype(jnp.int32)
    first = jnp.concatenate([jnp.ones((1,), bool), blk_expert[1:] != blk_expert[:-1]])
    blk_flag = jnp.where(blk < n_used, jnp.where(first, 2, 1), 0).astype(jnp.int32)

    experts = jnp.arange(N_EXPERTS, dtype=jnp.int32)
    onehot = e_t[:, :, None] == experts
    pos = jnp.sum(jnp.where(onehot, p_starts, 0), axis=-1) + rank_t

    stride = n_tok + 1
    tok = jnp.arange(n_tok, dtype=jnp.int32)
    real_keys = (e_t * stride + tok[None, :]).reshape(-1)
    cum_fill = jnp.cumsum(padded - counts)
    filler = jnp.arange(n_rows - n_pairs, dtype=jnp.int32)
    filler_exp = jnp.sum(filler[:, None] >= cum_fill[None, :], axis=1).astype(jnp.int32)
    filler_keys = filler_exp * stride + n_tok
    slot_tok = jnp.sort(jnp.concatenate([real_keys, filler_keys])) % stride
    spread = jnp.arange(n_rows, dtype=jnp.int32) % n_tok
    slot_tok = jnp.where(slot_tok == n_tok, spread, slot_tok)
    return blk_expert, blk_src, blk_flag, slot_tok, pos


def _pick_tile(*sizes):
    for tile in (512, 256, 128):
        if all(s % tile == 0 for s in sizes):
            return tile
    raise ValueError("row counts must be multiples of 128: %r" % (sizes,))


def kernel(x, c, ctx, c_ctx, ada_w, ada_b, norm_g, w_in, w_out, hy_conv, hy_w1, hy_b1, hy_w2, hy_b2,
           hy_w3, hy_freq, hy_skip, mla_q_norm, mla_kv_norm, mla_w_uq, mla_w_ukv, gqa_q_norm, gqa_k_norm,
           router_w, router_bias, exp_w_gate, exp_w_up, exp_w_down, sh_w_gate, sh_w_up, sh_w_down):
    bsz, n, d = x.shape
    n_ctx = ctx.shape[1]
    depth = ada_w.shape[0]
    assert d == D_MODEL and n % GRID_W == 0
    t_lat, t_ctx = bsz * n, bsz * n_ctx
    assert t_lat % n_ctx == 0
    tm = _pick_tile(n, t_ctx)
    tq = max(t for t in (1024, 512, 256) if n % t == 0)
    lat_tiles = t_lat // tm
    tiles_per_seq = n // tm
    tm_ffn = min(tm, 256)

    def mod_idx(tile):
        return lambda i: jnp.minimum(i // (n // tile), bsz)

    def rope_idx(i):
        return jnp.where(i < lat_tiles, i % tiles_per_seq, tiles_per_seq)

    x_all = jnp.concatenate([x.reshape(t_lat, d), ctx.reshape(t_ctx, d)], axis=0)
    mod_rows = -(-(bsz + 1) // 16) * 16
    s_in = jnp.concatenate([c, c_ctx[None], jnp.zeros((mod_rows - bsz - 1, d), F32)], axis=0)
    mod_all = adaln_all(s_in, ada_w, ada_b).reshape(depth, mod_rows, N_MOD, d)

    tabs_mla = rope_tables(n, MLA_ROPE, tm)
    tabs_gqa = rope_tables(n, GQA_HEAD_DIM, tm)
    mla_scale = (MLA_NOPE + MLA_ROPE) ** -0.5 * math.log2(math.e)
    gqa_scale = GQA_HEAD_DIM ** -0.5 * math.log2(math.e)
    dft ={m: (dft_tables(m, False), dft_tables(m, True)) for m in (n, n_ctx)}

    for l in range(depth):
        last = l == depth - 1
        mod = mod_all[l]
        vec = lambda a: a.reshape(1, -1)

        p = in_proj(x_all, vec(norm_g[l, 0]), mod, pack_w_in(w_in[l]), tm, mod_idx(tm))
        q_m = mla_q(p, vec(mla_q_norm[l]), pack_w_uq(mla_w_uq[l]), tabs_mla, tm, rope_idx, mla_scale)
        k_m, v_m = mla_kv(p, vec(mla_kv_norm[l]), pack_w_ukv(mla_w_ukv[l]), tabs_mla, tm, rope_idx)
        q_g, k_g, v_g = gqa_qkv(p, vec(gqa_q_norm[l]), vec(gqa_k_norm[l]), tabs_gqa, tm, rope_idx, gqa_scale)

        lat_parts = [(t_lat, n_ctx), (0, n)]
        mo = attention(q_m, k_m, v_m, bsz=bsz, n_q=n, tq=tq, q_row0=0, kv_parts=lat_parts,
                       heads=MLA_HEADS, kv_group=1, dk=MLA_QK_PAD, dv=MLA_V, name="mla_attn")
        go = attention(q_g, k_g, v_g, bsz=bsz, n_q=n, tq=tq, q_row0=0, kv_parts=lat_parts,
                       heads=GQA_HEADS, kv_group=GQA_HEADS // GQA_KV_HEADS, dk=GQA_HEAD_DIM,
                       dv=GQA_HEAD_DIM, name="gqa_attn")

        filt = (hy_w1[l], hy_b1[l], hy_w2[l], hy_b2[l], hy_w3[l], hy_freq[l])

        def hyena(m, row0):
            lag_tabs, sym_tabs = dft[m]
            kr, ki = hyena_spectra(*hyena_time_filters(m, *filt), *lag_tabs)
            return hyena_mixer(p, hy_conv[l], hy_skip[l], kr, ki, *sym_tabs, bsz=bsz, n=m, row0=row0)

        ho = hyena(n, 0)

        if not last:
            ctx_parts = [(t_lat, n_ctx)]
            mo_c = attention(q_m, k_m, v_m, bsz=bsz, n_q=n_ctx, tq=n_ctx, q_row0=t_lat, kv_parts=ctx_parts,
                             heads=MLA_HEADS, kv_group=1, dk=MLA_QK_PAD, dv=MLA_V, name="mla_attn_ctx")
            go_c = attention(q_g, k_g, v_g, bsz=bsz, n_q=n_ctx, tq=n_ctx, q_row0=t_lat, kv_parts=ctx_parts,
                             heads=GQA_HEADS, kv_group=GQA_HEADS // GQA_KV_HEADS, dk=GQA_HEAD_DIM,
                             dv=GQA_HEAD_DIM, name="gqa_attn_ctx")
            ho_c = hyena(n_ctx, t_lat)
            ho = jnp.concatenate([ho, ho_c], axis=0)
            mo = jnp.concatenate([mo, mo_c], axis=0)
            go = jnp.concatenate([go, go_c], axis=0)
            x_cur = x_all
        else:
            x_cur = x_all[:t_lat]

        wo = w_out[l].astype(BF16)
        rw_t = router_w[l].T
        rw_hi = rw_t.astype(BF16)
        rw_lo = (rw_t - rw_hi.astype(F32)).astype(BF16)
        x_mid, f, logits_t = out_proj(ho, mo, go, wo[:HY_W], wo[HY_W:HY_W + MLA_HEADS * MLA_V],
                                      wo[HY_W + MLA_HEADS * MLA_V:], x_cur, vec(norm_g[l, 1]),
                                      vec(norm_g[l, 2]), mod, rw_hi, rw_lo, tm, mod_idx(tm))

        e_t, gate_t, rank_t, counts = router(logits_t, router_bias[l], tm)
        blk_expert, blk_src, blk_flag, slot_tok, pos = dispatch_plan(e_t[:TOP_K], rank_t[:TOP_K], counts[:, 0])
        ys = moe_experts(blk_expert, blk_src, blk_flag, f[slot_tok], exp_w_gate, exp_w_up, exp_w_down, l)
        x_all = shared_ffn(f, ys[pos], gate_t.T, x_mid, sh_w_gate[l].astype(BF16), sh_w_up[l].astype(BF16),
                           sh_w_down[l].astype(BF16), vec(norm_g[l, 3]), mod, tm_ffn, mod_idx(tm_ffn))

    return x_all[:t_lat].reshape(bsz, n, d)
```

```python
import functools
import math

import jax
import jax.numpy as jnp
from jax import lax
from jax.experimental import pallas as pl
from jax.experimental.pallas import tpu as pltpu

F32 = jnp.float32
BF16 = jnp.bfloat16

D_MODEL = 2048
GRID_W = 64
EPS = 1e-6
N_MOD = 6
HY_W = D_MODEL // 4
HY_ORDER = 2
HY_BANDS = 16
HY_TARGET = 1e-2
HY_FAST = 0.3
HY_SLOW = 1.5
MLA_NOPE = 128
MLA_ROPE = 64
MLA_V = 128
MLA_HEADS = 6
MLA_Q_RANK = 768
MLA_KV_RANK = 256
GQA_HEAD_DIM = 128
GQA_HEADS = 6
GQA_KV_HEADS = 2
ROPE_THETA = 10000.0
N_EXPERTS = 64
TOP_K = 6
N_GROUPS = 8
TOPK_GROUPS = 4
EXPERT_HIDDEN = D_MODEL // 4
ROUTED_SCALE = 2.5

LANES = 128
VMEM_LIMIT_BYTES = 56 * 1024 * 1024

COL_HY = 0
COL_MQ = 3 * HY_W
COL_GQ = COL_MQ + MLA_Q_RANK
COL_CKV = COL_GQ + GQA_HEADS * GQA_HEAD_DIM
COL_GK = COL_CKV + MLA_KV_RANK
COL_GV = COL_GK + GQA_KV_HEADS * GQA_HEAD_DIM
COL_KR = COL_GV + GQA_KV_HEADS * GQA_HEAD_DIM
IN_COLS_PAD = COL_KR + LANES
MLA_QK_PAD = 2 * LANES
MOE_BLOCK = 256

NT_DIMS = (((1,), (1,)), ((), ()))


def _cparams(n_axes):
    return pltpu.CompilerParams(dimension_semantics=("arbitrary",) * n_axes,
                                vmem_limit_bytes=VMEM_LIMIT_BYTES)


def _resident(shape):
    nd = len(shape)
    return pl.BlockSpec(shape, lambda *_: (0,) * nd, pipeline_mode=pl.Buffered(1))


def _rms(x):
    return x * lax.rsqrt(jnp.mean(x * x, axis=-1, keepdims=True) + EPS)


def _silu(x):
    return x / (1.0 + jnp.exp(-x))


def _dot(a, b):
    return jnp.dot(a, b, preferred_element_type=F32)


def _dot_nt(a, b):
    return lax.dot_general(a, b, NT_DIMS, preferred_element_type=F32)


def _rope(x, c, s1, s2, shift):
    w = x.shape[-1]
    return x * c + pltpu.roll(x, w - shift, 1) * s1 + pltpu.roll(x, shift, 1) * s2


def _adaln_kernel(s_ref, w_ref, b_ref, o_ref):
    s = _silu(s_ref[...]).astype(BF16)
    o_ref[0] = _dot(s, w_ref[0].astype(BF16)) + b_ref[0]


def adaln_all(s_in, ada_w, ada_b):
    depth, d, n_out = ada_w.shape
    rows = s_in.shape[0]
    tn = 1024
    return pl.pallas_call(
        _adaln_kernel,
        grid=(depth, n_out // tn),
        in_specs=[pl.BlockSpec((rows, d), lambda l, j: (0, 0)),
                  pl.BlockSpec((1, d, tn), lambda l, j: (l, 0, j)),
                  pl.BlockSpec((1, 1, tn), lambda l, j: (l, 0, j))],
        out_specs=pl.BlockSpec((1, rows, tn), lambda l, j: (l, 0, j)),
        out_shape=jax.ShapeDtypeStruct((depth, rows, n_out), F32),
        compiler_params=_cparams(2),
        name="adaln",
    )(s_in, ada_w, ada_b.reshape(depth, 1, n_out))


def _in_proj_kernel(x_ref, g_ref, mod_ref, w_ref, o_ref):
    h = _rms(x_ref[...]) * g_ref[...]
    h = h * (1.0 + mod_ref[0, 1:2, :]) + mod_ref[0, 0:1, :]
    o_ref[...] = _dot(h.astype(BF16), w_ref[...]).astype(o_ref.dtype)


def in_proj(x_all, g, mod, w, tm, mod_idx):
    t, d = x_all.shape
    n_out = w.shape[1]
    return pl.pallas_call(
        _in_proj_kernel,
        grid=(t // tm,),
        in_specs=[pl.BlockSpec((tm, d), lambda i: (i, 0)),
                  pl.BlockSpec((1, d), lambda i: (0, 0)),
                  pl.BlockSpec((1, N_MOD, d), lambda i: (mod_idx(i), 0, 0)),
                  _resident((d, n_out))],
        out_specs=pl.BlockSpec((tm, n_out), lambda i: (i, 0)),
        out_shape=jax.ShapeDtypeStruct((t, n_out), BF16),
        compiler_params=_cparams(1),
        name="in_proj",
    )(x_all, g, mod, w)


def _mla_q_kernel(cq_ref, qn_ref, w_ref, c_ref, s1_ref, s2_ref, o_ref, *, scale):
    hn = (_rms(cq_ref[...].astype(F32)) * qn_ref[...]).astype(BF16)
    q = _dot(hn, w_ref[...]) * scale
    c, s1, s2 = c_ref[...], s1_ref[...], s2_ref[...]
    for h in range(MLA_HEADS):
        lo = h * MLA_QK_PAD
        o_ref[:, lo:lo + MLA_NOPE] = q[:, lo:lo + MLA_NOPE].astype(BF16)
        r = q[:, lo + MLA_NOPE:lo + MLA_QK_PAD]
        o_ref[:, lo + MLA_NOPE:lo + MLA_QK_PAD] = _rope(r, c, s1, s2, MLA_ROPE // 4).astype(BF16)


def mla_q(p, q_norm, w_uq, tabs, tm, rope_idx, scale):
    t = p.shape[0]
    n_out = MLA_HEADS * MLA_QK_PAD
    tab_spec = pl.BlockSpec((tm, LANES), lambda i: (rope_idx(i), 0))
    return pl.pallas_call(
        functools.partial(_mla_q_kernel, scale=scale),
        grid=(t // tm,),
        in_specs=[pl.BlockSpec((tm, MLA_Q_RANK), lambda i: (i, COL_MQ // MLA_Q_RANK)),
                  pl.BlockSpec((1, MLA_Q_RANK), lambda i: (0, 0)),
                  _resident((MLA_Q_RANK, n_out)),
                  tab_spec, tab_spec, tab_spec],
        out_specs=pl.BlockSpec((tm, n_out), lambda i: (i, 0)),
        out_shape=jax.ShapeDtypeStruct((t, n_out), BF16),
        compiler_params=_cparams(1),
        name="mla_q",
    )(p, q_norm, w_uq, *tabs)


def _mla_kv_kernel(ckv_ref, kr_ref, kvn_ref, w_ref, c_ref, s1_ref, s2_ref, k_ref, v_ref):
    hn = (_rms(ckv_ref[...].astype(F32)) * kvn_ref[...]).astype(BF16)
    kv = _dot(hn, w_ref[...])
    kr = _rope(kr_ref[...].astype(F32), c_ref[...], s1_ref[...], s2_ref[...],
               MLA_ROPE // 4).astype(BF16)
    ones = jnp.ones((kv.shape[0], MLA_V), BF16)
    v0 = MLA_HEADS * MLA_NOPE
    for h in range(MLA_HEADS):
        lo = h * MLA_QK_PAD
        k_ref[:, lo:lo + MLA_NOPE] = kv[:, h * MLA_NOPE:(h + 1) * MLA_NOPE].astype(BF16)
        k_ref[:, lo + MLA_NOPE:lo + MLA_QK_PAD] = kr
        v_ref[:, 2 * h * MLA_V:(2 * h + 1) * MLA_V] = kv[:, v0 + h * MLA_V:v0 + (h + 1) * MLA_V].astype(BF16)
        v_ref[:, (2 * h + 1) * MLA_V:(2 * h + 2) * MLA_V] = ones


def mla_kv(p, kv_norm, w_ukv, tabs, tm, rope_idx):
    t = p.shape[0]
    nk = MLA_HEADS * MLA_QK_PAD
    nv = MLA_HEADS * 2 * MLA_V
    tab_spec = pl.BlockSpec((tm, LANES), lambda i: (rope_idx(i), 0))
    return pl.pallas_call(
        _mla_kv_kernel,
        grid=(t // tm,),
        in_specs=[pl.BlockSpec((tm, MLA_KV_RANK), lambda i: (i, COL_CKV // MLA_KV_RANK)),
                  pl.BlockSpec((tm, LANES), lambda i: (i, COL_KR // LANES)),
                  pl.BlockSpec((1, MLA_KV_RANK), lambda i: (0, 0)),
                  _resident((MLA_KV_RANK, MLA_HEADS * (MLA_NOPE + MLA_V))),
                  tab_spec, tab_spec, tab_spec],
        out_specs=[pl.BlockSpec((tm, nk), lambda i: (i, 0)),
                   pl.BlockSpec((tm, nv), lambda i: (i, 0))],
        out_shape=[jax.ShapeDtypeStruct((t, nk), BF16),
                   jax.ShapeDtypeStruct((t, nv), BF16)],
        compiler_params=_cparams(1),
        name="mla_kv",
    )(p, p, kv_norm, w_ukv, *tabs)


def _gqa_qkv_kernel(q_ref, k_ref, v_ref, qn_ref, kn_ref, c_ref, s1_ref, s2_ref, qo_ref, ko_ref, vo_ref, *, scale):
    c, s1, s2 = c_ref[...], s1_ref[...], s2_ref[...]
    for h in range(GQA_HEADS):
        sl = slice(h * GQA_HEAD_DIM, (h + 1) * GQA_HEAD_DIM)
        x = _rms(q_ref[:, sl].astype(F32)) * qn_ref[...]
        qo_ref[:, sl] = (_rope(x, c, s1, s2, GQA_HEAD_DIM // 4) * scale).astype(BF16)
    ones = jnp.ones((v_ref.shape[0], GQA_HEAD_DIM), BF16)
    for g in range(GQA_KV_HEADS):
        sl = slice(g * GQA_HEAD_DIM, (g + 1) * GQA_HEAD_DIM)
        x = _rms(k_ref[:, sl].astype(F32)) * kn_ref[...]
        ko_ref[:, sl] = _rope(x, c, s1, s2, GQA_HEAD_DIM // 4).astype(BF16)
        vo_ref[:, 2 * g * GQA_HEAD_DIM:(2 * g + 1) * GQA_HEAD_DIM] = v_ref[:, sl]
        vo_ref[:, (2 * g + 1) * GQA_HEAD_DIM:(2 * g + 2) * GQA_HEAD_DIM] = ones


def gqa_qkv(p, q_norm, k_norm, tabs, tm, rope_idx, scale):
    t = p.shape[0]
    nq = GQA_HEADS * GQA_HEAD_DIM
    nk = GQA_KV_HEADS * GQA_HEAD_DIM
    tab_spec = pl.BlockSpec((tm, LANES), lambda i: (rope_idx(i), 0))
    return pl.pallas_call(
        functools.partial(_gqa_qkv_kernel, scale=scale),
        grid=(t // tm,),
        in_specs=[pl.BlockSpec((tm, nq), lambda i: (i, COL_GQ // nq)),
                  pl.BlockSpec((tm, nk), lambda i: (i, COL_GK // nk)),
                  pl.BlockSpec((tm, nk), lambda i: (i, COL_GV // nk)),
                  pl.BlockSpec((1, GQA_HEAD_DIM), lambda i: (0, 0)),
                  pl.BlockSpec((1, GQA_HEAD_DIM), lambda i: (0, 0)),
                  tab_spec, tab_spec, tab_spec],
        out_specs=[pl.BlockSpec((tm, nq), lambda i: (i, 0)),
                   pl.BlockSpec((tm, nk), lambda i: (i, 0)),
                   pl.BlockSpec((tm, 2 * nk), lambda i: (i, 0))],
        out_shape=[jax.ShapeDtypeStruct((t, nq), BF16),
                   jax.ShapeDtypeStruct((t, nk), BF16),
                   jax.ShapeDtypeStruct((t, 2 * nk), BF16)],
        compiler_params=_cparams(1),
        name="gqa_qkv",
    )(p, p, p, q_norm, k_norm, *tabs)


ATTN_SPLIT_ROWS = 256
ATTN_EXP_ROWS = 16


def _attn_kernel(*refs, n_parts):
    q_ref = refs[0]
    k_refs = refs[1:1 + n_parts]
    v_refs = refs[1 + n_parts:1 + 2 * n_parts]
    o_ref, s_scr, p_scr, m_scr = refs[1 + 2 * n_parts:]
    tq = q_ref.shape[0]
    dv = o_ref.shape[1]
    bounds = [0]
    for k in k_refs:
        bounds.append(bounds[-1] + k.shape[0])
    split = min(ATTN_SPLIT_ROWS, tq)
    groups = [slice(r, r + split) for r in range(0, tq, split)]

    for rs in groups:
        q = q_ref[rs, :]
        m = None
        for j, k in enumerate(k_refs):
            s = _dot_nt(q, k[...])
            s_scr[rs, bounds[j]:bounds[j + 1]] = s
            mj = jnp.max(s, axis=-1, keepdims=True)
            m = mj if m is None else jnp.maximum(m, mj)
        m_scr[rs, :] = m
    for r in range(0, tq, ATTN_EXP_ROWS):
        rows = slice(r, r + ATTN_EXP_ROWS)
        p_scr[rows, :] = jnp.exp2(s_scr[rows, :] - m_scr[rows, :]).astype(BF16)
    for rs in groups:
        o2 = functools.reduce(jnp.add, [_dot(p_scr[rs, bounds[j]:bounds[j + 1]], v[...])
                                        for j, v in enumerate(v_refs)])
        o_ref[rs, :] = (o2[:, :dv] / o2[:, dv:]).astype(o_ref.dtype)


def attention(q, k, v, *, bsz, n_q, tq, q_row0, kv_parts, heads, kv_group, dk, dv, name):
    assert tq % min(ATTN_SPLIT_ROWS, tq) == 0 and tq % ATTN_EXP_ROWS == 0
    nq_t = n_q // tq
    n_keys = sum(n_rows for _, n_rows in kv_parts)
    in_specs = [pl.BlockSpec((tq, dk), lambda b, h, i: (q_row0 // tq + b * nq_t + i, h))]
    for row0, n_rows in kv_parts:
        in_specs.append(pl.BlockSpec(
            (n_rows, dk), lambda b, h, i, row0=row0, n_rows=n_rows: (row0 // n_rows + b, h // kv_group)))
    for row0, n_rows in kv_parts:
        in_specs.append(pl.BlockSpec(
            (n_rows, 2 * dv), lambda b, h, i, row0=row0, n_rows=n_rows: (row0 // n_rows + b, h // kv_group)))
    n_parts = len(kv_parts)
    return pl.pallas_call(
        functools.partial(_attn_kernel, n_parts=n_parts),
        grid=(bsz, heads, nq_t),
        in_specs=in_specs,
        out_specs=pl.BlockSpec((tq, dv), lambda b, h, i: (b * nq_t + i, h)),
        out_shape=jax.ShapeDtypeStruct((bsz * n_q, heads * dv), BF16),
        scratch_shapes=[pltpu.VMEM((tq, n_keys), F32), pltpu.VMEM((tq, n_keys), BF16),
                        pltpu.VMEM((tq, 1), F32)],
        compiler_params=_cparams(3),
        name=name,
    )(q, *([k] * n_parts), *([v] * n_parts))


def dft_tables(n, half_shift):
    f = jnp.arange(n, dtype=jnp.int32)[:, None]
    s = jnp.arange(n, dtype=jnp.int32)[None, :]
    if half_shift:
        phase = ((2 * f + 1) * (2 * s + 1)) % (8 * n)
        ang = phase.astype(F32) * (2.0 * math.pi / (8 * n))
    else:
        phase = ((2 * f + 1) * s) % (4 * n)
        ang = phase.astype(F32) * (2.0 * math.pi / (4 * n))
    return jnp.cos(ang).astype(BF16), jnp.sin(ang).astype(BF16)


def hyena_time_filters(n, w1, b1, w2, b2, w3, freq):
    t = jnp.linspace(0.0, 1.0, n, dtype=F32)[:, None]
    ang = 2.0 * math.pi * jnp.arange(n, dtype=F32)[:, None] / n
    bands = jnp.linspace(1e-4, HY_BANDS - 1, HY_BANDS, dtype=F32)
    feats = jnp.concatenate([t, jnp.cos(ang * bands), jnp.sin(ang * bands)], axis=-1)
    fr = freq.astype(F32)
    hp = lax.Precision.HIGHEST
    hdn = jnp.sin(fr * (jnp.dot(feats, w1, precision=hp) + b1))
    hdn = jnp.sin(fr * (jnp.dot(hdn, w2, precision=hp) + b2))
    h = jnp.dot(hdn, w3, precision=hp).reshape(n, HY_ORDER, 2, HY_W)
    deltas = jnp.abs(jnp.linspace(math.log(HY_TARGET) / HY_SLOW, math.log(HY_TARGET) / HY_FAST,
                                  HY_W, dtype=F32))
    h = h * jnp.exp(-t * deltas)[:, None, None, :]
    h_fwd = h[:, :, 0]
    h_bwd = h[:, :, 1].at[0].set(0.0)
    r = lax.rsqrt(jnp.sum(h_fwd * h_fwd, axis=0) + jnp.sum(h_bwd * h_bwd, axis=0) + EPS)
    kf = (h_fwd * r).reshape(n, HY_ORDER * HY_W)
    kb = (h_bwd * r).reshape(n, HY_ORDER * HY_W)
    return kf + kb, kb - kf


HY_CHUNK = 256


def _spectra_kernel(ks_ref, kd_ref, c_ref, s_ref, kr_ref, ki_ref, *, inv_n):
    kr_ref[...] = _dot(c_ref[...], ks_ref[...].astype(BF16)) * inv_n
    ki_ref[...] = _dot(s_ref[...], kd_ref[...].astype(BF16)) * inv_n


def hyena_spectra(ksum, kdiff, cmat, smat):
    n, cols = ksum.shape
    spec = pl.BlockSpec((n, HY_CHUNK), lambda i: (0, i))
    return pl.pallas_call(
        functools.partial(_spectra_kernel, inv_n=1.0 / n),
        grid=(cols // HY_CHUNK,),
        in_specs=[spec, spec, _resident((n, n)), _resident((n, n))],
        out_specs=[spec, spec],
        out_shape=[jax.ShapeDtypeStruct((n, cols), F32)] * 2,
        compiler_params=_cparams(1),
        name="hyena_spectra",
    )(ksum, kdiff, cmat, smat)


def _hyena_kernel(v_ref, x1_ref, x2_ref, wv_ref, w1_ref, w2_ref, skip_ref,
                  k1r_ref, k1i_ref, k2r_ref, k2i_ref, c_ref, s_ref, o_ref, zb_scr, y_scr, *, n, fc):
    lane_groups = [slice(h * LANES, (h + 1) * LANES) for h in range(o_ref.shape[1] // LANES)]

    def short_conv(u_ref, w_ref, cs):
        u = u_ref[:, cs].astype(F32)
        pos = lax.broadcasted_iota(jnp.int32, u.shape, 0)
        prev = jnp.where(pos == 0, 0.0, pltpu.roll(u, 1, 0))
        nxt = jnp.where(pos == n - 1, 0.0, pltpu.roll(u, n - 1, 0))
        return prev * w_ref[0:1, cs] + u * w_ref[1:2, cs] + nxt * w_ref[2:3, cs]

    def long_conv(kr_ref, ki_ref):
        zb = zb_scr[...]
        for j in range(n // fc):
            sl = slice(j * fc, (j + 1) * fc)
            a = _dot(c_ref[sl, :], zb)
            b = _dot(s_ref[sl, :], zb)
            kr = kr_ref[sl, :]
            ki = ki_ref[sl, :]
            yr = (a * kr + b * ki).astype(BF16)
            yi = (a * ki - b * kr).astype(BF16)
            y_scr[...] += _dot(c_ref[:, sl], yr) - _dot(s_ref[:, sl], yi)

    for cs in lane_groups:
        v = short_conv(v_ref, wv_ref, cs)
        zb_scr[:, cs] = v.astype(BF16)
        y_scr[:, cs] = v * skip_ref[0:1, cs]
    long_conv(k1r_ref, k1i_ref)
    for cs in lane_groups:
        z = short_conv(x1_ref, w1_ref, cs) * y_scr[:, cs]
        zb_scr[:, cs] = z.astype(BF16)
        y_scr[:, cs] = z * skip_ref[1:2, cs]
    long_conv(k2r_ref, k2i_ref)
    for cs in lane_groups:
        o_ref[:, cs] = (short_conv(x2_ref, w2_ref, cs) * y_scr[:, cs]).astype(o_ref.dtype)


def hyena_mixer(p, conv_w, skip, kr, ki, cmat, smat, *, bsz, n, row0):
    ch = HY_CHUNK
    nc = HY_W // ch
    fc = min(512, n)
    blk0 = row0 // n

    def u_spec(k):
        return pl.BlockSpec((n, ch), lambda c, b: (blk0 + b, k * nc + c))

    def w_spec(k):
        return pl.BlockSpec((3, ch), lambda c, b: (0, k * nc + c))

    def k_spec(order):
        return pl.BlockSpec((n, ch), lambda c, b: (0, order * nc + c), pipeline_mode=pl.Buffered(1))

    return pl.pallas_call(
        functools.partial(_hyena_kernel, n=n, fc=fc),
        grid=(nc, bsz),
        in_specs=[u_spec(0), u_spec(1), u_spec(2), w_spec(0), w_spec(1), w_spec(2),
                  pl.BlockSpec((HY_ORDER, ch), lambda c, b: (0, c)),
                  k_spec(0), k_spec(0), k_spec(1), k_spec(1),
                  _resident((n, n)), _resident((n, n))],
        out_specs=pl.BlockSpec((n, ch), lambda c, b: (b, c)),
        out_shape=jax.ShapeDtypeStruct((bsz * n, HY_W), BF16),
        scratch_shapes=[pltpu.VMEM((n, ch), BF16), pltpu.VMEM((n, ch), F32)],
        compiler_params=_cparams(2),
        name="hyena_n%d" % n,
    )(p, p, p, conv_w, conv_w, conv_w, skip, kr, ki, kr, ki, cmat, smat)


def _out_proj_kernel(ho_ref, mo_ref, go_ref, wh_ref, wm_ref, wg_ref, x_ref, g1_ref, g2_ref,
                     mod_ref, rwh_ref, rwl_ref, xo_ref, f_ref, lg_ref):
    mix = _dot(ho_ref[...], wh_ref[...]) + _dot(mo_ref[...], wm_ref[...]) + _dot(go_ref[...], wg_ref[...])
    xn = x_ref[...] + mod_ref[0, 2:3, :] * (_rms(mix) * g1_ref[...])
    xo_ref[...] = xn
    f = (_rms(xn) * g2_ref[...]) * (1.0 + mod_ref[0, 4:5, :]) + mod_ref[0, 3:4, :]
    fh = f.astype(BF16)
    fl = (f - fh.astype(F32)).astype(BF16)
    f_ref[...] = fh
    lg_ref[...] = _dot_nt(rwh_ref[...], fh) + _dot_nt(rwh_ref[...], fl) + _dot_nt(rwl_ref[...], fh)


def out_proj(ho, mo, go, wh, wm, wg, x_all, g1, g2, mod, rwh, rwl, tm, mod_idx):
    t = ho.shape[0]
    d = D_MODEL
    row = lambda w: pl.BlockSpec((tm, w), lambda i: (i, 0))
    vec = pl.BlockSpec((1, d), lambda i: (0, 0))
    return pl.pallas_call(
        _out_proj_kernel,
        grid=(t // tm,),
        in_specs=[row(ho.shape[1]), row(mo.shape[1]), row(go.shape[1]),
                  _resident(wh.shape), _resident(wm.shape), _resident(wg.shape),
                  row(d), vec, vec,
                  pl.BlockSpec((1, N_MOD, d), lambda i: (mod_idx(i), 0, 0)),
                  _resident(rwh.shape), _resident(rwl.shape)],
        out_specs=[row(d), row(d), pl.BlockSpec((N_EXPERTS, tm), lambda i: (0, i))],
        out_shape=[jax.ShapeDtypeStruct((t, d), F32),
                   jax.ShapeDtypeStruct((t, d), BF16),
                   jax.ShapeDtypeStruct((N_EXPERTS, t), F32)],
        compiler_params=_cparams(1),
        name="out_proj",
    )(ho, mo, go, wh, wm, wg, x_all, g1, g2, mod, rwh, rwl)


def _moe_kernel(be_ref, bsrc_ref, bflag_ref, bnext_ref, bslot_ref, x_ref, wg_hbm, wu_hbm, wd_hbm, o_ref,
                wg_f, wu_f, wd_f, wg_s, wu_s, wd_s, sem, *, layer):
    i = pl.program_id(0)

    def weight_copies(expert, slot):
        pairs = ((wg_hbm, wg_f), (wu_hbm, wu_f), (wd_hbm, wd_f))
        return [pltpu.make_async_copy(src.at[layer, expert], dst.at[slot], sem.at[k, slot])
                for k, (src, dst) in enumerate(pairs)]

    @pl.when(bflag_ref[i] == 2)
    def _():
        expert, slot, nxt = be_ref[i], bslot_ref[i], bnext_ref[i]

        @pl.when(i == 0)
        def _():
            for cp in weight_copies(expert, slot):
                cp.start()

        for cp in weight_copies(expert, slot):
            cp.wait()

        @pl.when(nxt >= 0)
        def _():
            for cp in weight_copies(nxt, 1 - slot):
                cp.start()

        wg_s[...] = wg_f[slot].astype(BF16)
        wu_s[...] = wu_f[slot].astype(BF16)
        wd_s[...] = wd_f[slot].astype(BF16)

    @pl.when(bflag_ref[i] != 0)
    def _():
        x = x_ref[...]
        a = _silu(_dot(x, wg_s[...])) * _dot(x, wu_s[...])
        o_ref[...] = _dot(a.astype(BF16), wd_s[...]).astype(o_ref.dtype)

    @pl.when(bflag_ref[i] == 0)
    def _():
        o_ref[...] = jnp.zeros(o_ref.shape, o_ref.dtype)


def moe_experts(blk_expert, blk_src, blk_flag, blk_next, blk_slot, xs, wg, wu, wd, layer):
    n_rows, d = xs.shape
    hid = wg.shape[3]
    hbm = pl.BlockSpec(memory_space=pl.ANY)
    grid_spec = pltpu.PrefetchScalarGridSpec(
        num_scalar_prefetch=5,
        grid=(n_rows // MOE_BLOCK,),
        in_specs=[pl.BlockSpec((MOE_BLOCK, d), lambda i, be, bs, bf, bn, bl: (bs[i], 0)), hbm, hbm, hbm],
        out_specs=pl.BlockSpec((MOE_BLOCK, d), lambda i, be, bs, bf, bn, bl: (i, 0)),
        scratch_shapes=[pltpu.VMEM((2, d, hid), F32), pltpu.VMEM((2, d, hid), F32),
                        pltpu.VMEM((2, hid, d), F32),
                        pltpu.VMEM((d, hid), BF16), pltpu.VMEM((d, hid), BF16),
                        pltpu.VMEM((hid, d), BF16),
                        pltpu.SemaphoreType.DMA((3, 2))],
    )
    return pl.pallas_call(
        functools.partial(_moe_kernel, layer=layer),
        grid_spec=grid_spec,
        out_shape=jax.ShapeDtypeStruct((n_rows, d), BF16),
        compiler_params=_cparams(1),
        name="moe_experts",
    )(blk_expert, blk_src, blk_flag, blk_next, blk_slot, xs, wg, wu, wd)


def _shared_expert_kernel(f_ref, sg_ref, su_ref, sd_ref, o_ref):
    f = f_ref[...]
    a = _silu(_dot(f, sg_ref[...])) * _dot(f, su_ref[...])
    o_ref[...] = _dot(a.astype(BF16), sd_ref[...]).astype(o_ref.dtype)


def shared_expert(f, sg, su, sd, tm):
    t, d = f.shape
    row = pl.BlockSpec((tm, d), lambda i: (i, 0))
    return pl.pallas_call(
        _shared_expert_kernel,
        grid=(t // tm,),
        in_specs=[row, _resident(sg.shape), _resident(su.shape), _resident(sd.shape)],
        out_specs=row,
        out_shape=jax.ShapeDtypeStruct((t, d), BF16),
        compiler_params=_cparams(1),
        cost_estimate=pl.CostEstimate(flops=6 * t * d * sg.shape[1], transcendentals=t * sg.shape[1],
                                      bytes_accessed=4 * t * d + 6 * d * sg.shape[1]),
        name="shared_expert",
    )(f, sg, su, sd)


def _ffn_combine_kernel(ys_ref, yg_ref, gate_ref, x_ref, g3_ref, mod_ref, o_ref):
    y = ys_ref[...].astype(F32)
    gates = gate_ref[...]
    for k in range(TOP_K):
        y = y + gates[:, k:k + 1] * yg_ref[k].astype(F32)
    o_ref[...] = x_ref[...] + mod_ref[0, 5:6, :] * (_rms(y) * g3_ref[...])


def ffn_combine(y_shared, y_gathered, gates, x_all, g3, mod, tm, mod_idx):
    t, d = y_shared.shape
    row = pl.BlockSpec((tm, d), lambda i: (i, 0))
    return pl.pallas_call(
        _ffn_combine_kernel,
        grid=(t // tm,),
        in_specs=[row, pl.BlockSpec((TOP_K, tm, d), lambda i: (0, i, 0)),
                  pl.BlockSpec((tm, gates.shape[1]), lambda i: (i, 0)), row,
                  pl.BlockSpec((1, d), lambda i: (0, 0)),
                  pl.BlockSpec((1, N_MOD, d), lambda i: (mod_idx(i), 0, 0))],
        out_specs=row,
        out_shape=jax.ShapeDtypeStruct((t, d), F32),
        compiler_params=_cparams(1),
        name="ffn_combine",
    )(y_shared, y_gathered, gates, x_all, g3, mod)


def _first_max(vals, idx, n):
    m = jnp.max(vals, axis=0, keepdims=True)
    return m, jnp.min(jnp.where(vals == m, idx, float(n)), axis=0, keepdims=True)


def _router_kernel(lg_ref, bias_ref, e_ref, g_ref, r_ref, cnt_ref, carry_ref):
    @pl.when(pl.program_id(0) == 0)
    def _():
        carry_ref[...] = jnp.zeros(carry_ref.shape, F32)

    lg = lg_ref[...]
    tt = lg.shape[1]
    gsz = N_EXPERTS // N_GROUPS
    neg = -jnp.inf
    scores = 1.0 / (1.0 + jnp.exp(-lg))
    biased = scores + bias_ref[...]
    sub = lax.broadcasted_iota(jnp.int32, (gsz, tt), 0).astype(F32)

    grp_rows = []
    for g in range(N_GROUPS):
        blk = biased[g * gsz:(g + 1) * gsz, :]
        m1, i1 = _first_max(blk, sub, gsz)
        m2 = jnp.max(jnp.where(sub == i1, neg, blk), axis=0, keepdims=True)
        grp_rows.append(m1 + m2)
    cur = jnp.concatenate(grp_rows, axis=0)
    gidx = lax.broadcasted_iota(jnp.int32, (N_GROUPS, tt), 0).astype(F32)
    gsel = jnp.zeros((N_GROUPS, tt), F32)
    for _ in range(TOPK_GROUPS):
        _, gi = _first_max(cur, gidx, N_GROUPS)
        hit = gidx == gi
        gsel = jnp.where(hit, 1.0, gsel)
        cur = jnp.where(hit, neg, cur)
    emask = jnp.concatenate([jnp.broadcast_to(gsel[g:g + 1, :], (gsz, tt)) for g in range(N_GROUPS)], axis=0)

    cand = jnp.where(emask > 0.0, biased, neg)
    eidx = lax.broadcasted_iota(jnp.int32, (N_EXPERTS, tt), 0).astype(F32)
    chosen = jnp.zeros((N_EXPERTS, tt), F32)
    e_rows, g_rows = [], []
    for _ in range(TOP_K):
        _, ei = _first_max(cand, eidx, N_EXPERTS)
        hit = eidx == ei
        e_rows.append(ei)
        g_rows.append(jnp.sum(jnp.where(hit, scores, 0.0), axis=0, keepdims=True))
        chosen = jnp.where(hit, 1.0, chosen)
        cand = jnp.where(hit, neg, cand)
    gsum = functools.reduce(jnp.add, g_rows)
    g_rows = [g / gsum * ROUTED_SCALE for g in g_rows]

    before = (lax.broadcasted_iota(jnp.int32, (tt, tt), 0) < lax.broadcasted_iota(jnp.int32, (tt, tt), 1))
    prefix = _dot(chosen.astype(BF16), jnp.where(before, 1.0, 0.0).astype(BF16))
    rank_all = prefix + carry_ref[...]
    r_rows = [jnp.sum(jnp.where(eidx == ei, rank_all, 0.0), axis=0, keepdims=True) for ei in e_rows]
    carry_ref[...] = carry_ref[...] + jnp.sum(chosen, axis=1, keepdims=True)
    cnt_ref[...] = carry_ref[...]

    pad = [jnp.zeros((8 - TOP_K, tt), F32)]
    e_ref[...] = jnp.concatenate(e_rows + pad, axis=0).astype(jnp.int32)
    g_ref[...] = jnp.concatenate(g_rows + pad, axis=0)
    r_ref[...] = jnp.concatenate(r_rows + pad, axis=0).astype(jnp.int32)


def router(logits_t, bias, tt):
    n_exp, t = logits_t.shape
    col = pl.BlockSpec((8, tt), lambda i: (0, i))
    return pl.pallas_call(
        _router_kernel,
        grid=(t // tt,),
        in_specs=[pl.BlockSpec((n_exp, tt), lambda i: (0, i)),
                  pl.BlockSpec((n_exp, 1), lambda i: (0, 0))],
        out_specs=[col, col, col, pl.BlockSpec((n_exp, 1), lambda i: (0, 0))],
        out_shape=[jax.ShapeDtypeStruct((8, t), jnp.int32), jax.ShapeDtypeStruct((8, t), F32),
                   jax.ShapeDtypeStruct((8, t), jnp.int32), jax.ShapeDtypeStruct((n_exp, 1), F32)],
        scratch_shapes=[pltpu.VMEM((n_exp, 1), F32)],
        compiler_params=_cparams(1),
        name="router",
    )(logits_t, bias.reshape(n_exp, 1).astype(F32))


def rope_tables(n, rot_dim, tm):
    rows = n // GRID_W
    row = jnp.repeat(jnp.arange(rows, dtype=F32), GRID_W)
    col = jnp.tile(jnp.arange(GRID_W, dtype=F32), rows)
    axis_dim = rot_dim // 2
    inv = ROPE_THETA ** (-jnp.arange(0, axis_dim, 2, dtype=F32) / axis_dim)
    ar, ac = row[:, None] * inv, col[:, None] * inv
    zero = jnp.zeros_like(ar)
    c = jnp.concatenate([jnp.cos(ar), jnp.cos(ar), jnp.cos(ac), jnp.cos(ac)], axis=-1)
    s1 = jnp.concatenate([-jnp.sin(ar), zero, -jnp.sin(ac), zero], axis=-1)
    s2 = jnp.concatenate([zero, jnp.sin(ar), zero, jnp.sin(ac)], axis=-1)

    def finish(tab, fill):
        tab = jnp.pad(tab, ((0, 0), (0, LANES - rot_dim)), constant_values=fill)
        return jnp.concatenate([tab, jnp.full((tm, LANES), fill, F32)], axis=0)

    return finish(c, 1.0), finish(s1, 0.0), finish(s2, 0.0)


def pack_w_in(w):
    hy = 3 * HY_W
    kv0 = hy + MLA_Q_RANK + GQA_HEADS * GQA_HEAD_DIM
    ckv = w[:, kv0:kv0 + MLA_KV_RANK]
    kr = w[:, kv0 + MLA_KV_RANK:kv0 + MLA_KV_RANK + MLA_ROPE]
    gk0 = kv0 + MLA_KV_RANK + MLA_ROPE
    gkv = w[:, gk0:]
    kr = jnp.pad(kr, ((0, 0), (0, LANES - MLA_ROPE)))
    return jnp.concatenate([w[:, :kv0], ckv, gkv, kr], axis=1).astype(BF16)


def pack_w_uq(w):
    w = w.reshape(MLA_Q_RANK, MLA_HEADS, MLA_NOPE + MLA_ROPE)
    w = jnp.pad(w, ((0, 0), (0, 0), (0, MLA_QK_PAD - MLA_NOPE - MLA_ROPE)))
    return w.reshape(MLA_Q_RANK, MLA_HEADS * MLA_QK_PAD).astype(BF16)


def pack_w_ukv(w):
    w = w.reshape(MLA_KV_RANK, MLA_HEADS, MLA_NOPE + MLA_V)
    k = w[:, :, :MLA_NOPE].reshape(MLA_KV_RANK, MLA_HEADS * MLA_NOPE)
    v = w[:, :, MLA_NOPE:].reshape(MLA_KV_RANK, MLA_HEADS * MLA_V)
    return jnp.concatenate([k, v], axis=1).astype(BF16)


def dispatch_plan(e_t, rank_t, counts):
    n_tok = e_t.shape[1]
    n_pairs = n_tok * TOP_K
    counts = counts.astype(jnp.int32)
    padded = (counts + MOE_BLOCK - 1) // MOE_BLOCK * MOE_BLOCK
    p_ends = jnp.cumsum(padded)
    p_starts = p_ends - padded
    n_blocks = -(-n_pairs // MOE_BLOCK) + N_EXPERTS
    n_rows = n_blocks * MOE_BLOCK
    blk = jnp.arange(n_blocks, dtype=jnp.int32)
    n_used = p_ends[-1] // MOE_BLOCK
    blk_src = jnp.minimum(blk, n_used - 1)
    blk_expert = jnp.sum((blk_src * MOE_BLOCK)[:, None] >= p_ends[None, :], axis=1).astype(jnp.int32)
    first = jnp.concatenate([jnp.ones((1,), bool), blk_expert[1:] != blk_expert[:-1]])
    blk_flag = jnp.where(blk < n_used, jnp.where(first, 2, 1), 0).astype(jnp.int32)

    experts = jnp.arange(N_EXPERTS, dtype=jnp.int32)
    used = counts > 0
    later = jnp.where(used[None, :] & (experts[None, :] > experts[:, None]), experts[None, :], N_EXPERTS)
    next_used = jnp.min(later, axis=1)
    next_used = jnp.where(next_used == N_EXPERTS, -1, next_used).astype(jnp.int32)
    slot_of = ((jnp.cumsum(used.astype(jnp.int32)) - 1) % 2).astype(jnp.int32)
    blk_next = next_used[blk_expert]
    blk_slot = slot_of[blk_expert]

    onehot = e_t[:, :, None] == experts
    pos = jnp.sum(jnp.where(onehot, p_starts, 0), axis=-1) + rank_t

    stride = n_tok + 1
    tok = jnp.arange(n_tok, dtype=jnp.int32)
    real_keys = (e_t * stride + tok[None, :]).reshape(-1)
    cum_fill = jnp.cumsum(padded - counts)
    filler = jnp.arange(n_rows - n_pairs, dtype=jnp.int32)
    filler_exp = jnp.sum(filler[:, None] >= cum_fill[None, :], axis=1).astype(jnp.int32)
    filler_keys = filler_exp * stride + n_tok
    slot_tok = jnp.sort(jnp.concatenate([real_keys, filler_keys])) % stride
    spread = jnp.arange(n_rows, dtype=jnp.int32) % n_tok
    slot_tok = jnp.where(slot_tok == n_tok, spread, slot_tok)
    return (blk_expert, blk_src, blk_flag, blk_next, blk_slot), slot_tok, pos


def _pick_tile(*sizes):
    for tile in (512, 256, 128):
        if all(s % tile == 0 for s in sizes):
            return tile
    raise ValueError("row counts must be multiples of 128: %r" % (sizes,))


def kernel(x, c, ctx, c_ctx, ada_w, ada_b, norm_g, w_in, w_out, hy_conv, hy_w1, hy_b1, hy_w2, hy_b2,
           hy_w3, hy_freq, hy_skip, mla_q_norm, mla_kv_norm, mla_w_uq, mla_w_ukv, gqa_q_norm, gqa_k_norm,
           router_w, router_bias, exp_w_gate, exp_w_up, exp_w_down, sh_w_gate, sh_w_up, sh_w_down):
    bsz, n, d = x.shape
    n_ctx = ctx.shape[1]
    depth = ada_w.shape[0]
    assert d == D_MODEL and n % GRID_W == 0
    t_lat, t_ctx = bsz * n, bsz * n_ctx
    assert t_lat % n_ctx == 0
    tm = _pick_tile(n, t_ctx)
    tq = max(t for t in (1024, 512, 256) if n % t == 0)
    lat_tiles = t_lat // tm
    tiles_per_seq = n // tm
    tm_ffn = min(tm, 256)

    def mod_idx(tile):
        return lambda i: jnp.minimum(i // (n // tile), bsz)

    def rope_idx(i):
        return jnp.where(i < lat_tiles, i % tiles_per_seq, tiles_per_seq)

    x_all = jnp.concatenate([x.reshape(t_lat, d), ctx.reshape(t_ctx, d)], axis=0)
    mod_rows = -(-(bsz + 1) // 16) * 16
    s_in = jnp.concatenate([c, c_ctx[None], jnp.zeros((mod_rows - bsz - 1, d), F32)], axis=0)
    mod_all = adaln_all(s_in, ada_w, ada_b).reshape(depth, mod_rows, N_MOD, d)

    tabs_mla = rope_tables(n, MLA_ROPE, tm)
    tabs_gqa = rope_tables(n, GQA_HEAD_DIM, tm)
    mla_scale = (MLA_NOPE + MLA_ROPE) ** -0.5 * math.log2(math.e)
    gqa_scale = GQA_HEAD_DIM ** -0.5 * math.log2(math.e)
    dft ={m: (dft_tables(m, False), dft_tables(m, True)) for m in (n, n_ctx)}

    for l in range(depth):
        last = l == depth - 1
        mod = mod_all[l]
        vec = lambda a: a.reshape(1, -1)

        p = in_proj(x_all, vec(norm_g[l, 0]), mod, pack_w_in(w_in[l]), tm, mod_idx(tm))
        q_m = mla_q(p, vec(mla_q_norm[l]), pack_w_uq(mla_w_uq[l]), tabs_mla, tm, rope_idx, mla_scale)
        k_m, v_m = mla_kv(p, vec(mla_kv_norm[l]), pack_w_ukv(mla_w_ukv[l]), tabs_mla, tm, rope_idx)
        q_g, k_g, v_g = gqa_qkv(p, vec(gqa_q_norm[l]), vec(gqa_k_norm[l]), tabs_gqa, tm, rope_idx, gqa_scale)

        mla_args = dict(bsz=bsz, heads=MLA_HEADS, kv_group=1, dk=MLA_QK_PAD, dv=MLA_V)
        gqa_args = dict(bsz=bsz, heads=GQA_HEADS, kv_group=GQA_HEADS // GQA_KV_HEADS, dk=GQA_HEAD_DIM,
                        dv=GQA_HEAD_DIM)
        lat_q = dict(n_q=n, tq=tq, q_row0=0, kv_parts=[(t_lat, n_ctx), (0, n)])
        mo = attention(q_m, k_m, v_m, name="mla_attn", **lat_q, **mla_args)
        go = attention(q_g, k_g, v_g, name="gqa_attn", **lat_q, **gqa_args)

        filt = (hy_w1[l], hy_b1[l], hy_w2[l], hy_b2[l], hy_w3[l], hy_freq[l])

        def hyena(m, row0):
            lag_tabs, sym_tabs = dft[m]
            kr, ki = hyena_spectra(*hyena_time_filters(m, *filt), *lag_tabs)
            return hyena_mixer(p, hy_conv[l], hy_skip[l], kr, ki, *sym_tabs, bsz=bsz, n=m, row0=row0)

        ho = hyena(n, 0)

        if not last:
            ctx_q = dict(n_q=n_ctx, tq=n_ctx, q_row0=t_lat, kv_parts=[(t_lat, n_ctx)])
            mo = jnp.concatenate([mo, attention(q_m, k_m, v_m, name="mla_attn_ctx", **ctx_q, **mla_args)])
            go = jnp.concatenate([go, attention(q_g, k_g, v_g, name="gqa_attn_ctx", **ctx_q, **gqa_args)])
            ho = jnp.concatenate([ho, hyena(n_ctx, t_lat)])

        wo = w_out[l].astype(BF16)
        rw_t = router_w[l].T
        rw_hi = rw_t.astype(BF16)
        rw_lo = (rw_t - rw_hi.astype(F32)).astype(BF16)
        x_mid, f, logits_t = out_proj(ho, mo, go, wo[:HY_W], wo[HY_W:HY_W + MLA_HEADS * MLA_V],
                                      wo[HY_W + MLA_HEADS * MLA_V:], x_all, vec(norm_g[l, 1]),
                                      vec(norm_g[l, 2]), mod, rw_hi, rw_lo, tm, mod_idx(tm))

        e_t, gate_t, rank_t, counts = router(logits_t, router_bias[l], tm)
        blocks, slot_tok, pos = dispatch_plan(e_t[:TOP_K], rank_t[:TOP_K], counts[:, 0])
        xs = f[slot_tok]
        y_shared = shared_expert(f, sh_w_gate[l].astype(BF16), sh_w_up[l].astype(BF16),
                                 sh_w_down[l].astype(BF16), tm)
        ys = moe_experts(*blocks, xs, exp_w_gate, exp_w_up, exp_w_down, l)
        x_all = ffn_combine(y_shared, ys[pos], gate_t.T, x_mid, vec(norm_g[l, 3]), mod, tm_ffn, mod_idx(tm_ffn))

    return x_all.reshape(bsz, n, d)
```

```python
import functools
import math

import jax
import jax.numpy as jnp
from jax import lax
from jax.experimental import pallas as pl
from jax.experimental.pallas import tpu as pltpu

F32 = jnp.float32
BF16 = jnp.bfloat16

D_MODEL = 2048
GRID_W = 64
EPS = 1e-6
N_MOD = 6
HY_W = D_MODEL // 4
HY_ORDER = 2
HY_BANDS = 16
HY_TARGET = 1e-2
HY_FAST = 0.3
HY_SLOW = 1.5
MLA_NOPE = 128
MLA_ROPE = 64
MLA_V = 128
MLA_HEADS = 6
MLA_Q_RANK = 768
MLA_KV_RANK = 256
GQA_HEAD_DIM = 128
GQA_HEADS = 6
GQA_KV_HEADS = 2
ROPE_THETA = 10000.0
N_EXPERTS = 64
TOP_K = 6
N_GROUPS = 8
TOPK_GROUPS = 4
EXPERT_HIDDEN = D_MODEL // 4
ROUTED_SCALE = 2.5

LANES = 128
VMEM_LIMIT_BYTES = 56 * 1024 * 1024

COL_HY = 0
COL_MQ = 3 * HY_W
COL_GQ = COL_MQ + MLA_Q_RANK
COL_CKV = COL_GQ + GQA_HEADS * GQA_HEAD_DIM
COL_GK = COL_CKV + MLA_KV_RANK
COL_GV = COL_GK + GQA_KV_HEADS * GQA_HEAD_DIM
COL_KR = COL_GV + GQA_KV_HEADS * GQA_HEAD_DIM
IN_COLS_PAD = COL_KR + LANES
MLA_QK_PAD = 2 * LANES
MOE_BLOCK = 256

NT_DIMS = (((1,), (1,)), ((), ()))


def _cparams(n_axes):
    return pltpu.CompilerParams(dimension_semantics=("arbitrary",) * n_axes,
                                vmem_limit_bytes=VMEM_LIMIT_BYTES)


def _resident(shape):
    nd = len(shape)
    return pl.BlockSpec(shape, lambda *_: (0,) * nd, pipeline_mode=pl.Buffered(1))


def _rms(x):
    return x * lax.rsqrt(jnp.mean(x * x, axis=-1, keepdims=True) + EPS)


def _silu(x):
    return x / (1.0 + jnp.exp(-x))


def _dot(a, b):
    return jnp.dot(a, b, preferred_element_type=F32)


def _dot_nt(a, b):
    return lax.dot_general(a, b, NT_DIMS, preferred_element_type=F32)


def _rope(x, c, s1, s2, shift):
    w = x.shape[-1]
    return x * c + pltpu.roll(x, w - shift, 1) * s1 + pltpu.roll(x, shift, 1) * s2


def _adaln_kernel(s_ref, w_ref, b_ref, o_ref):
    s = _silu(s_ref[...]).astype(BF16)
    o_ref[0] = _dot(s, w_ref[0].astype(BF16)) + b_ref[0]


def adaln_all(s_in, ada_w, ada_b):
    depth, d, n_out = ada_w.shape
    rows = s_in.shape[0]
    tn = 1024
    return pl.pallas_call(
        _adaln_kernel,
        grid=(depth, n_out // tn),
        in_specs=[pl.BlockSpec((rows, d), lambda l, j: (0, 0)),
                  pl.BlockSpec((1, d, tn), lambda l, j: (l, 0, j)),
                  pl.BlockSpec((1, 1, tn), lambda l, j: (l, 0, j))],
        out_specs=pl.BlockSpec((1, rows, tn), lambda l, j: (l, 0, j)),
        out_shape=jax.ShapeDtypeStruct((depth, rows, n_out), F32),
        compiler_params=_cparams(2),
        name="adaln",
    )(s_in, ada_w, ada_b.reshape(depth, 1, n_out))


def _in_proj_kernel(x_ref, g_ref, mod_ref, w_ref, o_ref):
    h = _rms(x_ref[...]) * g_ref[...]
    h = h * (1.0 + mod_ref[0, 1:2, :]) + mod_ref[0, 0:1, :]
    o_ref[...] = _dot(h.astype(BF16), w_ref[...]).astype(o_ref.dtype)


def in_proj(x_all, g, mod, w, tm, mod_idx):
    t, d = x_all.shape
    n_out = w.shape[1]
    return pl.pallas_call(
        _in_proj_kernel,
        grid=(t // tm,),
        in_specs=[pl.BlockSpec((tm, d), lambda i: (i, 0)),
                  pl.BlockSpec((1, d), lambda i: (0, 0)),
                  pl.BlockSpec((1, N_MOD, d), lambda i: (mod_idx(i), 0, 0)),
                  _resident((d, n_out))],
        out_specs=pl.BlockSpec((tm, n_out), lambda i: (i, 0)),
        out_shape=jax.ShapeDtypeStruct((t, n_out), BF16),
        compiler_params=_cparams(1),
        name="in_proj",
    )(x_all, g, mod, w)


def _mla_q_kernel(cq_ref, qn_ref, w_ref, c_ref, s1_ref, s2_ref, o_ref, *, scale):
    hn = (_rms(cq_ref[...].astype(F32)) * qn_ref[...]).astype(BF16)
    q = _dot(hn, w_ref[...]) * scale
    c, s1, s2 = c_ref[...], s1_ref[...], s2_ref[...]
    for h in range(MLA_HEADS):
        lo = h * MLA_QK_PAD
        o_ref[:, lo:lo + MLA_NOPE] = q[:, lo:lo + MLA_NOPE].astype(BF16)
        r = q[:, lo + MLA_NOPE:lo + MLA_QK_PAD]
        o_ref[:, lo + MLA_NOPE:lo + MLA_QK_PAD] = _rope(r, c, s1, s2, MLA_ROPE // 4).astype(BF16)


def mla_q(p, q_norm, w_uq, tabs, tm, rope_idx, scale):
    t = p.shape[0]
    n_out = MLA_HEADS * MLA_QK_PAD
    tab_spec = pl.BlockSpec((tm, LANES), lambda i: (rope_idx(i), 0))
    return pl.pallas_call(
        functools.partial(_mla_q_kernel, scale=scale),
        grid=(t // tm,),
        in_specs=[pl.BlockSpec((tm, MLA_Q_RANK), lambda i: (i, COL_MQ // MLA_Q_RANK)),
                  pl.BlockSpec((1, MLA_Q_RANK), lambda i: (0, 0)),
                  _resident((MLA_Q_RANK, n_out)),
                  tab_spec, tab_spec, tab_spec],
        out_specs=pl.BlockSpec((tm, n_out), lambda i: (i, 0)),
        out_shape=jax.ShapeDtypeStruct((t, n_out), BF16),
        compiler_params=_cparams(1),
        name="mla_q",
    )(p, q_norm, w_uq, *tabs)


def _mla_kv_kernel(ckv_ref, kr_ref, kvn_ref, w_ref, c_ref, s1_ref, s2_ref, k_ref, v_ref):
    hn = (_rms(ckv_ref[...].astype(F32)) * kvn_ref[...]).astype(BF16)
    kv = _dot(hn, w_ref[...])
    kr = _rope(kr_ref[...].astype(F32), c_ref[...], s1_ref[...], s2_ref[...],
               MLA_ROPE // 4).astype(BF16)
    ones = jnp.ones((kv.shape[0], MLA_V), BF16)
    v0 = MLA_HEADS * MLA_NOPE
    for h in range(MLA_HEADS):
        lo = h * MLA_QK_PAD
        k_ref[:, lo:lo + MLA_NOPE] = kv[:, h * MLA_NOPE:(h + 1) * MLA_NOPE].astype(BF16)
        k_ref[:, lo + MLA_NOPE:lo + MLA_QK_PAD] = kr
        v_ref[:, 2 * h * MLA_V:(2 * h + 1) * MLA_V] = kv[:, v0 + h * MLA_V:v0 + (h + 1) * MLA_V].astype(BF16)
        v_ref[:, (2 * h + 1) * MLA_V:(2 * h + 2) * MLA_V] = ones


def mla_kv(p, kv_norm, w_ukv, tabs, tm, rope_idx):
    t = p.shape[0]
    nk = MLA_HEADS * MLA_QK_PAD
    nv = MLA_HEADS * 2 * MLA_V
    tab_spec = pl.BlockSpec((tm, LANES), lambda i: (rope_idx(i), 0))
    return pl.pallas_call(
        _mla_kv_kernel,
        grid=(t // tm,),
        in_specs=[pl.BlockSpec((tm, MLA_KV_RANK), lambda i: (i, COL_CKV // MLA_KV_RANK)),
                  pl.BlockSpec((tm, LANES), lambda i: (i, COL_KR // LANES)),
                  pl.BlockSpec((1, MLA_KV_RANK), lambda i: (0, 0)),
                  _resident((MLA_KV_RANK, MLA_HEADS * (MLA_NOPE + MLA_V))),
                  tab_spec, tab_spec, tab_spec],
        out_specs=[pl.BlockSpec((tm, nk), lambda i: (i, 0)),
                   pl.BlockSpec((tm, nv), lambda i: (i, 0))],
        out_shape=[jax.ShapeDtypeStruct((t, nk), BF16),
                   jax.ShapeDtypeStruct((t, nv), BF16)],
        compiler_params=_cparams(1),
        name="mla_kv",
    )(p, p, kv_norm, w_ukv, *tabs)


def _gqa_qkv_kernel(q_ref, k_ref, v_ref, qn_ref, kn_ref, c_ref, s1_ref, s2_ref, qo_ref, ko_ref, vo_ref, *, scale):
    c, s1, s2 = c_ref[...], s1_ref[...], s2_ref[...]
    for h in range(GQA_HEADS):
        sl = slice(h * GQA_HEAD_DIM, (h + 1) * GQA_HEAD_DIM)
        x = _rms(q_ref[:, sl].astype(F32)) * qn_ref[...]
        qo_ref[:, sl] = (_rope(x, c, s1, s2, GQA_HEAD_DIM // 4) * scale).astype(BF16)
    ones = jnp.ones((v_ref.shape[0], GQA_HEAD_DIM), BF16)
    for g in range(GQA_KV_HEADS):
        sl = slice(g * GQA_HEAD_DIM, (g + 1) * GQA_HEAD_DIM)
        x = _rms(k_ref[:, sl].astype(F32)) * kn_ref[...]
        ko_ref[:, sl] = _rope(x, c, s1, s2, GQA_HEAD_DIM // 4).astype(BF16)
        vo_ref[:, 2 * g * GQA_HEAD_DIM:(2 * g + 1) * GQA_HEAD_DIM] = v_ref[:, sl]
        vo_ref[:, (2 * g + 1) * GQA_HEAD_DIM:(2 * g + 2) * GQA_HEAD_DIM] = ones


def gqa_qkv(p, q_norm, k_norm, tabs, tm, rope_idx, scale):
    t = p.shape[0]
    nq = GQA_HEADS * GQA_HEAD_DIM
    nk = GQA_KV_HEADS * GQA_HEAD_DIM
    tab_spec = pl.BlockSpec((tm, LANES), lambda i: (rope_idx(i), 0))
    return pl.pallas_call(
        functools.partial(_gqa_qkv_kernel, scale=scale),
        grid=(t // tm,),
        in_specs=[pl.BlockSpec((tm, nq), lambda i: (i, COL_GQ // nq)),
                  pl.BlockSpec((tm, nk), lambda i: (i, COL_GK // nk)),
                  pl.BlockSpec((tm, nk), lambda i: (i, COL_GV // nk)),
                  pl.BlockSpec((1, GQA_HEAD_DIM), lambda i: (0, 0)),
                  pl.BlockSpec((1, GQA_HEAD_DIM), lambda i: (0, 0)),
                  tab_spec, tab_spec, tab_spec],
        out_specs=[pl.BlockSpec((tm, nq), lambda i: (i, 0)),
                   pl.BlockSpec((tm, nk), lambda i: (i, 0)),
                   pl.BlockSpec((tm, 2 * nk), lambda i: (i, 0))],
        out_shape=[jax.ShapeDtypeStruct((t, nq), BF16),
                   jax.ShapeDtypeStruct((t, nk), BF16),
                   jax.ShapeDtypeStruct((t, 2 * nk), BF16)],
        compiler_params=_cparams(1),
        name="gqa_qkv",
    )(p, p, p, q_norm, k_norm, *tabs)


ATTN_SPLIT_ROWS = 256
ATTN_EXP_ROWS = 16


def _attn_kernel(*refs, n_parts):
    q_ref = refs[0]
    k_refs = refs[1:1 + n_parts]
    v_refs = refs[1 + n_parts:1 + 2 * n_parts]
    o_ref, s_scr, p_scr, m_scr = refs[1 + 2 * n_parts:]
    tq = q_ref.shape[0]
    dv = o_ref.shape[1]
    bounds = [0]
    for k in k_refs:
        bounds.append(bounds[-1] + k.shape[0])
    split = min(ATTN_SPLIT_ROWS, tq)
    groups = [slice(r, r + split) for r in range(0, tq, split)]

    for rs in groups:
        q = q_ref[rs, :]
        m = None
        for j, k in enumerate(k_refs):
            s = _dot_nt(q, k[...])
            s_scr[rs, bounds[j]:bounds[j + 1]] = s
            mj = jnp.max(s, axis=-1, keepdims=True)
            m = mj if m is None else jnp.maximum(m, mj)
        m_scr[rs, :] = m
    for r in range(0, tq, ATTN_EXP_ROWS):
        rows = slice(r, r + ATTN_EXP_ROWS)
        p_scr[rows, :] = jnp.exp2(s_scr[rows, :] - m_scr[rows, :]).astype(BF16)
    for rs in groups:
        o2 = functools.reduce(jnp.add, [_dot(p_scr[rs, bounds[j]:bounds[j + 1]], v[...])
                                        for j, v in enumerate(v_refs)])
        o_ref[rs, :] = (o2[:, :dv] / o2[:, dv:]).astype(o_ref.dtype)


def attention(q, k, v, *, bsz, n_q, tq, q_row0, kv_parts, heads, kv_group, dk, dv, name):
    assert tq % min(ATTN_SPLIT_ROWS, tq) == 0 and tq % ATTN_EXP_ROWS == 0
    nq_t = n_q // tq
    n_keys = sum(n_rows for _, n_rows in kv_parts)
    in_specs = [pl.BlockSpec((tq, dk), lambda b, h, i: (q_row0 // tq + b * nq_t + i, h))]
    for row0, n_rows in kv_parts:
        in_specs.append(pl.BlockSpec(
            (n_rows, dk), lambda b, h, i, row0=row0, n_rows=n_rows: (row0 // n_rows + b, h // kv_group)))
    for row0, n_rows in kv_parts:
        in_specs.append(pl.BlockSpec(
            (n_rows, 2 * dv), lambda b, h, i, row0=row0, n_rows=n_rows: (row0 // n_rows + b, h // kv_group)))
    n_parts = len(kv_parts)
    return pl.pallas_call(
        functools.partial(_attn_kernel, n_parts=n_parts),
        grid=(bsz, heads, nq_t),
        in_specs=in_specs,
        out_specs=pl.BlockSpec((tq, dv), lambda b, h, i: (b * nq_t + i, h)),
        out_shape=jax.ShapeDtypeStruct((bsz * n_q, heads * dv), BF16),
        scratch_shapes=[pltpu.VMEM((tq, n_keys), F32), pltpu.VMEM((tq, n_keys), BF16),
                        pltpu.VMEM((tq, 1), F32)],
        compiler_params=_cparams(3),
        name=name,
    )(q, *([k] * n_parts), *([v] * n_parts))


def dft_tables(n, half_shift):
    f = jnp.arange(n, dtype=jnp.int32)[:, None]
    s = jnp.arange(n, dtype=jnp.int32)[None, :]
    if half_shift:
        phase = ((2 * f + 1) * (2 * s + 1)) % (8 * n)
        ang = phase.astype(F32) * (2.0 * math.pi / (8 * n))
    else:
        phase = ((2 * f + 1) * s) % (4 * n)
        ang = phase.astype(F32) * (2.0 * math.pi / (4 * n))
    return jnp.cos(ang).astype(BF16), jnp.sin(ang).astype(BF16)


def hyena_time_filters(n, w1, b1, w2, b2, w3, freq):
    t = jnp.linspace(0.0, 1.0, n, dtype=F32)[:, None]
    ang = 2.0 * math.pi * jnp.arange(n, dtype=F32)[:, None] / n
    bands = jnp.linspace(1e-4, HY_BANDS - 1, HY_BANDS, dtype=F32)
    feats = jnp.concatenate([t, jnp.cos(ang * bands), jnp.sin(ang * bands)], axis=-1)
    fr = freq.astype(F32)
    hp = lax.Precision.HIGHEST
    hdn = jnp.sin(fr * (jnp.dot(feats, w1, precision=hp) + b1))
    hdn = jnp.sin(fr * (jnp.dot(hdn, w2, precision=hp) + b2))
    h = jnp.dot(hdn, w3, precision=hp).reshape(n, HY_ORDER, 2, HY_W)
    deltas = jnp.abs(jnp.linspace(math.log(HY_TARGET) / HY_SLOW, math.log(HY_TARGET) / HY_FAST,
                                  HY_W, dtype=F32))
    h = h * jnp.exp(-t * deltas)[:, None, None, :]
    h_fwd = h[:, :, 0]
    h_bwd = h[:, :, 1].at[0].set(0.0)
    r = lax.rsqrt(jnp.sum(h_fwd * h_fwd, axis=0) + jnp.sum(h_bwd * h_bwd, axis=0) + EPS)
    kf = (h_fwd * r).reshape(n, HY_ORDER * HY_W)
    kb = (h_bwd * r).reshape(n, HY_ORDER * HY_W)
    return kf + kb, kb - kf


HY_CHUNK = 256


def _spectra_kernel(ks_ref, kd_ref, c_ref, s_ref, kr_ref, ki_ref, *, inv_n):
    kr_ref[...] = _dot(c_ref[...], ks_ref[...].astype(BF16)) * inv_n
    ki_ref[...] = _dot(s_ref[...], kd_ref[...].astype(BF16)) * inv_n


def hyena_spectra(ksum, kdiff, cmat, smat):
    n, cols = ksum.shape
    spec = pl.BlockSpec((n, HY_CHUNK), lambda i: (0, i))
    return pl.pallas_call(
        functools.partial(_spectra_kernel, inv_n=1.0 / n),
        grid=(cols // HY_CHUNK,),
        in_specs=[spec, spec, _resident((n, n)), _resident((n, n))],
        out_specs=[spec, spec],
        out_shape=[jax.ShapeDtypeStruct((n, cols), F32)] * 2,
        compiler_params=_cparams(1),
        name="hyena_spectra",
    )(ksum, kdiff, cmat, smat)


def _hyena_kernel(v_ref, x1_ref, x2_ref, wv_ref, w1_ref, w2_ref, skip_ref,
                  k1r_ref, k1i_ref, k2r_ref, k2i_ref, c_ref, s_ref, o_ref, zb_scr, y_scr, *, n, fc):
    lane_groups = [slice(h * LANES, (h + 1) * LANES) for h in range(o_ref.shape[1] // LANES)]

    def short_conv(u_ref, w_ref, cs):
        u = u_ref[:, cs].astype(F32)
        pos = lax.broadcasted_iota(jnp.int32, u.shape, 0)
        prev = jnp.where(pos == 0, 0.0, pltpu.roll(u, 1, 0))
        nxt = jnp.where(pos == n - 1, 0.0, pltpu.roll(u, n - 1, 0))
        return prev * w_ref[0:1, cs] + u * w_ref[1:2, cs] + nxt * w_ref[2:3, cs]

    def long_conv(kr_ref, ki_ref):
        zb = zb_scr[...]
        for j in range(n // fc):
            sl = slice(j * fc, (j + 1) * fc)
            a = _dot(c_ref[sl, :], zb)
            b = _dot(s_ref[sl, :], zb)
            kr = kr_ref[sl, :]
            ki = ki_ref[sl, :]
            yr = (a * kr + b * ki).astype(BF16)
            yi = (a * ki - b * kr).astype(BF16)
            y_scr[...] += _dot(c_ref[:, sl], yr) - _dot(s_ref[:, sl], yi)

    for cs in lane_groups:
        v = short_conv(v_ref, wv_ref, cs)
        zb_scr[:, cs] = v.astype(BF16)
        y_scr[:, cs] = v * skip_ref[0:1, cs]
    long_conv(k1r_ref, k1i_ref)
    for cs in lane_groups:
        z = short_conv(x1_ref, w1_ref, cs) * y_scr[:, cs]
        zb_scr[:, cs] = z.astype(BF16)
        y_scr[:, cs] = z * skip_ref[1:2, cs]
    long_conv(k2r_ref, k2i_ref)
    for cs in lane_groups:
        o_ref[:, cs] = (short_conv(x2_ref, w2_ref, cs) * y_scr[:, cs]).astype(o_ref.dtype)


def hyena_mixer(p, conv_w, skip, kr, ki, cmat, smat, *, bsz, n, row0):
    ch = HY_CHUNK
    nc = HY_W // ch
    fc = min(512, n)
    blk0 = row0 // n

    def u_spec(k):
        return pl.BlockSpec((n, ch), lambda c, b: (blk0 + b, k * nc + c))

    def w_spec(k):
        return pl.BlockSpec((3, ch), lambda c, b: (0, k * nc + c))

    def k_spec(order):
        return pl.BlockSpec((n, ch), lambda c, b: (0, order * nc + c), pipeline_mode=pl.Buffered(1))

    return pl.pallas_call(
        functools.partial(_hyena_kernel, n=n, fc=fc),
        grid=(nc, bsz),
        in_specs=[u_spec(0), u_spec(1), u_spec(2), w_spec(0), w_spec(1), w_spec(2),
                  pl.BlockSpec((HY_ORDER, ch), lambda c, b: (0, c)),
                  k_spec(0), k_spec(0), k_spec(1), k_spec(1),
                  _resident((n, n)), _resident((n, n))],
        out_specs=pl.BlockSpec((n, ch), lambda c, b: (b, c)),
        out_shape=jax.ShapeDtypeStruct((bsz * n, HY_W), BF16),
        scratch_shapes=[pltpu.VMEM((n, ch), BF16), pltpu.VMEM((n, ch), F32)],
        compiler_params=_cparams(2),
        name="hyena_n%d" % n,
    )(p, p, p, conv_w, conv_w, conv_w, skip, kr, ki, kr, ki, cmat, smat)


def _out_proj_kernel(ho_ref, mo_ref, go_ref, wh_ref, wm_ref, wg_ref, x_ref, g1_ref, g2_ref,
                     mod_ref, rwh_ref, rwl_ref, xo_ref, f_ref, lg_ref):
    mix = _dot(ho_ref[...], wh_ref[...]) + _dot(mo_ref[...], wm_ref[...]) + _dot(go_ref[...], wg_ref[...])
    xn = x_ref[...] + mod_ref[0, 2:3, :] * (_rms(mix) * g1_ref[...])
    xo_ref[...] = xn
    f = (_rms(xn) * g2_ref[...]) * (1.0 + mod_ref[0, 4:5, :]) + mod_ref[0, 3:4, :]
    fh = f.astype(BF16)
    fl = (f - fh.astype(F32)).astype(BF16)
    f_ref[...] = fh
    lg_ref[...] = _dot_nt(rwh_ref[...], fh) + _dot_nt(rwh_ref[...], fl) + _dot_nt(rwl_ref[...], fh)


def out_proj(ho, mo, go, wh, wm, wg, x_all, g1, g2, mod, rwh, rwl, tm, mod_idx):
    t = ho.shape[0]
    d = D_MODEL
    row = lambda w: pl.BlockSpec((tm, w), lambda i: (i, 0))
    vec = pl.BlockSpec((1, d), lambda i: (0, 0))
    return pl.pallas_call(
        _out_proj_kernel,
        grid=(t // tm,),
        in_specs=[row(ho.shape[1]), row(mo.shape[1]), row(go.shape[1]),
                  _resident(wh.shape), _resident(wm.shape), _resident(wg.shape),
                  row(d), vec, vec,
                  pl.BlockSpec((1, N_MOD, d), lambda i: (mod_idx(i), 0, 0)),
                  _resident(rwh.shape), _resident(rwl.shape)],
        out_specs=[row(d), row(d), pl.BlockSpec((N_EXPERTS, tm), lambda i: (0, i))],
        out_shape=[jax.ShapeDtypeStruct((t, d), F32),
                   jax.ShapeDtypeStruct((t, d), BF16),
                   jax.ShapeDtypeStruct((N_EXPERTS, t), F32)],
        compiler_params=_cparams(1),
        name="out_proj",
    )(ho, mo, go, wh, wm, wg, x_all, g1, g2, mod, rwh, rwl)


def _moe_kernel(be_ref, bsrc_ref, bflag_ref, bnext_ref, bslot_ref, x_ref, wg_hbm, wu_hbm, wd_hbm, o_ref,
                wg_f, wu_f, wd_f, wg_s, wu_s, wd_s, sem, *, layer):
    i = pl.program_id(0)

    def weight_copies(expert, slot):
        pairs = ((wg_hbm, wg_f), (wu_hbm, wu_f), (wd_hbm, wd_f))
        return [pltpu.make_async_copy(src.at[layer, expert], dst.at[slot], sem.at[k, slot])
                for k, (src, dst) in enumerate(pairs)]

    @pl.when(bflag_ref[i] == 2)
    def _():
        expert, slot, nxt = be_ref[i], bslot_ref[i], bnext_ref[i]

        @pl.when(i == 0)
        def _():
            for cp in weight_copies(expert, slot):
                cp.start()

        for cp in weight_copies(expert, slot):
            cp.wait()

        @pl.when(nxt >= 0)
        def _():
            for cp in weight_copies(nxt, 1 - slot):
                cp.start()

        wg_s[...] = wg_f[slot].astype(BF16)
        wu_s[...] = wu_f[slot].astype(BF16)
        wd_s[...] = wd_f[slot].astype(BF16)

    @pl.when(bflag_ref[i] != 0)
    def _():
        x = x_ref[...]
        a = _silu(_dot(x, wg_s[...])) * _dot(x, wu_s[...])
        o_ref[...] = _dot(a.astype(BF16), wd_s[...]).astype(o_ref.dtype)

    @pl.when(bflag_ref[i] == 0)
    def _():
        o_ref[...] = jnp.zeros(o_ref.shape, o_ref.dtype)


def moe_experts(blk_expert, blk_src, blk_flag, blk_next, blk_slot, xs, wg, wu, wd, layer):
    n_rows, d = xs.shape
    hid = wg.shape[3]
    hbm = pl.BlockSpec(memory_space=pl.ANY)
    grid_spec = pltpu.PrefetchScalarGridSpec(
        num_scalar_prefetch=5,
        grid=(n_rows // MOE_BLOCK,),
        in_specs=[pl.BlockSpec((MOE_BLOCK, d), lambda i, be, bs, bf, bn, bl: (bs[i], 0)), hbm, hbm, hbm],
        out_specs=pl.BlockSpec((MOE_BLOCK, d), lambda i, be, bs, bf, bn, bl: (i, 0)),
        scratch_shapes=[pltpu.VMEM((2, d, hid), F32), pltpu.VMEM((2, d, hid), F32),
                        pltpu.VMEM((2, hid, d), F32),
                        pltpu.VMEM((d, hid), BF16), pltpu.VMEM((d, hid), BF16),
                        pltpu.VMEM((hid, d), BF16),
                        pltpu.SemaphoreType.DMA((3, 2))],
    )
    return pl.pallas_call(
        functools.partial(_moe_kernel, layer=layer),
        grid_spec=grid_spec,
        out_shape=jax.ShapeDtypeStruct((n_rows, d), BF16),
        compiler_params=_cparams(1),
        cost_estimate=pl.CostEstimate(flops=6 * n_rows * d * hid, transcendentals=n_rows * hid,
                                      bytes_accessed=4 * n_rows * d + 12 * N_EXPERTS * d * hid),
        name="moe_experts",
    )(blk_expert, blk_src, blk_flag, blk_next, blk_slot, xs, wg, wu, wd)


def _shared_expert_kernel(f_ref, sg_ref, su_ref, sd_ref, o_ref):
    f = f_ref[...]
    a = _silu(_dot(f, sg_ref[...])) * _dot(f, su_ref[...])
    o_ref[...] = _dot(a.astype(BF16), sd_ref[...]).astype(o_ref.dtype)


def shared_expert(f, sg, su, sd, tm):
    t, d = f.shape
    row = pl.BlockSpec((tm, d), lambda i: (i, 0))
    return pl.pallas_call(
        _shared_expert_kernel,
        grid=(t // tm,),
        in_specs=[row, _resident(sg.shape), _resident(su.shape), _resident(sd.shape)],
        out_specs=row,
        out_shape=jax.ShapeDtypeStruct((t, d), BF16),
        compiler_params=_cparams(1),
        cost_estimate=pl.CostEstimate(flops=6 * t * d * sg.shape[1], transcendentals=t * sg.shape[1],
                                      bytes_accessed=4 * t * d + 6 * d * sg.shape[1]),
        name="shared_expert",
    )(f, sg, su, sd)


def _ffn_combine_kernel(ys_ref, *refs, seg_tiles):
    n_seg = len(seg_tiles)
    yg_refs, gate_refs = refs[:n_seg], refs[n_seg:2 * n_seg]
    x_ref, g3_ref, mod_ref, o_ref = refs[2 * n_seg:]
    i = pl.program_id(0)
    start = 0
    for s in range(n_seg):
        @pl.when((i >= start) & (i < start + seg_tiles[s]))
        def _(s=s):
            y = ys_ref[...].astype(F32)
            gates = gate_refs[s][...]
            for k in range(TOP_K):
                y = y + gates[:, k:k + 1] * yg_refs[s][k].astype(F32)
            o_ref[...] = x_ref[...] + mod_ref[0, 5:6, :] * (_rms(y) * g3_ref[...])
        start += seg_tiles[s]


def ffn_combine(y_shared, routed, x_all, g3, mod, tm, mod_idx):
    t, d = y_shared.shape
    seg_tiles = [yg.shape[1] // tm for yg, _ in routed]
    assert sum(seg_tiles) == t // tm
    row = pl.BlockSpec((tm, d), lambda i: (i, 0))
    starts = [sum(seg_tiles[:s]) for s in range(len(routed))]

    def seg_idx(s):
        return lambda i: jnp.clip(i - starts[s], 0, seg_tiles[s] - 1)

    yg_specs = [pl.BlockSpec((TOP_K, tm, d), lambda i, s=s: (0, seg_idx(s)(i), 0)) for s in range(len(routed))]
    gate_specs = [pl.BlockSpec((tm, routed[s][1].shape[1]), lambda i, s=s: (seg_idx(s)(i), 0))
                  for s in range(len(routed))]
    return pl.pallas_call(
        functools.partial(_ffn_combine_kernel, seg_tiles=tuple(seg_tiles)),
        grid=(t // tm,),
        in_specs=[row] + yg_specs + gate_specs + [row, pl.BlockSpec((1, d), lambda i: (0, 0)),
                                                  pl.BlockSpec((1, N_MOD, d), lambda i: (mod_idx(i), 0, 0))],
        out_specs=row,
        out_shape=jax.ShapeDtypeStruct((t, d), F32),
        compiler_params=_cparams(1),
        name="ffn_combine",
    )(y_shared, *[yg for yg, _ in routed], *[g for _, g in routed], x_all, g3, mod)


def _first_max(vals, idx, n):
    m = jnp.max(vals, axis=0, keepdims=True)
    return m, jnp.min(jnp.where(vals == m, idx, float(n)), axis=0, keepdims=True)


def _router_kernel(lg_ref, bias_ref, e_ref, g_ref, r_ref, cnt_ref, carry_ref):
    @pl.when(pl.program_id(0) == 0)
    def _():
        carry_ref[...] = jnp.zeros(carry_ref.shape, F32)

    lg = lg_ref[...]
    tt = lg.shape[1]
    gsz = N_EXPERTS // N_GROUPS
    neg = -jnp.inf
    scores = 1.0 / (1.0 + jnp.exp(-lg))
    biased = scores + bias_ref[...]
    sub = lax.broadcasted_iota(jnp.int32, (gsz, tt), 0).astype(F32)

    grp_rows = []
    for g in range(N_GROUPS):
        blk = biased[g * gsz:(g + 1) * gsz, :]
        m1, i1 = _first_max(blk, sub, gsz)
        m2 = jnp.max(jnp.where(sub == i1, neg, blk), axis=0, keepdims=True)
        grp_rows.append(m1 + m2)
    cur = jnp.concatenate(grp_rows, axis=0)
    gidx = lax.broadcasted_iota(jnp.int32, (N_GROUPS, tt), 0).astype(F32)
    gsel = jnp.zeros((N_GROUPS, tt), F32)
    for _ in range(TOPK_GROUPS):
        _, gi = _first_max(cur, gidx, N_GROUPS)
        hit = gidx == gi
        gsel = jnp.where(hit, 1.0, gsel)
        cur = jnp.where(hit, neg, cur)
    emask = jnp.concatenate([jnp.broadcast_to(gsel[g:g + 1, :], (gsz, tt)) for g in range(N_GROUPS)], axis=0)

    cand = jnp.where(emask > 0.0, biased, neg)
    eidx = lax.broadcasted_iota(jnp.int32, (N_EXPERTS, tt), 0).astype(F32)
    chosen = jnp.zeros((N_EXPERTS, tt), F32)
    e_rows, g_rows = [], []
    for _ in range(TOP_K):
        _, ei = _first_max(cand, eidx, N_EXPERTS)
        hit = eidx == ei
        e_rows.append(ei)
        g_rows.append(jnp.sum(jnp.where(hit, scores, 0.0), axis=0, keepdims=True))
        chosen = jnp.where(hit, 1.0, chosen)
        cand = jnp.where(hit, neg, cand)
    gsum = functools.reduce(jnp.add, g_rows)
    g_rows = [g / gsum * ROUTED_SCALE for g in g_rows]

    before = (lax.broadcasted_iota(jnp.int32, (tt, tt), 0) < lax.broadcasted_iota(jnp.int32, (tt, tt), 1))
    prefix = _dot(chosen.astype(BF16), jnp.where(before, 1.0, 0.0).astype(BF16))
    rank_all = prefix + carry_ref[...]
    r_rows = [jnp.sum(jnp.where(eidx == ei, rank_all, 0.0), axis=0, keepdims=True) for ei in e_rows]
    carry_ref[...] = carry_ref[...] + jnp.sum(chosen, axis=1, keepdims=True)
    cnt_ref[...] = carry_ref[...]

    pad = [jnp.zeros((8 - TOP_K, tt), F32)]
    e_ref[...] = jnp.concatenate(e_rows + pad, axis=0).astype(jnp.int32)
    g_ref[...] = jnp.concatenate(g_rows + pad, axis=0)
    r_ref[...] = jnp.concatenate(r_rows + pad, axis=0).astype(jnp.int32)


def router(logits_t, bias, tt):
    n_exp, t = logits_t.shape
    col = pl.BlockSpec((8, tt), lambda i: (0, i))
    return pl.pallas_call(
        _router_kernel,
        grid=(t // tt,),
        in_specs=[pl.BlockSpec((n_exp, tt), lambda i: (0, i)),
                  pl.BlockSpec((n_exp, 1), lambda i: (0, 0))],
        out_specs=[col, col, col, pl.BlockSpec((n_exp, 1), lambda i: (0, 0))],
        out_shape=[jax.ShapeDtypeStruct((8, t), jnp.int32), jax.ShapeDtypeStruct((8, t), F32),
                   jax.ShapeDtypeStruct((8, t), jnp.int32), jax.ShapeDtypeStruct((n_exp, 1), F32)],
        scratch_shapes=[pltpu.VMEM((n_exp, 1), F32)],
        compiler_params=_cparams(1),
        name="router",
    )(logits_t, bias.reshape(n_exp, 1).astype(F32))


def rope_tables(n, rot_dim, tm):
    rows = n // GRID_W
    row = jnp.repeat(jnp.arange(rows, dtype=F32), GRID_W)
    col = jnp.tile(jnp.arange(GRID_W, dtype=F32), rows)
    axis_dim = rot_dim // 2
    inv = ROPE_THETA ** (-jnp.arange(0, axis_dim, 2, dtype=F32) / axis_dim)
    ar, ac = row[:, None] * inv, col[:, None] * inv
    zero = jnp.zeros_like(ar)
    c = jnp.concatenate([jnp.cos(ar), jnp.cos(ar), jnp.cos(ac), jnp.cos(ac)], axis=-1)
    s1 = jnp.concatenate([-jnp.sin(ar), zero, -jnp.sin(ac), zero], axis=-1)
    s2 = jnp.concatenate([zero, jnp.sin(ar), zero, jnp.sin(ac)], axis=-1)

    def finish(tab, fill):
        tab = jnp.pad(tab, ((0, 0), (0, LANES - rot_dim)), constant_values=fill)
        return jnp.concatenate([tab, jnp.full((tm, LANES), fill, F32)], axis=0)

    return finish(c, 1.0), finish(s1, 0.0), finish(s2, 0.0)


def pack_w_in(w):
    hy = 3 * HY_W
    kv0 = hy + MLA_Q_RANK + GQA_HEADS * GQA_HEAD_DIM
    ckv = w[:, kv0:kv0 + MLA_KV_RANK]
    kr = w[:, kv0 + MLA_KV_RANK:kv0 + MLA_KV_RANK + MLA_ROPE]
    gk0 = kv0 + MLA_KV_RANK + MLA_ROPE
    gkv = w[:, gk0:]
    kr = jnp.pad(kr, ((0, 0), (0, LANES - MLA_ROPE)))
    return jnp.concatenate([w[:, :kv0], ckv, gkv, kr], axis=1).astype(BF16)


def pack_w_uq(w):
    w = w.reshape(MLA_Q_RANK, MLA_HEADS, MLA_NOPE + MLA_ROPE)
    w = jnp.pad(w, ((0, 0), (0, 0), (0, MLA_QK_PAD - MLA_NOPE - MLA_ROPE)))
    return w.reshape(MLA_Q_RANK, MLA_HEADS * MLA_QK_PAD).astype(BF16)


def pack_w_ukv(w):
    w = w.reshape(MLA_KV_RANK, MLA_HEADS, MLA_NOPE + MLA_V)
    k = w[:, :, :MLA_NOPE].reshape(MLA_KV_RANK, MLA_HEADS * MLA_NOPE)
    v = w[:, :, MLA_NOPE:].reshape(MLA_KV_RANK, MLA_HEADS * MLA_V)
    return jnp.concatenate([k, v], axis=1).astype(BF16)


def dispatch_plan(e_t, rank_t, counts):
    n_tok = e_t.shape[1]
    n_pairs = n_tok * TOP_K
    counts = counts.astype(jnp.int32)
    padded = (counts + MOE_BLOCK - 1) // MOE_BLOCK * MOE_BLOCK
    p_ends = jnp.cumsum(padded)
    p_starts = p_ends - padded
    n_blocks = -(-n_pairs // MOE_BLOCK) + N_EXPERTS
    n_rows = n_blocks * MOE_BLOCK
    blk = jnp.arange(n_blocks, dtype=jnp.int32)
    n_used = p_ends[-1] // MOE_BLOCK
    blk_src = jnp.minimum(blk, n_used - 1)
    blk_expert = jnp.sum((blk_src * MOE_BLOCK)[:, None] >= p_ends[None, :], axis=1).astype(jnp.int32)
    first = jnp.concatenate([jnp.ones((1,), bool), blk_expert[1:] != blk_expert[:-1]])
    blk_flag = jnp.where(blk < n_used, jnp.where(first, 2, 1), 0).astype(jnp.int32)

    experts = jnp.arange(N_EXPERTS, dtype=jnp.int32)
    used = counts > 0
    later = jnp.where(used[None, :] & (experts[None, :] > experts[:, None]), experts[None, :], N_EXPERTS)
    next_used = jnp.min(later, axis=1)
    next_used = jnp.where(next_used == N_EXPERTS, -1, next_used).astype(jnp.int32)
    slot_of = ((jnp.cumsum(used.astype(jnp.int32)) - 1) % 2).astype(jnp.int32)
    blk_next = next_used[blk_expert]
    blk_slot = slot_of[blk_expert]

    onehot = e_t[:, :, None] == experts
    pos = jnp.sum(jnp.where(onehot, p_starts, 0), axis=-1) + rank_t

    stride = n_tok + 1
    tok = jnp.arange(n_tok, dtype=jnp.int32)
    real_keys = (e_t * stride + tok[None, :]).reshape(-1)
    cum_fill = jnp.cumsum(padded - counts)
    filler = jnp.arange(n_rows - n_pairs, dtype=jnp.int32)
    filler_exp = jnp.sum(filler[:, None] >= cum_fill[None, :], axis=1).astype(jnp.int32)
    filler_keys = filler_exp * stride + n_tok
    slot_tok = jnp.sort(jnp.concatenate([real_keys, filler_keys])) % stride
    spread = jnp.arange(n_rows, dtype=jnp.int32) % n_tok
    slot_tok = jnp.where(slot_tok == n_tok, spread, slot_tok)
    return (blk_expert, blk_src, blk_flag, blk_next, blk_slot), slot_tok, pos


def _pick_tile(*sizes):
    for tile in (512, 256, 128):
        if all(s % tile == 0 for s in sizes):
            return tile
    raise ValueError("row counts must be multiples of 128: %r" % (sizes,))


def kernel(x, c, ctx, c_ctx, ada_w, ada_b, norm_g, w_in, w_out, hy_conv, hy_w1, hy_b1, hy_w2, hy_b2,
           hy_w3, hy_freq, hy_skip, mla_q_norm, mla_kv_norm, mla_w_uq, mla_w_ukv, gqa_q_norm, gqa_k_norm,
           router_w, router_bias, exp_w_gate, exp_w_up, exp_w_down, sh_w_gate, sh_w_up, sh_w_down):
    bsz, n, d = x.shape
    n_ctx = ctx.shape[1]
    depth = ada_w.shape[0]
    assert d == D_MODEL and n % GRID_W == 0
    t_lat, t_ctx = bsz * n, bsz * n_ctx
    assert t_lat % n_ctx == 0
    tm = _pick_tile(n, t_ctx)
    tq = max(t for t in (1024, 512, 256) if n % t == 0)
    lat_tiles = t_lat // tm
    tiles_per_seq = n // tm
    tm_ffn = min(tm, 256)

    def mod_idx(tile):
        return lambda i: jnp.minimum(i // (n // tile), bsz)

    def rope_idx(i):
        return jnp.where(i < lat_tiles, i % tiles_per_seq, tiles_per_seq)

    x_all = jnp.concatenate([x.reshape(t_lat, d), ctx.reshape(t_ctx, d)], axis=0)
    mod_rows = -(-(bsz + 1) // 16) * 16
    s_in = jnp.concatenate([c, c_ctx[None], jnp.zeros((mod_rows - bsz - 1, d), F32)], axis=0)
    mod_all = adaln_all(s_in, ada_w, ada_b).reshape(depth, mod_rows, N_MOD, d)

    tabs_mla = rope_tables(n, MLA_ROPE, tm)
    tabs_gqa = rope_tables(n, GQA_HEAD_DIM, tm)
    mla_scale = (MLA_NOPE + MLA_ROPE) ** -0.5 * math.log2(math.e)
    gqa_scale = GQA_HEAD_DIM ** -0.5 * math.log2(math.e)
    dft ={m: (dft_tables(m, False), dft_tables(m, True)) for m in (n, n_ctx)}

    for l in range(depth):
        last = l == depth - 1
        mod = mod_all[l]
        vec = lambda a: a.reshape(1, -1)

        p = in_proj(x_all, vec(norm_g[l, 0]), mod, pack_w_in(w_in[l]), tm, mod_idx(tm))
        q_m = mla_q(p, vec(mla_q_norm[l]), pack_w_uq(mla_w_uq[l]), tabs_mla, tm, rope_idx, mla_scale)
        k_m, v_m = mla_kv(p, vec(mla_kv_norm[l]), pack_w_ukv(mla_w_ukv[l]), tabs_mla, tm, rope_idx)
        q_g, k_g, v_g = gqa_qkv(p, vec(gqa_q_norm[l]), vec(gqa_k_norm[l]), tabs_gqa, tm, rope_idx, gqa_scale)

        mla_args = dict(bsz=bsz, heads=MLA_HEADS, kv_group=1, dk=MLA_QK_PAD, dv=MLA_V)
        gqa_args = dict(bsz=bsz, heads=GQA_HEADS, kv_group=GQA_HEADS // GQA_KV_HEADS, dk=GQA_HEAD_DIM,
                        dv=GQA_HEAD_DIM)
        lat_q = dict(n_q=n, tq=tq, q_row0=0, kv_parts=[(t_lat, n_ctx), (0, n)])
        mo = attention(q_m, k_m, v_m, name="mla_attn", **lat_q, **mla_args)
        go = attention(q_g, k_g, v_g, name="gqa_attn", **lat_q, **gqa_args)

        filt = (hy_w1[l], hy_b1[l], hy_w2[l], hy_b2[l], hy_w3[l], hy_freq[l])

        def hyena(m, row0):
            lag_tabs, sym_tabs = dft[m]
            kr, ki = hyena_spectra(*hyena_time_filters(m, *filt), *lag_tabs)
            return hyena_mixer(p, hy_conv[l], hy_skip[l], kr, ki, *sym_tabs, bsz=bsz, n=m, row0=row0)

        ho = hyena(n, 0)

        if not last:
            ctx_q = dict(n_q=n_ctx, tq=n_ctx, q_row0=t_lat, kv_parts=[(t_lat, n_ctx)])
            mo = jnp.concatenate([mo, attention(q_m, k_m, v_m, name="mla_attn_ctx", **ctx_q, **mla_args)])
            go = jnp.concatenate([go, attention(q_g, k_g, v_g, name="gqa_attn_ctx", **ctx_q, **gqa_args)])
            ho = jnp.concatenate([ho, hyena(n_ctx, t_lat)])

        wo = w_out[l].astype(BF16)
        rw_t = router_w[l].T
        rw_hi = rw_t.astype(BF16)
        rw_lo = (rw_t - rw_hi.astype(F32)).astype(BF16)
        x_mid, f, logits_t = out_proj(ho, mo, go, wo[:HY_W], wo[HY_W:HY_W + MLA_HEADS * MLA_V],
                                      wo[HY_W + MLA_HEADS * MLA_V:], x_all, vec(norm_g[l, 1]),
                                      vec(norm_g[l, 2]), mod, rw_hi, rw_lo, tm, mod_idx(tm))

        n_tiles = f.shape[0] // tm
        seg_bounds = [0, (n_tiles // 2) * tm, n_tiles * tm]
        y_shared = shared_expert(f, sh_w_gate[l].astype(BF16), sh_w_up[l].astype(BF16),
                                 sh_w_down[l].astype(BF16), tm)
        routed = []
        for a, b in zip(seg_bounds[:-1], seg_bounds[1:]):
            e_t, gate_t, rank_t, counts = router(logits_t[:, a:b], router_bias[l], tm)
            blocks, slot_tok, pos = dispatch_plan(e_t[:TOP_K], rank_t[:TOP_K], counts[:, 0])
            ys = moe_experts(*blocks, f[slot_tok + a], exp_w_gate, exp_w_up, exp_w_down, l)
            routed.append((ys[pos], gate_t.T))
        x_all = ffn_combine(y_shared, routed, x_mid, vec(norm_g[l, 3]), mod, tm_ffn, mod_idx(tm_ffn))

    return x_all.reshape(bsz, n, d)
```

```python
import functools
import math

import jax
import jax.numpy as jnp
from jax import lax
from jax.experimental import pallas as pl
from jax.experimental.pallas import tpu as pltpu

F32 = jnp.float32
BF16 = jnp.bfloat16

D_MODEL = 2048
GRID_W = 64
EPS = 1e-6
N_MOD = 6
HY_W = D_MODEL // 4
HY_ORDER = 2
HY_BANDS = 16
HY_TARGET = 1e-2
HY_FAST = 0.3
HY_SLOW = 1.5
MLA_NOPE = 128
MLA_ROPE = 64
MLA_V = 128
MLA_HEADS = 6
MLA_Q_RANK = 768
MLA_KV_RANK = 256
GQA_HEAD_DIM = 128
GQA_HEADS = 6
GQA_KV_HEADS = 2
ROPE_THETA = 10000.0
N_EXPERTS = 64
TOP_K = 6
N_GROUPS = 8
TOPK_GROUPS = 4
EXPERT_HIDDEN = D_MODEL // 4
ROUTED_SCALE = 2.5

LANES = 128
VMEM_LIMIT_BYTES = 56 * 1024 * 1024

COL_HY = 0
COL_MQ = 3 * HY_W
COL_GQ = COL_MQ + MLA_Q_RANK
COL_CKV = COL_GQ + GQA_HEADS * GQA_HEAD_DIM
COL_GK = COL_CKV + MLA_KV_RANK
COL_GV = COL_GK + GQA_KV_HEADS * GQA_HEAD_DIM
COL_KR = COL_GV + GQA_KV_HEADS * GQA_HEAD_DIM
IN_COLS_PAD = COL_KR + LANES
MLA_QK_PAD = 2 * LANES
MOE_BLOCK = 256

NT_DIMS = (((1,), (1,)), ((), ()))


def _cparams(n_axes):
    return pltpu.CompilerParams(dimension_semantics=("arbitrary",) * n_axes,
                                vmem_limit_bytes=VMEM_LIMIT_BYTES)


def _resident(shape):
    nd = len(shape)
    return pl.BlockSpec(shape, lambda *_: (0,) * nd, pipeline_mode=pl.Buffered(1))


def _rms(x):
    return x * lax.rsqrt(jnp.mean(x * x, axis=-1, keepdims=True) + EPS)


def _silu(x):
    return x / (1.0 + jnp.exp(-x))


def _dot(a, b):
    return jnp.dot(a, b, preferred_element_type=F32)


def _dot_nt(a, b):
    return lax.dot_general(a, b, NT_DIMS, preferred_element_type=F32)


def _rope(x, c, s1, s2, shift):
    w = x.shape[-1]
    return x * c + pltpu.roll(x, w - shift, 1) * s1 + pltpu.roll(x, shift, 1) * s2


def _adaln_kernel(s_ref, w_ref, b_ref, o_ref):
    s = _silu(s_ref[...]).astype(BF16)
    o_ref[0] = _dot(s, w_ref[0].astype(BF16)) + b_ref[0]


def adaln_all(s_in, ada_w, ada_b):
    depth, d, n_out = ada_w.shape
    rows = s_in.shape[0]
    tn = 1024
    return pl.pallas_call(
        _adaln_kernel,
        grid=(depth, n_out // tn),
        in_specs=[pl.BlockSpec((rows, d), lambda l, j: (0, 0)),
                  pl.BlockSpec((1, d, tn), lambda l, j: (l, 0, j)),
                  pl.BlockSpec((1, 1, tn), lambda l, j: (l, 0, j))],
        out_specs=pl.BlockSpec((1, rows, tn), lambda l, j: (l, 0, j)),
        out_shape=jax.ShapeDtypeStruct((depth, rows, n_out), F32),
        compiler_params=_cparams(2),
        name="adaln",
    )(s_in, ada_w, ada_b.reshape(depth, 1, n_out))


def _part_tiles(parts, tm):
    return tuple(p.shape[0] // tm for p in parts)


def _part_specs(parts, tm):
    specs, start = [], 0
    for part in parts:
        tiles = part.shape[0] // tm
        specs.append(pl.BlockSpec((tm, part.shape[1]),
                                  lambda i, start=start, tiles=tiles: (jnp.clip(i - start, 0, tiles - 1), 0)))
        start += tiles
    return specs


def _read_part(refs, tiles):
    val, start = refs[0][...], tiles[0]
    for ref, n_tiles in zip(refs[1:], tiles[1:]):
        val = jnp.where(pl.program_id(0) >= start, ref[...], val)
        start += n_tiles
    return val


def _in_proj_kernel(*refs, x_tiles):
    x_refs = refs[:len(x_tiles)]
    g_ref, mod_ref, w_ref, o_ref = refs[len(x_tiles):]
    h = _rms(_read_part(x_refs, x_tiles)) * g_ref[...]
    h = h * (1.0 + mod_ref[0, 1:2, :]) + mod_ref[0, 0:1, :]
    o_ref[...] = _dot(h.astype(BF16), w_ref[...]).astype(o_ref.dtype)


def in_proj(x_parts, g, mod, w, tm, mod_idx):
    d, n_out = w.shape
    x_tiles = _part_tiles(x_parts, tm)
    t = sum(x_tiles) * tm
    return pl.pallas_call(
        functools.partial(_in_proj_kernel, x_tiles=x_tiles),
        grid=(t // tm,),
        in_specs=_part_specs(x_parts, tm) + [pl.BlockSpec((1, d), lambda i: (0, 0)),
                                             pl.BlockSpec((1, N_MOD, d), lambda i: (mod_idx(i), 0, 0)),
                                             _resident((d, n_out))],
        out_specs=pl.BlockSpec((tm, n_out), lambda i: (i, 0)),
        out_shape=jax.ShapeDtypeStruct((t, n_out), BF16),
        compiler_params=_cparams(1),
        name="in_proj",
    )(*x_parts, g, mod, w)


def _mla_q_kernel(cq_ref, qn_ref, w_ref, c_ref, s1_ref, s2_ref, o_ref, *, scale):
    hn = (_rms(cq_ref[...].astype(F32)) * qn_ref[...]).astype(BF16)
    q = _dot(hn, w_ref[...]) * scale
    c, s1, s2 = c_ref[...], s1_ref[...], s2_ref[...]
    for h in range(MLA_HEADS):
        lo = h * MLA_QK_PAD
        o_ref[:, lo:lo + MLA_NOPE] = q[:, lo:lo + MLA_NOPE].astype(BF16)
        r = q[:, lo + MLA_NOPE:lo + MLA_QK_PAD]
        o_ref[:, lo + MLA_NOPE:lo + MLA_QK_PAD] = _rope(r, c, s1, s2, MLA_ROPE // 4).astype(BF16)


def mla_q(p, q_norm, w_uq, tabs, tm, rope_idx, scale):
    t = p.shape[0]
    n_out = MLA_HEADS * MLA_QK_PAD
    tab_spec = pl.BlockSpec((tm, LANES), lambda i: (rope_idx(i), 0))
    return pl.pallas_call(
        functools.partial(_mla_q_kernel, scale=scale),
        grid=(t // tm,),
        in_specs=[pl.BlockSpec((tm, MLA_Q_RANK), lambda i: (i, COL_MQ // MLA_Q_RANK)),
                  pl.BlockSpec((1, MLA_Q_RANK), lambda i: (0, 0)),
                  _resident((MLA_Q_RANK, n_out)),
                  tab_spec, tab_spec, tab_spec],
        out_specs=pl.BlockSpec((tm, n_out), lambda i: (i, 0)),
        out_shape=jax.ShapeDtypeStruct((t, n_out), BF16),
        compiler_params=_cparams(1),
        name="mla_q",
    )(p, q_norm, w_uq, *tabs)


def _mla_kv_kernel(ckv_ref, kr_ref, kvn_ref, w_ref, c_ref, s1_ref, s2_ref, k_ref, v_ref):
    hn = (_rms(ckv_ref[...].astype(F32)) * kvn_ref[...]).astype(BF16)
    kv = _dot(hn, w_ref[...])
    kr = _rope(kr_ref[...].astype(F32), c_ref[...], s1_ref[...], s2_ref[...],
               MLA_ROPE // 4).astype(BF16)
    ones = jnp.ones((kv.shape[0], MLA_V), BF16)
    v0 = MLA_HEADS * MLA_NOPE
    for h in range(MLA_HEADS):
        lo = h * MLA_QK_PAD
        k_ref[:, lo:lo + MLA_NOPE] = kv[:, h * MLA_NOPE:(h + 1) * MLA_NOPE].astype(BF16)
        k_ref[:, lo + MLA_NOPE:lo + MLA_QK_PAD] = kr
        v_ref[:, 2 * h * MLA_V:(2 * h + 1) * MLA_V] = kv[:, v0 + h * MLA_V:v0 + (h + 1) * MLA_V].astype(BF16)
        v_ref[:, (2 * h + 1) * MLA_V:(2 * h + 2) * MLA_V] = ones


def mla_kv(p, kv_norm, w_ukv, tabs, tm, rope_idx):
    t = p.shape[0]
    nk = MLA_HEADS * MLA_QK_PAD
    nv = MLA_HEADS * 2 * MLA_V
    tab_spec = pl.BlockSpec((tm, LANES), lambda i: (rope_idx(i), 0))
    return pl.pallas_call(
        _mla_kv_kernel,
        grid=(t // tm,),
        in_specs=[pl.BlockSpec((tm, MLA_KV_RANK), lambda i: (i, COL_CKV // MLA_KV_RANK)),
                  pl.BlockSpec((tm, LANES), lambda i: (i, COL_KR // LANES)),
                  pl.BlockSpec((1, MLA_KV_RANK), lambda i: (0, 0)),
                  _resident((MLA_KV_RANK, MLA_HEADS * (MLA_NOPE + MLA_V))),
                  tab_spec, tab_spec, tab_spec],
        out_specs=[pl.BlockSpec((tm, nk), lambda i: (i, 0)),
                   pl.BlockSpec((tm, nv), lambda i: (i, 0))],
        out_shape=[jax.ShapeDtypeStruct((t, nk), BF16),
                   jax.ShapeDtypeStruct((t, nv), BF16)],
        compiler_params=_cparams(1),
        name="mla_kv",
    )(p, p, kv_norm, w_ukv, *tabs)


def _gqa_qkv_kernel(q_ref, k_ref, v_ref, qn_ref, kn_ref, c_ref, s1_ref, s2_ref, qo_ref, ko_ref, vo_ref, *, scale):
    c, s1, s2 = c_ref[...], s1_ref[...], s2_ref[...]
    for h in range(GQA_HEADS):
        sl = slice(h * GQA_HEAD_DIM, (h + 1) * GQA_HEAD_DIM)
        x = _rms(q_ref[:, sl].astype(F32)) * qn_ref[...]
        qo_ref[:, sl] = (_rope(x, c, s1, s2, GQA_HEAD_DIM // 4) * scale).astype(BF16)
    ones = jnp.ones((v_ref.shape[0], GQA_HEAD_DIM), BF16)
    for g in range(GQA_KV_HEADS):
        sl = slice(g * GQA_HEAD_DIM, (g + 1) * GQA_HEAD_DIM)
        x = _rms(k_ref[:, sl].astype(F32)) * kn_ref[...]
        ko_ref[:, sl] = _rope(x, c, s1, s2, GQA_HEAD_DIM // 4).astype(BF16)
        vo_ref[:, 2 * g * GQA_HEAD_DIM:(2 * g + 1) * GQA_HEAD_DIM] = v_ref[:, sl]
        vo_ref[:, (2 * g + 1) * GQA_HEAD_DIM:(2 * g + 2) * GQA_HEAD_DIM] = ones


def gqa_qkv(p, q_norm, k_norm, tabs, tm, rope_idx, scale):
    t = p.shape[0]
    nq = GQA_HEADS * GQA_HEAD_DIM
    nk = GQA_KV_HEADS * GQA_HEAD_DIM
    tab_spec = pl.BlockSpec((tm, LANES), lambda i: (rope_idx(i), 0))
    return pl.pallas_call(
        functools.partial(_gqa_qkv_kernel, scale=scale),
        grid=(t // tm,),
        in_specs=[pl.BlockSpec((tm, nq), lambda i: (i, COL_GQ // nq)),
                  pl.BlockSpec((tm, nk), lambda i: (i, COL_GK // nk)),
                  pl.BlockSpec((tm, nk), lambda i: (i, COL_GV // nk)),
                  pl.BlockSpec((1, GQA_HEAD_DIM), lambda i: (0, 0)),
                  pl.BlockSpec((1, GQA_HEAD_DIM), lambda i: (0, 0)),
                  tab_spec, tab_spec, tab_spec],
        out_specs=[pl.BlockSpec((tm, nq), lambda i: (i, 0)),
                   pl.BlockSpec((tm, nk), lambda i: (i, 0)),
                   pl.BlockSpec((tm, 2 * nk), lambda i: (i, 0))],
        out_shape=[jax.ShapeDtypeStruct((t, nq), BF16),
                   jax.ShapeDtypeStruct((t, nk), BF16),
                   jax.ShapeDtypeStruct((t, 2 * nk), BF16)],
        compiler_params=_cparams(1),
        name="gqa_qkv",
    )(p, p, p, q_norm, k_norm, *tabs)


ATTN_SPLIT_ROWS = 256
ATTN_EXP_ROWS = 16


def _attn_kernel(*refs, n_parts):
    q_ref = refs[0]
    k_refs = refs[1:1 + n_parts]
    v_refs = refs[1 + n_parts:1 + 2 * n_parts]
    o_ref, s_scr, p_scr, m_scr = refs[1 + 2 * n_parts:]
    tq = q_ref.shape[0]
    dv = o_ref.shape[1]
    bounds = [0]
    for k in k_refs:
        bounds.append(bounds[-1] + k.shape[0])
    split = min(ATTN_SPLIT_ROWS, tq)
    groups = [slice(r, r + split) for r in range(0, tq, split)]

    for rs in groups:
        q = q_ref[rs, :]
        m = None
        for j, k in enumerate(k_refs):
            s = _dot_nt(q, k[...])
            s_scr[rs, bounds[j]:bounds[j + 1]] = s
            mj = jnp.max(s, axis=-1, keepdims=True)
            m = mj if m is None else jnp.maximum(m, mj)
        m_scr[rs, :] = m
    for r in range(0, tq, ATTN_EXP_ROWS):
        rows = slice(r, r + ATTN_EXP_ROWS)
        p_scr[rows, :] = jnp.exp2(s_scr[rows, :] - m_scr[rows, :]).astype(BF16)
    for rs in groups:
        o2 = functools.reduce(jnp.add, [_dot(p_scr[rs, bounds[j]:bounds[j + 1]], v[...])
                                        for j, v in enumerate(v_refs)])
        o_ref[rs, :] = (o2[:, :dv] / o2[:, dv:]).astype(o_ref.dtype)


def attention(q, k, v, *, bsz, n_q, tq, q_row0, kv_parts, heads, kv_group, dk, dv, name):
    assert tq % min(ATTN_SPLIT_ROWS, tq) == 0 and tq % ATTN_EXP_ROWS == 0
    nq_t = n_q // tq
    n_keys = sum(n_rows for _, n_rows in kv_parts)
    in_specs = [pl.BlockSpec((tq, dk), lambda b, h, i: (q_row0 // tq + b * nq_t + i, h))]
    for row0, n_rows in kv_parts:
        in_specs.append(pl.BlockSpec(
            (n_rows, dk), lambda b, h, i, row0=row0, n_rows=n_rows: (row0 // n_rows + b, h // kv_group)))
    for row0, n_rows in kv_parts:
        in_specs.append(pl.BlockSpec(
            (n_rows, 2 * dv), lambda b, h, i, row0=row0, n_rows=n_rows: (row0 // n_rows + b, h // kv_group)))
    n_parts = len(kv_parts)
    return pl.pallas_call(
        functools.partial(_attn_kernel, n_parts=n_parts),
        grid=(bsz, heads, nq_t),
        in_specs=in_specs,
        out_specs=pl.BlockSpec((tq, dv), lambda b, h, i: (b * nq_t + i, h)),
        out_shape=jax.ShapeDtypeStruct((bsz * n_q, heads * dv), BF16),
        scratch_shapes=[pltpu.VMEM((tq, n_keys), F32), pltpu.VMEM((tq, n_keys), BF16),
                        pltpu.VMEM((tq, 1), F32)],
        compiler_params=_cparams(3),
        name=name,
    )(q, *([k] * n_parts), *([v] * n_parts))


def _dft_kernel(ca_ref, sa_ref, cb_ref, sb_ref, cos_ref, sin_ref):
    cb, sb = cb_ref[...], sb_ref[...]
    for q in range(cos_ref.shape[1] // LANES):
        ca, sa = ca_ref[:, q:q + 1], sa_ref[:, q:q + 1]
        cols = slice(q * LANES, (q + 1) * LANES)
        cos_ref[:, cols] = (ca * cb - sa * sb).astype(BF16)
        sin_ref[:, cols] = (sa * cb + ca * sb).astype(BF16)


def dft_tables(n, half_shift):
    f = jnp.arange(n, dtype=jnp.int32)[:, None]
    q = jnp.arange(n // LANES, dtype=jnp.int32)[None, :]
    r = jnp.arange(LANES, dtype=jnp.int32)[None, :]
    if half_shift:
        period = 8 * n
        pa, pb = ((2 * f + 1) * (2 * LANES * q)) % period, ((2 * f + 1) * (2 * r + 1)) % period
    else:
        period = 4 * n
        pa, pb = ((2 * f + 1) * (LANES * q)) % period, ((2 * f + 1) * r) % period
    ang_a = pa.astype(F32) * (2.0 * math.pi / period)
    ang_b = pb.astype(F32) * (2.0 * math.pi / period)
    tr = min(n, 256)
    a_spec = pl.BlockSpec((tr, n // LANES), lambda i: (i, 0))
    b_spec = pl.BlockSpec((tr, LANES), lambda i: (i, 0))
    o_spec = pl.BlockSpec((tr, n), lambda i: (i, 0))
    return pl.pallas_call(
        _dft_kernel,
        grid=(n // tr,),
        in_specs=[a_spec, a_spec, b_spec, b_spec],
        out_specs=[o_spec, o_spec],
        out_shape=[jax.ShapeDtypeStruct((n, n), BF16)] * 2,
        compiler_params=_cparams(1),
        name="dft_tables",
    )(jnp.cos(ang_a), jnp.sin(ang_a), jnp.cos(ang_b), jnp.sin(ang_b))


def hyena_time_filters(n, w1, b1, w2, b2, w3, freq):
    t = jnp.linspace(0.0, 1.0, n, dtype=F32)[:, None]
    ang = 2.0 * math.pi * jnp.arange(n, dtype=F32)[:, None] / n
    bands = jnp.linspace(1e-4, HY_BANDS - 1, HY_BANDS, dtype=F32)
    feats = jnp.concatenate([t, jnp.cos(ang * bands), jnp.sin(ang * bands)], axis=-1)
    fr = freq.astype(F32)
    hp = lax.Precision.HIGHEST
    hdn = jnp.sin(fr * (jnp.dot(feats, w1, precision=hp) + b1))
    hdn = jnp.sin(fr * (jnp.dot(hdn, w2, precision=hp) + b2))
    h = jnp.dot(hdn, w3, precision=hp).reshape(n, HY_ORDER, 2, HY_W)
    deltas = jnp.abs(jnp.linspace(math.log(HY_TARGET) / HY_SLOW, math.log(HY_TARGET) / HY_FAST,
                                  HY_W, dtype=F32))
    h = h * jnp.exp(-t * deltas)[:, None, None, :]
    h_fwd = h[:, :, 0]
    h_bwd = h[:, :, 1].at[0].set(0.0)
    r = lax.rsqrt(jnp.sum(h_fwd * h_fwd, axis=0) + jnp.sum(h_bwd * h_bwd, axis=0) + EPS)
    kf = (h_fwd * r).reshape(n, HY_ORDER * HY_W)
    kb = (h_bwd * r).reshape(n, HY_ORDER * HY_W)
    return kf + kb, kb - kf


HY_CHUNK = 256


def _spectra_kernel(ks_ref, kd_ref, c_ref, s_ref, kr_ref, ki_ref, *, inv_n):
    kr_ref[...] = _dot(c_ref[...], ks_ref[...].astype(BF16)) * inv_n
    ki_ref[...] = _dot(s_ref[...], kd_ref[...].astype(BF16)) * inv_n


def hyena_spectra(ksum, kdiff, cmat, smat):
    n, cols = ksum.shape
    spec = pl.BlockSpec((n, HY_CHUNK), lambda i: (0, i))
    return pl.pallas_call(
        functools.partial(_spectra_kernel, inv_n=1.0 / n),
        grid=(cols // HY_CHUNK,),
        in_specs=[spec, spec, _resident((n, n)), _resident((n, n))],
        out_specs=[spec, spec],
        out_shape=[jax.ShapeDtypeStruct((n, cols), F32)] * 2,
        compiler_params=_cparams(1),
        name="hyena_spectra",
    )(ksum, kdiff, cmat, smat)


def _hyena_kernel(v_ref, x1_ref, x2_ref, wv_ref, w1_ref, w2_ref, skip_ref,
                  k1r_ref, k1i_ref, k2r_ref, k2i_ref, c_ref, s_ref, o_ref, zb_scr, y_scr, *, n, fc):
    lane_groups = [slice(h * LANES, (h + 1) * LANES) for h in range(o_ref.shape[1] // LANES)]

    def short_conv(u_ref, w_ref, cs):
        u = u_ref[:, cs].astype(F32)
        pos = lax.broadcasted_iota(jnp.int32, u.shape, 0)
        prev = jnp.where(pos == 0, 0.0, pltpu.roll(u, 1, 0))
        nxt = jnp.where(pos == n - 1, 0.0, pltpu.roll(u, n - 1, 0))
        return prev * w_ref[0:1, cs] + u * w_ref[1:2, cs] + nxt * w_ref[2:3, cs]

    def long_conv(kr_ref, ki_ref):
        zb = zb_scr[...]
        for j in range(n // fc):
            sl = slice(j * fc, (j + 1) * fc)
            a = _dot(c_ref[sl, :], zb)
            b = _dot(s_ref[sl, :], zb)
            kr = kr_ref[sl, :]
            ki = ki_ref[sl, :]
            yr = (a * kr + b * ki).astype(BF16)
            yi = (a * ki - b * kr).astype(BF16)
            y_scr[...] += _dot(c_ref[:, sl], yr) - _dot(s_ref[:, sl], yi)

    for cs in lane_groups:
        v = short_conv(v_ref, wv_ref, cs)
        zb_scr[:, cs] = v.astype(BF16)
        y_scr[:, cs] = v * skip_ref[0:1, cs]
    long_conv(k1r_ref, k1i_ref)
    for cs in lane_groups:
        z = short_conv(x1_ref, w1_ref, cs) * y_scr[:, cs]
        zb_scr[:, cs] = z.astype(BF16)
        y_scr[:, cs] = z * skip_ref[1:2, cs]
    long_conv(k2r_ref, k2i_ref)
    for cs in lane_groups:
        o_ref[:, cs] = (short_conv(x2_ref, w2_ref, cs) * y_scr[:, cs]).astype(o_ref.dtype)


def hyena_mixer(p, conv_w, skip, kr, ki, cmat, smat, *, bsz, n, row0):
    ch = HY_CHUNK
    nc = HY_W // ch
    fc = min(512, n)
    blk0 = row0 // n

    def u_spec(k):
        return pl.BlockSpec((n, ch), lambda c, b: (blk0 + b, k * nc + c))

    def w_spec(k):
        return pl.BlockSpec((3, ch), lambda c, b: (0, k * nc + c))

    def k_spec(order):
        return pl.BlockSpec((n, ch), lambda c, b: (0, order * nc + c), pipeline_mode=pl.Buffered(1))

    return pl.pallas_call(
        functools.partial(_hyena_kernel, n=n, fc=fc),
        grid=(nc, bsz),
        in_specs=[u_spec(0), u_spec(1), u_spec(2), w_spec(0), w_spec(1), w_spec(2),
                  pl.BlockSpec((HY_ORDER, ch), lambda c, b: (0, c)),
                  k_spec(0), k_spec(0), k_spec(1), k_spec(1),
                  _resident((n, n)), _resident((n, n))],
        out_specs=pl.BlockSpec((n, ch), lambda c, b: (b, c)),
        out_shape=jax.ShapeDtypeStruct((bsz * n, HY_W), BF16),
        scratch_shapes=[pltpu.VMEM((n, ch), BF16), pltpu.VMEM((n, ch), F32)],
        compiler_params=_cparams(2),
        name="hyena_n%d" % n,
    )(p, p, p, conv_w, conv_w, conv_w, skip, kr, ki, kr, ki, cmat, smat)


def _out_proj_kernel(*refs, tiles):
    refs = list(refs)
    take = lambda n: [refs.pop(0) for _ in range(n)]
    ho_refs, mo_refs, go_refs = take(len(tiles[0])), take(len(tiles[1])), take(len(tiles[2]))
    wh_ref, wm_ref, wg_ref = take(3)
    x_refs = take(len(tiles[3]))
    g1_ref, g2_ref, mod_ref, rwh_ref, rwl_ref, xo_ref, f_ref, lg_ref = refs
    mix = (_dot(_read_part(ho_refs, tiles[0]), wh_ref[...]) + _dot(_read_part(mo_refs, tiles[1]), wm_ref[...])
           + _dot(_read_part(go_refs, tiles[2]), wg_ref[...]))
    xn = _read_part(x_refs, tiles[3]) + mod_ref[0, 2:3, :] * (_rms(mix) * g1_ref[...])
    xo_ref[...] = xn
    f = (_rms(xn) * g2_ref[...]) * (1.0 + mod_ref[0, 4:5, :]) + mod_ref[0, 3:4, :]
    fh = f.astype(BF16)
    fl = (f - fh.astype(F32)).astype(BF16)
    f_ref[...] = fh
    lg_ref[...] = _dot_nt(rwh_ref[...], fh) + _dot_nt(rwh_ref[...], fl) + _dot_nt(rwl_ref[...], fh)


def out_proj(ho_parts, mo_parts, go_parts, wh, wm, wg, x_parts, g1, g2, mod, rwh, rwl, tm, mod_idx):
    tiles = tuple(_part_tiles(parts, tm) for parts in (ho_parts, mo_parts, go_parts, x_parts))
    t = sum(tiles[0]) * tm
    d = D_MODEL
    row = lambda w: pl.BlockSpec((tm, w), lambda i: (i, 0))
    vec = pl.BlockSpec((1, d), lambda i: (0, 0))
    return pl.pallas_call(
        functools.partial(_out_proj_kernel, tiles=tiles),
        grid=(t // tm,),
        in_specs=(_part_specs(ho_parts, tm) + _part_specs(mo_parts, tm) + _part_specs(go_parts, tm)
                  + [_resident(wh.shape), _resident(wm.shape), _resident(wg.shape)]
                  + _part_specs(x_parts, tm)
                  + [vec, vec, pl.BlockSpec((1, N_MOD, d), lambda i: (mod_idx(i), 0, 0)),
                     _resident(rwh.shape), _resident(rwl.shape)]),
        out_specs=[row(d), row(d), pl.BlockSpec((N_EXPERTS, tm), lambda i: (0, i))],
        out_shape=[jax.ShapeDtypeStruct((t, d), F32),
                   jax.ShapeDtypeStruct((t, d), BF16),
                   jax.ShapeDtypeStruct((N_EXPERTS, t), F32)],
        compiler_params=_cparams(1),
        name="out_proj",
    )(*ho_parts, *mo_parts, *go_parts, wh, wm, wg, *x_parts, g1, g2, mod, rwh, rwl)


def _moe_kernel(be_ref, bsrc_ref, bflag_ref, bnext_ref, bslot_ref, x_ref, wg_hbm, wu_hbm, wd_hbm, o_ref,
                wg_f, wu_f, wd_f, wg_s, wu_s, wd_s, sem, *, layer):
    i = pl.program_id(0)

    def weight_copies(expert, slot):
        pairs = ((wg_hbm, wg_f), (wu_hbm, wu_f), (wd_hbm, wd_f))
        return [pltpu.make_async_copy(src.at[layer, expert], dst.at[slot], sem.at[k, slot])
                for k, (src, dst) in enumerate(pairs)]

    @pl.when(bflag_ref[i] == 2)
    def _():
        expert, slot, nxt = be_ref[i], bslot_ref[i], bnext_ref[i]

        @pl.when(i == 0)
        def _():
            for cp in weight_copies(expert, slot):
                cp.start()

        for cp in weight_copies(expert, slot):
            cp.wait()

        @pl.when(nxt >= 0)
        def _():
            for cp in weight_copies(nxt, 1 - slot):
                cp.start()

        wg_s[...] = wg_f[slot].astype(BF16)
        wu_s[...] = wu_f[slot].astype(BF16)
        wd_s[...] = wd_f[slot].astype(BF16)

    @pl.when(bflag_ref[i] != 0)
    def _():
        x = x_ref[...]
        a = _silu(_dot(x, wg_s[...])) * _dot(x, wu_s[...])
        o_ref[...] = _dot(a.astype(BF16), wd_s[...]).astype(o_ref.dtype)

    @pl.when(bflag_ref[i] == 0)
    def _():
        o_ref[...] = jnp.zeros(o_ref.shape, o_ref.dtype)


def moe_experts(blk_expert, blk_src, blk_flag, blk_next, blk_slot, xs, wg, wu, wd, layer):
    n_rows, d = xs.shape
    hid = wg.shape[3]
    hbm = pl.BlockSpec(memory_space=pl.ANY)
    grid_spec = pltpu.PrefetchScalarGridSpec(
        num_scalar_prefetch=5,
        grid=(n_rows // MOE_BLOCK,),
        in_specs=[pl.BlockSpec((MOE_BLOCK, d), lambda i, be, bs, bf, bn, bl: (bs[i], 0)), hbm, hbm, hbm],
        out_specs=pl.BlockSpec((MOE_BLOCK, d), lambda i, be, bs, bf, bn, bl: (i, 0)),
        scratch_shapes=[pltpu.VMEM((2, d, hid), F32), pltpu.VMEM((2, d, hid), F32),
                        pltpu.VMEM((2, hid, d), F32),
                        pltpu.VMEM((d, hid), BF16), pltpu.VMEM((d, hid), BF16),
                        pltpu.VMEM((hid, d), BF16),
                        pltpu.SemaphoreType.DMA((3, 2))],
    )
    return pl.pallas_call(
        functools.partial(_moe_kernel, layer=layer),
        grid_spec=grid_spec,
        out_shape=jax.ShapeDtypeStruct((n_rows, d), BF16),
        compiler_params=_cparams(1),
        cost_estimate=pl.CostEstimate(flops=6 * n_rows * d * hid, transcendentals=n_rows * hid,
                                      bytes_accessed=4 * n_rows * d + 12 * N_EXPERTS * d * hid),
        name="moe_experts",
    )(blk_expert, blk_src, blk_flag, blk_next, blk_slot, xs, wg, wu, wd)


def _shared_expert_kernel(f_ref, sg_ref, su_ref, sd_ref, o_ref):
    f = f_ref[...]
    a = _silu(_dot(f, sg_ref[...])) * _dot(f, su_ref[...])
    o_ref[...] = _dot(a.astype(BF16), sd_ref[...]).astype(o_ref.dtype)


def shared_expert(f, sg, su, sd, tm):
    t, d = f.shape
    row = pl.BlockSpec((tm, d), lambda i: (i, 0))
    return pl.pallas_call(
        _shared_expert_kernel,
        grid=(t // tm,),
        in_specs=[row, _resident(sg.shape), _resident(su.shape), _resident(sd.shape)],
        out_specs=row,
        out_shape=jax.ShapeDtypeStruct((t, d), BF16),
        compiler_params=_cparams(1),
        cost_estimate=pl.CostEstimate(flops=6 * t * d * sg.shape[1], transcendentals=t * sg.shape[1],
                                      bytes_accessed=4 * t * d + 6 * d * sg.shape[1]),
        name="shared_expert",
    )(f, sg, su, sd)


def _ffn_combine_kernel(ys_ref, yg_ref, gate_ref, x_ref, g3_ref, mod_ref, o_ref):
    y = ys_ref[...].astype(F32)
    gates = gate_ref[...]
    for k in range(TOP_K):
        y = y + gates[:, k:k + 1] * yg_ref[k].astype(F32)
    o_ref[...] = x_ref[...] + mod_ref[0, 5:6, :] * (_rms(y) * g3_ref[...])


def ffn_combine(y_shared, y_gathered, gates, x_all, g3, mod, tm, mod_idx):
    t, d = y_shared.shape
    row = pl.BlockSpec((tm, d), lambda i: (i, 0))
    return pl.pallas_call(
        _ffn_combine_kernel,
        grid=(t // tm,),
        in_specs=[row, pl.BlockSpec((TOP_K, tm, d), lambda i: (0, i, 0)),
                  pl.BlockSpec((tm, gates.shape[1]), lambda i: (i, 0)), row,
                  pl.BlockSpec((1, d), lambda i: (0, 0)),
                  pl.BlockSpec((1, N_MOD, d), lambda i: (mod_idx(i), 0, 0))],
        out_specs=row,
        out_shape=jax.ShapeDtypeStruct((t, d), F32),
        compiler_params=_cparams(1),
        name="ffn_combine",
    )(y_shared, y_gathered, gates, x_all, g3, mod)


def _first_max(vals, idx, n):
    m = jnp.max(vals, axis=0, keepdims=True)
    return m, jnp.min(jnp.where(vals == m, idx, float(n)), axis=0, keepdims=True)


def _router_kernel(lg_ref, bias_ref, e_ref, g_ref, r_ref, cnt_ref, carry_ref):
    @pl.when(pl.program_id(0) == 0)
    def _():
        carry_ref[...] = jnp.zeros(carry_ref.shape, F32)

    lg = lg_ref[...]
    tt = lg.shape[1]
    gsz = N_EXPERTS // N_GROUPS
    neg = -jnp.inf
    scores = 1.0 / (1.0 + jnp.exp(-lg))
    biased = scores + bias_ref[...]
    sub = lax.broadcasted_iota(jnp.int32, (gsz, tt), 0).astype(F32)

    grp_rows = []
    for g in range(N_GROUPS):
        blk = biased[g * gsz:(g + 1) * gsz, :]
        m1, i1 = _first_max(blk, sub, gsz)
        m2 = jnp.max(jnp.where(sub == i1, neg, blk), axis=0, keepdims=True)
        grp_rows.append(m1 + m2)
    cur = jnp.concatenate(grp_rows, axis=0)
    gidx = lax.broadcasted_iota(jnp.int32, (N_GROUPS, tt), 0).astype(F32)
    gsel = jnp.zeros((N_GROUPS, tt), F32)
    for _ in range(TOPK_GROUPS):
        _, gi = _first_max(cur, gidx, N_GROUPS)
        hit = gidx == gi
        gsel = jnp.where(hit, 1.0, gsel)
        cur = jnp.where(hit, neg, cur)
    emask = jnp.concatenate([jnp.broadcast_to(gsel[g:g + 1, :], (gsz, tt)) for g in range(N_GROUPS)], axis=0)

    cand = jnp.where(emask > 0.0, biased, neg)
    eidx = lax.broadcasted_iota(jnp.int32, (N_EXPERTS, tt), 0).astype(F32)
    chosen = jnp.zeros((N_EXPERTS, tt), F32)
    e_rows, g_rows = [], []
    for _ in range(TOP_K):
        _, ei = _first_max(cand, eidx, N_EXPERTS)
        hit = eidx == ei
        e_rows.append(ei)
        g_rows.append(jnp.sum(jnp.where(hit, scores, 0.0), axis=0, keepdims=True))
        chosen = jnp.where(hit, 1.0, chosen)
        cand = jnp.where(hit, neg, cand)
    gsum = functools.reduce(jnp.add, g_rows)
    g_rows = [g / gsum * ROUTED_SCALE for g in g_rows]

    before = (lax.broadcasted_iota(jnp.int32, (tt, tt), 0) < lax.broadcasted_iota(jnp.int32, (tt, tt), 1))
    prefix = _dot(chosen.astype(BF16), jnp.where(before, 1.0, 0.0).astype(BF16))
    rank_all = prefix + carry_ref[...]
    r_rows = [jnp.sum(jnp.where(eidx == ei, rank_all, 0.0), axis=0, keepdims=True) for ei in e_rows]
    carry_ref[...] = carry_ref[...] + jnp.sum(chosen, axis=1, keepdims=True)
    cnt_ref[...] = carry_ref[...]

    pad = [jnp.zeros((8 - TOP_K, tt), F32)]
    e_ref[...] = jnp.concatenate(e_rows + pad, axis=0).astype(jnp.int32)
    g_ref[...] = jnp.concatenate(g_rows + pad, axis=0)
    r_ref[...] = jnp.concatenate(r_rows + pad, axis=0).astype(jnp.int32)


def router(logits_t, bias, tt):
    n_exp, t = logits_t.shape
    col = pl.BlockSpec((8, tt), lambda i: (0, i))
    return pl.pallas_call(
        _router_kernel,
        grid=(t // tt,),
        in_specs=[pl.BlockSpec((n_exp, tt), lambda i: (0, i)),
                  pl.BlockSpec((n_exp, 1), lambda i: (0, 0))],
        out_specs=[col, col, col, pl.BlockSpec((n_exp, 1), lambda i: (0, 0))],
        out_shape=[jax.ShapeDtypeStruct((8, t), jnp.int32), jax.ShapeDtypeStruct((8, t), F32),
                   jax.ShapeDtypeStruct((8, t), jnp.int32), jax.ShapeDtypeStruct((n_exp, 1), F32)],
        scratch_shapes=[pltpu.VMEM((n_exp, 1), F32)],
        compiler_params=_cparams(1),
        name="router",
    )(logits_t, bias.reshape(n_exp, 1).astype(F32))


def rope_tables(n, rot_dim, tm):
    rows = n // GRID_W
    row = jnp.repeat(jnp.arange(rows, dtype=F32), GRID_W)
    col = jnp.tile(jnp.arange(GRID_W, dtype=F32), rows)
    axis_dim = rot_dim // 2
    inv = ROPE_THETA ** (-jnp.arange(0, axis_dim, 2, dtype=F32) / axis_dim)
    ar, ac = row[:, None] * inv, col[:, None] * inv
    zero = jnp.zeros_like(ar)
    c = jnp.concatenate([jnp.cos(ar), jnp.cos(ar), jnp.cos(ac), jnp.cos(ac)], axis=-1)
    s1 = jnp.concatenate([-jnp.sin(ar), zero, -jnp.sin(ac), zero], axis=-1)
    s2 = jnp.concatenate([zero, jnp.sin(ar), zero, jnp.sin(ac)], axis=-1)

    def finish(tab, fill):
        tab = jnp.pad(tab, ((0, 0), (0, LANES - rot_dim)), constant_values=fill)
        return jnp.concatenate([tab, jnp.full((tm, LANES), fill, F32)], axis=0)

    return finish(c, 1.0), finish(s1, 0.0), finish(s2, 0.0)


def pack_w_in(w):
    hy = 3 * HY_W
    kv0 = hy + MLA_Q_RANK + GQA_HEADS * GQA_HEAD_DIM
    ckv = w[:, kv0:kv0 + MLA_KV_RANK]
    kr = w[:, kv0 + MLA_KV_RANK:kv0 + MLA_KV_RANK + MLA_ROPE]
    gk0 = kv0 + MLA_KV_RANK + MLA_ROPE
    gkv = w[:, gk0:]
    kr = jnp.pad(kr, ((0, 0), (0, LANES - MLA_ROPE)))
    return jnp.concatenate([w[:, :kv0], ckv, gkv, kr], axis=1).astype(BF16)


def pack_w_uq(w):
    w = w.reshape(MLA_Q_RANK, MLA_HEADS, MLA_NOPE + MLA_ROPE)
    w = jnp.pad(w, ((0, 0), (0, 0), (0, MLA_QK_PAD - MLA_NOPE - MLA_ROPE)))
    return w.reshape(MLA_Q_RANK, MLA_HEADS * MLA_QK_PAD).astype(BF16)


def pack_w_ukv(w):
    w = w.reshape(MLA_KV_RANK, MLA_HEADS, MLA_NOPE + MLA_V)
    k = w[:, :, :MLA_NOPE].reshape(MLA_KV_RANK, MLA_HEADS * MLA_NOPE)
    v = w[:, :, MLA_NOPE:].reshape(MLA_KV_RANK, MLA_HEADS * MLA_V)
    return jnp.concatenate([k, v], axis=1).astype(BF16)


def dispatch_plan(e_t, rank_t, counts):
    n_tok = e_t.shape[1]
    n_pairs = n_tok * TOP_K
    counts = counts.astype(jnp.int32)
    padded = (counts + MOE_BLOCK - 1) // MOE_BLOCK * MOE_BLOCK
    p_ends = jnp.cumsum(padded)
    p_starts = p_ends - padded
    n_blocks = -(-n_pairs // MOE_BLOCK) + N_EXPERTS
    n_rows = n_blocks * MOE_BLOCK
    blk = jnp.arange(n_blocks, dtype=jnp.int32)
    n_used = p_ends[-1] // MOE_BLOCK
    blk_src = jnp.minimum(blk, n_used - 1)
    blk_expert = jnp.sum((blk_src * MOE_BLOCK)[:, None] >= p_ends[None, :], axis=1).astype(jnp.int32)
    first = jnp.concatenate([jnp.ones((1,), bool), blk_expert[1:] != blk_expert[:-1]])
    blk_flag = jnp.where(blk < n_used, jnp.where(first, 2, 1), 0).astype(jnp.int32)

    experts = jnp.arange(N_EXPERTS, dtype=jnp.int32)
    used = counts > 0
    later = jnp.where(used[None, :] & (experts[None, :] > experts[:, None]), experts[None, :], N_EXPERTS)
    next_used = jnp.min(later, axis=1)
    next_used = jnp.where(next_used == N_EXPERTS, -1, next_used).astype(jnp.int32)
    slot_of = ((jnp.cumsum(used.astype(jnp.int32)) - 1) % 2).astype(jnp.int32)
    blk_onehot = blk_expert[:, None] == experts[None, :]
    blk_next = jnp.sum(jnp.where(blk_onehot, next_used[None, :], 0), axis=1).astype(jnp.int32)
    blk_slot = jnp.sum(jnp.where(blk_onehot, slot_of[None, :], 0), axis=1).astype(jnp.int32)

    onehot = e_t[:, :, None] == experts
    pos = jnp.sum(jnp.where(onehot, p_starts, 0), axis=-1) + rank_t

    stride = n_tok + 1
    tok = jnp.arange(n_tok, dtype=jnp.int32)
    real_keys = (e_t * stride + tok[None, :]).reshape(-1)
    cum_fill = jnp.cumsum(padded - counts)
    filler = jnp.arange(n_rows - n_pairs, dtype=jnp.int32)
    filler_exp = jnp.sum(filler[:, None] >= cum_fill[None, :], axis=1).astype(jnp.int32)
    filler_keys = filler_exp * stride + n_tok
    slot_tok = jnp.sort(jnp.concatenate([real_keys, filler_keys])) % stride
    spread = jnp.arange(n_rows, dtype=jnp.int32) % n_tok
    slot_tok = jnp.where(slot_tok == n_tok, spread, slot_tok)
    return (blk_expert, blk_src, blk_flag, blk_next, blk_slot), slot_tok, pos


def _pick_tile(*sizes):
    for tile in (512, 256, 128):
        if all(s % tile == 0 for s in sizes):
            return tile
    raise ValueError("row counts must be multiples of 128: %r" % (sizes,))


def kernel(x, c, ctx, c_ctx, ada_w, ada_b, norm_g, w_in, w_out, hy_conv, hy_w1, hy_b1, hy_w2, hy_b2,
           hy_w3, hy_freq, hy_skip, mla_q_norm, mla_kv_norm, mla_w_uq, mla_w_ukv, gqa_q_norm, gqa_k_norm,
           router_w, router_bias, exp_w_gate, exp_w_up, exp_w_down, sh_w_gate, sh_w_up, sh_w_down):
    bsz, n, d = x.shape
    n_ctx = ctx.shape[1]
    depth = ada_w.shape[0]
    assert d == D_MODEL and n % GRID_W == 0
    t_lat, t_ctx = bsz * n, bsz * n_ctx
    assert t_lat % n_ctx == 0
    tm = _pick_tile(n, t_ctx)
    tq = max(t for t in (1024, 512, 256) if n % t == 0)
    lat_tiles = t_lat // tm
    tiles_per_seq = n // tm
    tm_ffn = min(tm, 256)

    def mod_idx(tile):
        return lambda i: jnp.minimum(i // (n // tile), bsz)

    def rope_idx(i):
        return jnp.where(i < lat_tiles, i % tiles_per_seq, tiles_per_seq)

    x_parts = [x.reshape(t_lat, d), ctx.reshape(t_ctx, d)]
    mod_rows = -(-(bsz + 1) // 16) * 16
    s_in = jnp.concatenate([c, c_ctx[None], jnp.zeros((mod_rows - bsz - 1, d), F32)], axis=0)
    mod_all = adaln_all(s_in, ada_w, ada_b).reshape(depth, mod_rows, N_MOD, d)

    tabs_mla = rope_tables(n, MLA_ROPE, tm)
    tabs_gqa = rope_tables(n, GQA_HEAD_DIM, tm)
    mla_scale = (MLA_NOPE + MLA_ROPE) ** -0.5 * math.log2(math.e)
    gqa_scale = GQA_HEAD_DIM ** -0.5 * math.log2(math.e)
    dft ={m: (dft_tables(m, False), dft_tables(m, True)) for m in (n, n_ctx)}

    for l in range(depth):
        last = l == depth - 1
        mod = mod_all[l]
        vec = lambda a: a.reshape(1, -1)

        p = in_proj(x_parts, vec(norm_g[l, 0]), mod, pack_w_in(w_in[l]), tm, mod_idx(tm))
        q_m = mla_q(p, vec(mla_q_norm[l]), pack_w_uq(mla_w_uq[l]), tabs_mla, tm, rope_idx, mla_scale)
        k_m, v_m = mla_kv(p, vec(mla_kv_norm[l]), pack_w_ukv(mla_w_ukv[l]), tabs_mla, tm, rope_idx)
        q_g, k_g, v_g = gqa_qkv(p, vec(gqa_q_norm[l]), vec(gqa_k_norm[l]), tabs_gqa, tm, rope_idx, gqa_scale)

        mla_args = dict(bsz=bsz, heads=MLA_HEADS, kv_group=1, dk=MLA_QK_PAD, dv=MLA_V)
        gqa_args = dict(bsz=bsz, heads=GQA_HEADS, kv_group=GQA_HEADS // GQA_KV_HEADS, dk=GQA_HEAD_DIM,
                        dv=GQA_HEAD_DIM)
        lat_q = dict(n_q=n, tq=tq, q_row0=0, kv_parts=[(t_lat, n_ctx), (0, n)])
        mo = attention(q_m, k_m, v_m, name="mla_attn", **lat_q, **mla_args)
        go = attention(q_g, k_g, v_g, name="gqa_attn", **lat_q, **gqa_args)

        filt = (hy_w1[l], hy_b1[l], hy_w2[l], hy_b2[l], hy_w3[l], hy_freq[l])

        def hyena(m, row0):
            lag_tabs, sym_tabs = dft[m]
            kr, ki = hyena_spectra(*hyena_time_filters(m, *filt), *lag_tabs)
            return hyena_mixer(p, hy_conv[l], hy_skip[l], kr, ki, *sym_tabs, bsz=bsz, n=m, row0=row0)

        mo, go, ho = [mo], [go], [hyena(n, 0)]

        if not last:
            ctx_q = dict(n_q=n_ctx, tq=n_ctx, q_row0=t_lat, kv_parts=[(t_lat, n_ctx)])
            mo.append(attention(q_m, k_m, v_m, name="mla_attn_ctx", **ctx_q, **mla_args))
            go.append(attention(q_g, k_g, v_g, name="gqa_attn_ctx", **ctx_q, **gqa_args))
            ho.append(hyena(n_ctx, t_lat))

        wo = w_out[l].astype(BF16)
        rw_t = router_w[l].T
        rw_hi = rw_t.astype(BF16)
        rw_lo = (rw_t - rw_hi.astype(F32)).astype(BF16)
        x_mid, f, logits_t = out_proj(ho, mo, go, wo[:HY_W], wo[HY_W:HY_W + MLA_HEADS * MLA_V],
                                      wo[HY_W + MLA_HEADS * MLA_V:], x_parts, vec(norm_g[l, 1]),
                                      vec(norm_g[l, 2]), mod, rw_hi, rw_lo, tm, mod_idx(tm))

        e_t, gate_t, rank_t, counts = router(logits_t, router_bias[l], tm)
        blocks, slot_tok, pos = dispatch_plan(e_t[:TOP_K], rank_t[:TOP_K], counts[:, 0])
        xs = f[slot_tok]
        y_shared = shared_expert(f, sh_w_gate[l].astype(BF16), sh_w_up[l].astype(BF16),
                                 sh_w_down[l].astype(BF16), tm)
        ys = moe_experts(*blocks, xs, exp_w_gate, exp_w_up, exp_w_down, l)
        x_parts = [ffn_combine(y_shared, ys[pos], gate_t.T, x_mid, vec(norm_g[l, 3]), mod, tm_ffn,
                               mod_idx(tm_ffn))]

    return x_parts[0].reshape(bsz, n, d)
```

```python
import functools
import math

import jax
import jax.numpy as jnp
from jax import lax
from jax.experimental import pallas as pl
from jax.experimental.pallas import tpu as pltpu

F32 = jnp.float32
BF16 = jnp.bfloat16

D_MODEL = 2048
GRID_W = 64
EPS = 1e-6
N_MOD = 6
HY_W = D_MODEL // 4
HY_ORDER = 2
HY_BANDS = 16
HY_TARGET = 1e-2
HY_FAST = 0.3
HY_SLOW = 1.5
MLA_NOPE = 128
MLA_ROPE = 64
MLA_V = 128
MLA_HEADS = 6
MLA_Q_RANK = 768
MLA_KV_RANK = 256
GQA_HEAD_DIM = 128
GQA_HEADS = 6
GQA_KV_HEADS = 2
ROPE_THETA = 10000.0
N_EXPERTS = 64
TOP_K = 6
N_GROUPS = 8
TOPK_GROUPS = 4
EXPERT_HIDDEN = D_MODEL // 4
ROUTED_SCALE = 2.5

LANES = 128
VMEM_LIMIT_BYTES = 56 * 1024 * 1024

COL_HY = 0
COL_MQ = 3 * HY_W
COL_GQ = COL_MQ + MLA_Q_RANK
COL_CKV = COL_GQ + GQA_HEADS * GQA_HEAD_DIM
COL_GK = COL_CKV + MLA_KV_RANK
COL_GV = COL_GK + GQA_KV_HEADS * GQA_HEAD_DIM
COL_KR = COL_GV + GQA_KV_HEADS * GQA_HEAD_DIM
IN_COLS_PAD = COL_KR + LANES
MLA_QK_PAD = 2 * LANES
MOE_BLOCK = 256

NT_DIMS = (((1,), (1,)), ((), ()))


def _cparams(n_axes):
    return pltpu.CompilerParams(dimension_semantics=("arbitrary",) * n_axes,
                                vmem_limit_bytes=VMEM_LIMIT_BYTES)


def _resident(shape):
    nd = len(shape)
    return pl.BlockSpec(shape, lambda *_: (0,) * nd, pipeline_mode=pl.Buffered(1))


def _rms(x):
    return x * lax.rsqrt(jnp.mean(x * x, axis=-1, keepdims=True) + EPS)


def _silu(x):
    return x / (1.0 + jnp.exp(-x))


def _dot(a, b):
    return jnp.dot(a, b, preferred_element_type=F32)


def _dot_nt(a, b):
    return lax.dot_general(a, b, NT_DIMS, preferred_element_type=F32)


def _rope(x, c, s1, s2, shift):
    w = x.shape[-1]
    return x * c + pltpu.roll(x, w - shift, 1) * s1 + pltpu.roll(x, shift, 1) * s2


def _adaln_kernel(s_ref, w_ref, b_ref, o_ref):
    s = _silu(s_ref[...]).astype(BF16)
    o_ref[0] = _dot(s, w_ref[0].astype(BF16)) + b_ref[0]


def adaln_all(s_in, ada_w, ada_b):
    depth, d, n_out = ada_w.shape
    rows = s_in.shape[0]
    tn = 1024
    return pl.pallas_call(
        _adaln_kernel,
        grid=(depth, n_out // tn),
        in_specs=[pl.BlockSpec((rows, d), lambda l, j: (0, 0)),
                  pl.BlockSpec((1, d, tn), lambda l, j: (l, 0, j)),
                  pl.BlockSpec((1, 1, tn), lambda l, j: (l, 0, j))],
        out_specs=pl.BlockSpec((1, rows, tn), lambda l, j: (l, 0, j)),
        out_shape=jax.ShapeDtypeStruct((depth, rows, n_out), F32),
        compiler_params=_cparams(2),
        name="adaln",
    )(s_in, ada_w, ada_b.reshape(depth, 1, n_out))


def _part_tiles(parts, tm):
    return tuple(p.shape[0] // tm for p in parts)


def _part_specs(parts, tm):
    specs, start = [], 0
    for part in parts:
        tiles = part.shape[0] // tm
        specs.append(pl.BlockSpec((tm, part.shape[1]),
                                  lambda i, start=start, tiles=tiles: (jnp.clip(i - start, 0, tiles - 1), 0)))
        start += tiles
    return specs


def _read_part(refs, tiles):
    val, start = refs[0][...], tiles[0]
    for ref, n_tiles in zip(refs[1:], tiles[1:]):
        val = jnp.where(pl.program_id(0) >= start, ref[...], val)
        start += n_tiles
    return val


def _in_proj_kernel(*refs, x_tiles):
    x_refs = refs[:len(x_tiles)]
    g_ref, mod_ref, w_ref, o_ref = refs[len(x_tiles):]
    h = _rms(_read_part(x_refs, x_tiles)) * g_ref[...]
    h = h * (1.0 + mod_ref[0, 1:2, :]) + mod_ref[0, 0:1, :]
    o_ref[...] = _dot(h.astype(BF16), w_ref[...]).astype(o_ref.dtype)


def in_proj(x_parts, g, mod, w, tm, mod_idx):
    d, n_out = w.shape
    x_tiles = _part_tiles(x_parts, tm)
    t = sum(x_tiles) * tm
    return pl.pallas_call(
        functools.partial(_in_proj_kernel, x_tiles=x_tiles),
        grid=(t // tm,),
        in_specs=_part_specs(x_parts, tm) + [pl.BlockSpec((1, d), lambda i: (0, 0)),
                                             pl.BlockSpec((1, N_MOD, d), lambda i: (mod_idx(i), 0, 0)),
                                             _resident((d, n_out))],
        out_specs=pl.BlockSpec((tm, n_out), lambda i: (i, 0)),
        out_shape=jax.ShapeDtypeStruct((t, n_out), BF16),
        compiler_params=_cparams(1),
        name="in_proj",
    )(*x_parts, g, mod, w)


def _mla_q_kernel(cq_ref, qn_ref, w_ref, c_ref, s1_ref, s2_ref, o_ref, *, scale):
    hn = (_rms(cq_ref[...].astype(F32)) * qn_ref[...]).astype(BF16)
    q = _dot(hn, w_ref[...]) * scale
    c, s1, s2 = c_ref[...], s1_ref[...], s2_ref[...]
    for h in range(MLA_HEADS):
        lo = h * MLA_QK_PAD
        o_ref[:, lo:lo + MLA_NOPE] = q[:, lo:lo + MLA_NOPE].astype(BF16)
        r = q[:, lo + MLA_NOPE:lo + MLA_QK_PAD]
        o_ref[:, lo + MLA_NOPE:lo + MLA_QK_PAD] = _rope(r, c, s1, s2, MLA_ROPE // 4).astype(BF16)


def mla_q(p, q_norm, w_uq, tabs, tm, rope_idx, scale):
    t = p.shape[0]
    n_out = MLA_HEADS * MLA_QK_PAD
    tab_spec = pl.BlockSpec((tm, LANES), lambda i: (rope_idx(i), 0))
    return pl.pallas_call(
        functools.partial(_mla_q_kernel, scale=scale),
        grid=(t // tm,),
        in_specs=[pl.BlockSpec((tm, MLA_Q_RANK), lambda i: (i, COL_MQ // MLA_Q_RANK)),
                  pl.BlockSpec((1, MLA_Q_RANK), lambda i: (0, 0)),
                  _resident((MLA_Q_RANK, n_out)),
                  tab_spec, tab_spec, tab_spec],
        out_specs=pl.BlockSpec((tm, n_out), lambda i: (i, 0)),
        out_shape=jax.ShapeDtypeStruct((t, n_out), BF16),
        compiler_params=_cparams(1),
        name="mla_q",
    )(p, q_norm, w_uq, *tabs)


def _mla_kv_kernel(ckv_ref, kr_ref, kvn_ref, w_ref, c_ref, s1_ref, s2_ref, k_ref, v_ref):
    hn = (_rms(ckv_ref[...].astype(F32)) * kvn_ref[...]).astype(BF16)
    kv = _dot(hn, w_ref[...])
    kr = _rope(kr_ref[...].astype(F32), c_ref[...], s1_ref[...], s2_ref[...],
               MLA_ROPE // 4).astype(BF16)
    ones = jnp.ones((kv.shape[0], MLA_V), BF16)
    v0 = MLA_HEADS * MLA_NOPE
    for h in range(MLA_HEADS):
        lo = h * MLA_QK_PAD
        k_ref[:, lo:lo + MLA_NOPE] = kv[:, h * MLA_NOPE:(h + 1) * MLA_NOPE].astype(BF16)
        k_ref[:, lo + MLA_NOPE:lo + MLA_QK_PAD] = kr
        v_ref[:, 2 * h * MLA_V:(2 * h + 1) * MLA_V] = kv[:, v0 + h * MLA_V:v0 + (h + 1) * MLA_V].astype(BF16)
        v_ref[:, (2 * h + 1) * MLA_V:(2 * h + 2) * MLA_V] = ones


def mla_kv(p, kv_norm, w_ukv, tabs, tm, rope_idx):
    t = p.shape[0]
    nk = MLA_HEADS * MLA_QK_PAD
    nv = MLA_HEADS * 2 * MLA_V
    tab_spec = pl.BlockSpec((tm, LANES), lambda i: (rope_idx(i), 0))
    return pl.pallas_call(
        _mla_kv_kernel,
        grid=(t // tm,),
        in_specs=[pl.BlockSpec((tm, MLA_KV_RANK), lambda i: (i, COL_CKV // MLA_KV_RANK)),
                  pl.BlockSpec((tm, LANES), lambda i: (i, COL_KR // LANES)),
                  pl.BlockSpec((1, MLA_KV_RANK), lambda i: (0, 0)),
                  _resident((MLA_KV_RANK, MLA_HEADS * (MLA_NOPE + MLA_V))),
                  tab_spec, tab_spec, tab_spec],
        out_specs=[pl.BlockSpec((tm, nk), lambda i: (i, 0)),
                   pl.BlockSpec((tm, nv), lambda i: (i, 0))],
        out_shape=[jax.ShapeDtypeStruct((t, nk), BF16),
                   jax.ShapeDtypeStruct((t, nv), BF16)],
        compiler_params=_cparams(1),
        name="mla_kv",
    )(p, p, kv_norm, w_ukv, *tabs)


def _gqa_qkv_kernel(q_ref, k_ref, v_ref, qn_ref, kn_ref, c_ref, s1_ref, s2_ref, qo_ref, ko_ref, vo_ref, *, scale):
    c, s1, s2 = c_ref[...], s1_ref[...], s2_ref[...]
    for h in range(GQA_HEADS):
        sl = slice(h * GQA_HEAD_DIM, (h + 1) * GQA_HEAD_DIM)
        x = _rms(q_ref[:, sl].astype(F32)) * qn_ref[...]
        qo_ref[:, sl] = (_rope(x, c, s1, s2, GQA_HEAD_DIM // 4) * scale).astype(BF16)
    ones = jnp.ones((v_ref.shape[0], GQA_HEAD_DIM), BF16)
    for g in range(GQA_KV_HEADS):
        sl = slice(g * GQA_HEAD_DIM, (g + 1) * GQA_HEAD_DIM)
        x = _rms(k_ref[:, sl].astype(F32)) * kn_ref[...]
        ko_ref[:, sl] = _rope(x, c, s1, s2, GQA_HEAD_DIM // 4).astype(BF16)
        vo_ref[:, 2 * g * GQA_HEAD_DIM:(2 * g + 1) * GQA_HEAD_DIM] = v_ref[:, sl]
        vo_ref[:, (2 * g + 1) * GQA_HEAD_DIM:(2 * g + 2) * GQA_HEAD_DIM] = ones


def gqa_qkv(p, q_norm, k_norm, tabs, tm, rope_idx, scale):
    t = p.shape[0]
    nq = GQA_HEADS * GQA_HEAD_DIM
    nk = GQA_KV_HEADS * GQA_HEAD_DIM
    tab_spec = pl.BlockSpec((tm, LANES), lambda i: (rope_idx(i), 0))
    return pl.pallas_call(
        functools.partial(_gqa_qkv_kernel, scale=scale),
        grid=(t // tm,),
        in_specs=[pl.BlockSpec((tm, nq), lambda i: (i, COL_GQ // nq)),
                  pl.BlockSpec((tm, nk), lambda i: (i, COL_GK // nk)),
                  pl.BlockSpec((tm, nk), lambda i: (i, COL_GV // nk)),
                  pl.BlockSpec((1, GQA_HEAD_DIM), lambda i: (0, 0)),
                  pl.BlockSpec((1, GQA_HEAD_DIM), lambda i: (0, 0)),
                  tab_spec, tab_spec, tab_spec],
        out_specs=[pl.BlockSpec((tm, nq), lambda i: (i, 0)),
                   pl.BlockSpec((tm, nk), lambda i: (i, 0)),
                   pl.BlockSpec((tm, 2 * nk), lambda i: (i, 0))],
        out_shape=[jax.ShapeDtypeStruct((t, nq), BF16),
                   jax.ShapeDtypeStruct((t, nk), BF16),
                   jax.ShapeDtypeStruct((t, 2 * nk), BF16)],
        compiler_params=_cparams(1),
        name="gqa_qkv",
    )(p, p, p, q_norm, k_norm, *tabs)


ATTN_SPLIT_ROWS = 256
ATTN_EXP_ROWS = 16


def _attn_kernel(*refs, n_parts):
    q_ref = refs[0]
    k_refs = refs[1:1 + n_parts]
    v_refs = refs[1 + n_parts:1 + 2 * n_parts]
    o_ref, s_scr, p_scr, m_scr = refs[1 + 2 * n_parts:]
    tq = q_ref.shape[0]
    dv = o_ref.shape[1]
    bounds = [0]
    for k in k_refs:
        bounds.append(bounds[-1] + k.shape[0])
    split = min(ATTN_SPLIT_ROWS, tq)
    groups = [slice(r, r + split) for r in range(0, tq, split)]

    for rs in groups:
        q = q_ref[rs, :]
        m = None
        for j, k in enumerate(k_refs):
            s = _dot_nt(q, k[...])
            s_scr[rs, bounds[j]:bounds[j + 1]] = s
            mj = jnp.max(s, axis=-1, keepdims=True)
            m = mj if m is None else jnp.maximum(m, mj)
        m_scr[rs, :] = m
    for r in range(0, tq, ATTN_EXP_ROWS):
        rows = slice(r, r + ATTN_EXP_ROWS)
        p_scr[rows, :] = jnp.exp2(s_scr[rows, :] - m_scr[rows, :]).astype(BF16)
    for rs in groups:
        o2 = functools.reduce(jnp.add, [_dot(p_scr[rs, bounds[j]:bounds[j + 1]], v[...])
                                        for j, v in enumerate(v_refs)])
        o_ref[rs, :] = (o2[:, :dv] / o2[:, dv:]).astype(o_ref.dtype)


def attention(q, k, v, *, bsz, n_q, tq, q_row0, kv_parts, heads, kv_group, dk, dv, name):
    assert tq % min(ATTN_SPLIT_ROWS, tq) == 0 and tq % ATTN_EXP_ROWS == 0
    nq_t = n_q // tq
    n_keys = sum(n_rows for _, n_rows in kv_parts)
    in_specs = [pl.BlockSpec((tq, dk), lambda b, h, i: (q_row0 // tq + b * nq_t + i, h))]
    for row0, n_rows in kv_parts:
        in_specs.append(pl.BlockSpec(
            (n_rows, dk), lambda b, h, i, row0=row0, n_rows=n_rows: (row0 // n_rows + b, h // kv_group)))
    for row0, n_rows in kv_parts:
        in_specs.append(pl.BlockSpec(
            (n_rows, 2 * dv), lambda b, h, i, row0=row0, n_rows=n_rows: (row0 // n_rows + b, h // kv_group)))
    n_parts = len(kv_parts)
    return pl.pallas_call(
        functools.partial(_attn_kernel, n_parts=n_parts),
        grid=(bsz, heads, nq_t),
        in_specs=in_specs,
        out_specs=pl.BlockSpec((tq, dv), lambda b, h, i: (b * nq_t + i, h)),
        out_shape=jax.ShapeDtypeStruct((bsz * n_q, heads * dv), BF16),
        scratch_shapes=[pltpu.VMEM((tq, n_keys), F32), pltpu.VMEM((tq, n_keys), BF16),
                        pltpu.VMEM((tq, 1), F32)],
        compiler_params=_cparams(3),
        name=name,
    )(q, *([k] * n_parts), *([v] * n_parts))


def _dft_kernel(ca_ref, sa_ref, cb_ref, sb_ref, cos_ref, sin_ref):
    cb, sb = cb_ref[...], sb_ref[...]
    for q in range(cos_ref.shape[1] // LANES):
        ca, sa = ca_ref[:, q:q + 1], sa_ref[:, q:q + 1]
        cols = slice(q * LANES, (q + 1) * LANES)
        cos_ref[:, cols] = (ca * cb - sa * sb).astype(BF16)
        sin_ref[:, cols] = (sa * cb + ca * sb).astype(BF16)


def dft_tables(n, half_shift):
    f = jnp.arange(n, dtype=jnp.int32)[:, None]
    q = jnp.arange(n // LANES, dtype=jnp.int32)[None, :]
    r = jnp.arange(LANES, dtype=jnp.int32)[None, :]
    if half_shift:
        period = 8 * n
        pa, pb = ((2 * f + 1) * (2 * LANES * q)) % period, ((2 * f + 1) * (2 * r + 1)) % period
    else:
        period = 4 * n
        pa, pb = ((2 * f + 1) * (LANES * q)) % period, ((2 * f + 1) * r) % period
    ang_a = pa.astype(F32) * (2.0 * math.pi / period)
    ang_b = pb.astype(F32) * (2.0 * math.pi / period)
    tr = min(n, 256)
    a_spec = pl.BlockSpec((tr, n // LANES), lambda i: (i, 0))
    b_spec = pl.BlockSpec((tr, LANES), lambda i: (i, 0))
    o_spec = pl.BlockSpec((tr, n), lambda i: (i, 0))
    return pl.pallas_call(
        _dft_kernel,
        grid=(n // tr,),
        in_specs=[a_spec, a_spec, b_spec, b_spec],
        out_specs=[o_spec, o_spec],
        out_shape=[jax.ShapeDtypeStruct((n, n), BF16)] * 2,
        compiler_params=_cparams(1),
        name="dft_tables",
    )(jnp.cos(ang_a), jnp.sin(ang_a), jnp.cos(ang_b), jnp.sin(ang_b))


def hyena_time_filters(n, w1, b1, w2, b2, w3, freq):
    t = jnp.linspace(0.0, 1.0, n, dtype=F32)[:, None]
    ang = 2.0 * math.pi * jnp.arange(n, dtype=F32)[:, None] / n
    bands = jnp.linspace(1e-4, HY_BANDS - 1, HY_BANDS, dtype=F32)
    feats = jnp.concatenate([t, jnp.cos(ang * bands), jnp.sin(ang * bands)], axis=-1)
    fr = freq.astype(F32)
    hp = lax.Precision.HIGHEST
    hdn = jnp.sin(fr * (jnp.dot(feats, w1, precision=hp) + b1))
    hdn = jnp.sin(fr * (jnp.dot(hdn, w2, precision=hp) + b2))
    h = jnp.dot(hdn, w3, precision=hp).reshape(n, HY_ORDER, 2, HY_W)
    deltas = jnp.abs(jnp.linspace(math.log(HY_TARGET) / HY_SLOW, math.log(HY_TARGET) / HY_FAST,
                                  HY_W, dtype=F32))
    h = h * jnp.exp(-t * deltas)[:, None, None, :]
    h_fwd = h[:, :, 0]
    h_bwd = h[:, :, 1].at[0].set(0.0)
    r = lax.rsqrt(jnp.sum(h_fwd * h_fwd, axis=0) + jnp.sum(h_bwd * h_bwd, axis=0) + EPS)
    kf = (h_fwd * r).reshape(n, HY_ORDER * HY_W)
    kb = (h_bwd * r).reshape(n, HY_ORDER * HY_W)
    return kf + kb, kb - kf


HY_CHUNK = 256


def _spectra_kernel(ks_ref, kd_ref, c_ref, s_ref, kr_ref, ki_ref, *, inv_n):
    kr_ref[...] = _dot(c_ref[...], ks_ref[...].astype(BF16)) * inv_n
    ki_ref[...] = _dot(s_ref[...], kd_ref[...].astype(BF16)) * inv_n


def hyena_spectra(ksum, kdiff, cmat, smat):
    n, cols = ksum.shape
    spec = pl.BlockSpec((n, HY_CHUNK), lambda i: (0, i))
    return pl.pallas_call(
        functools.partial(_spectra_kernel, inv_n=1.0 / n),
        grid=(cols // HY_CHUNK,),
        in_specs=[spec, spec, _resident((n, n)), _resident((n, n))],
        out_specs=[spec, spec],
        out_shape=[jax.ShapeDtypeStruct((n, cols), F32)] * 2,
        compiler_params=_cparams(1),
        name="hyena_spectra",
    )(ksum, kdiff, cmat, smat)


def _hyena_kernel(v_ref, x1_ref, x2_ref, wv_ref, w1_ref, w2_ref, skip_ref,
                  k1r_ref, k1i_ref, k2r_ref, k2i_ref, c_ref, s_ref, o_ref, zb_scr, y_scr, *, n, fc):
    lane_groups = [slice(h * LANES, (h + 1) * LANES) for h in range(o_ref.shape[1] // LANES)]

    def short_conv(u_ref, w_ref, cs):
        u = u_ref[:, cs].astype(F32)
        pos = lax.broadcasted_iota(jnp.int32, u.shape, 0)
        prev = jnp.where(pos == 0, 0.0, pltpu.roll(u, 1, 0))
        nxt = jnp.where(pos == n - 1, 0.0, pltpu.roll(u, n - 1, 0))
        return prev * w_ref[0:1, cs] + u * w_ref[1:2, cs] + nxt * w_ref[2:3, cs]

    def long_conv(kr_ref, ki_ref):
        zb = zb_scr[...]
        for j in range(n // fc):
            sl = slice(j * fc, (j + 1) * fc)
            a = _dot(c_ref[sl, :], zb)
            b = _dot(s_ref[sl, :], zb)
            kr = kr_ref[sl, :]
            ki = ki_ref[sl, :]
            yr = (a * kr + b * ki).astype(BF16)
            yi = (a * ki - b * kr).astype(BF16)
            y_scr[...] += _dot(c_ref[:, sl], yr) - _dot(s_ref[:, sl], yi)

    for cs in lane_groups:
        v = short_conv(v_ref, wv_ref, cs)
        zb_scr[:, cs] = v.astype(BF16)
        y_scr[:, cs] = v * skip_ref[0:1, cs]
    long_conv(k1r_ref, k1i_ref)
    for cs in lane_groups:
        z = short_conv(x1_ref, w1_ref, cs) * y_scr[:, cs]
        zb_scr[:, cs] = z.astype(BF16)
        y_scr[:, cs] = z * skip_ref[1:2, cs]
    long_conv(k2r_ref, k2i_ref)
    for cs in lane_groups:
        o_ref[:, cs] = (short_conv(x2_ref, w2_ref, cs) * y_scr[:, cs]).astype(o_ref.dtype)


def hyena_mixer(p, conv_w, skip, kr, ki, cmat, smat, *, bsz, n, row0):
    ch = HY_CHUNK
    nc = HY_W // ch
    fc = min(512, n)
    blk0 = row0 // n

    def u_spec(k):
        return pl.BlockSpec((n, ch), lambda c, b: (blk0 + b, k * nc + c))

    def w_spec(k):
        return pl.BlockSpec((3, ch), lambda c, b: (0, k * nc + c))

    def k_spec(order):
        return pl.BlockSpec((n, ch), lambda c, b: (0, order * nc + c), pipeline_mode=pl.Buffered(1))

    return pl.pallas_call(
        functools.partial(_hyena_kernel, n=n, fc=fc),
        grid=(nc, bsz),
        in_specs=[u_spec(0), u_spec(1), u_spec(2), w_spec(0), w_spec(1), w_spec(2),
                  pl.BlockSpec((HY_ORDER, ch), lambda c, b: (0, c)),
                  k_spec(0), k_spec(0), k_spec(1), k_spec(1),
                  _resident((n, n)), _resident((n, n))],
        out_specs=pl.BlockSpec((n, ch), lambda c, b: (b, c)),
        out_shape=jax.ShapeDtypeStruct((bsz * n, HY_W), BF16),
        scratch_shapes=[pltpu.VMEM((n, ch), BF16), pltpu.VMEM((n, ch), F32)],
        compiler_params=_cparams(2),
        name="hyena_n%d" % n,
    )(p, p, p, conv_w, conv_w, conv_w, skip, kr, ki, kr, ki, cmat, smat)


def _out_proj_kernel(*refs, tiles):
    refs = list(refs)
    take = lambda n: [refs.pop(0) for _ in range(n)]
    ho_refs, mo_refs, go_refs = take(len(tiles[0])), take(len(tiles[1])), take(len(tiles[2]))
    wh_ref, wm_ref, wg_ref = take(3)
    x_refs = take(len(tiles[3]))
    g1_ref, g2_ref, mod_ref, rwh_ref, rwl_ref, xo_ref, f_ref, lg_ref = refs
    mix = (_dot(_read_part(ho_refs, tiles[0]), wh_ref[...]) + _dot(_read_part(mo_refs, tiles[1]), wm_ref[...])
           + _dot(_read_part(go_refs, tiles[2]), wg_ref[...]))
    xn = _read_part(x_refs, tiles[3]) + mod_ref[0, 2:3, :] * (_rms(mix) * g1_ref[...])
    xo_ref[...] = xn
    f = (_rms(xn) * g2_ref[...]) * (1.0 + mod_ref[0, 4:5, :]) + mod_ref[0, 3:4, :]
    fh = f.astype(BF16)
    fl = (f - fh.astype(F32)).astype(BF16)
    f_ref[...] = fh
    lg_ref[...] = _dot_nt(rwh_ref[...], fh) + _dot_nt(rwh_ref[...], fl) + _dot_nt(rwl_ref[...], fh)


def out_proj(ho_parts, mo_parts, go_parts, wh, wm, wg, x_parts, g1, g2, mod, rwh, rwl, tm, mod_idx):
    tiles = tuple(_part_tiles(parts, tm) for parts in (ho_parts, mo_parts, go_parts, x_parts))
    t = sum(tiles[0]) * tm
    d = D_MODEL
    row = lambda w: pl.BlockSpec((tm, w), lambda i: (i, 0))
    vec = pl.BlockSpec((1, d), lambda i: (0, 0))
    return pl.pallas_call(
        functools.partial(_out_proj_kernel, tiles=tiles),
        grid=(t // tm,),
        in_specs=(_part_specs(ho_parts, tm) + _part_specs(mo_parts, tm) + _part_specs(go_parts, tm)
                  + [_resident(wh.shape), _resident(wm.shape), _resident(wg.shape)]
                  + _part_specs(x_parts, tm)
                  + [vec, vec, pl.BlockSpec((1, N_MOD, d), lambda i: (mod_idx(i), 0, 0)),
                     _resident(rwh.shape), _resident(rwl.shape)]),
        out_specs=[row(d), row(d), pl.BlockSpec((N_EXPERTS, tm), lambda i: (0, i))],
        out_shape=[jax.ShapeDtypeStruct((t, d), F32),
                   jax.ShapeDtypeStruct((t, d), BF16),
                   jax.ShapeDtypeStruct((N_EXPERTS, t), F32)],
        compiler_params=_cparams(1),
        name="out_proj",
    )(*ho_parts, *mo_parts, *go_parts, wh, wm, wg, *x_parts, g1, g2, mod, rwh, rwl)


def _moe_kernel(be_ref, bsrc_ref, bflag_ref, bnext_ref, bslot_ref, x_ref, wg_hbm, wu_hbm, wd_hbm, o_ref,
                wg_f, wu_f, wd_f, wg_s, wu_s, wd_s, sem, *, layer):
    i = pl.program_id(0)

    def weight_copies(expert, slot):
        pairs = ((wg_hbm, wg_f), (wu_hbm, wu_f), (wd_hbm, wd_f))
        return [pltpu.make_async_copy(src.at[layer, expert], dst.at[slot], sem.at[k, slot])
                for k, (src, dst) in enumerate(pairs)]

    @pl.when(bflag_ref[i] == 2)
    def _():
        expert, slot, nxt = be_ref[i], bslot_ref[i], bnext_ref[i]

        @pl.when(i == 0)
        def _():
            for cp in weight_copies(expert, slot):
                cp.start()

        for cp in weight_copies(expert, slot):
            cp.wait()

        @pl.when(nxt >= 0)
        def _():
            for cp in weight_copies(nxt, 1 - slot):
                cp.start()

        wg_s[...] = wg_f[slot].astype(BF16)
        wu_s[...] = wu_f[slot].astype(BF16)
        wd_s[...] = wd_f[slot].astype(BF16)

    @pl.when(bflag_ref[i] != 0)
    def _():
        x = x_ref[...]
        a = _silu(_dot(x, wg_s[...])) * _dot(x, wu_s[...])
        o_ref[...] = _dot(a.astype(BF16), wd_s[...]).astype(o_ref.dtype)

    @pl.when(bflag_ref[i] == 0)
    def _():
        o_ref[...] = jnp.zeros(o_ref.shape, o_ref.dtype)


def moe_experts(blk_expert, blk_src, blk_flag, blk_next, blk_slot, xs, wg, wu, wd, layer):
    n_rows, d = xs.shape
    hid = wg.shape[3]
    hbm = pl.BlockSpec(memory_space=pl.ANY)
    grid_spec = pltpu.PrefetchScalarGridSpec(
        num_scalar_prefetch=5,
        grid=(n_rows // MOE_BLOCK,),
        in_specs=[pl.BlockSpec((MOE_BLOCK, d), lambda i, be, bs, bf, bn, bl: (bs[i], 0)), hbm, hbm, hbm],
        out_specs=pl.BlockSpec((MOE_BLOCK, d), lambda i, be, bs, bf, bn, bl: (i, 0)),
        scratch_shapes=[pltpu.VMEM((2, d, hid), F32), pltpu.VMEM((2, d, hid), F32),
                        pltpu.VMEM((2, hid, d), F32),
                        pltpu.VMEM((d, hid), BF16), pltpu.VMEM((d, hid), BF16),
                        pltpu.VMEM((hid, d), BF16),
                        pltpu.SemaphoreType.DMA((3, 2))],
    )
    return pl.pallas_call(
        functools.partial(_moe_kernel, layer=layer),
        grid_spec=grid_spec,
        out_shape=jax.ShapeDtypeStruct((n_rows, d), BF16),
        compiler_params=_cparams(1),
        cost_estimate=pl.CostEstimate(flops=6 * n_rows * d * hid, transcendentals=n_rows * hid,
                                      bytes_accessed=4 * n_rows * d + 12 * N_EXPERTS * d * hid),
        name="moe_experts",
    )(blk_expert, blk_src, blk_flag, blk_next, blk_slot, xs, wg, wu, wd)


def _shared_expert_kernel(f_ref, sg_ref, su_ref, sd_ref, o_ref):
    f = f_ref[...]
    a = _silu(_dot(f, sg_ref[...])) * _dot(f, su_ref[...])
    o_ref[...] = _dot(a.astype(BF16), sd_ref[...]).astype(o_ref.dtype)


def shared_expert(f, sg, su, sd, tm):
    t, d = f.shape
    row = pl.BlockSpec((tm, d), lambda i: (i, 0))
    return pl.pallas_call(
        _shared_expert_kernel,
        grid=(t // tm,),
        in_specs=[row, _resident(sg.shape), _resident(su.shape), _resident(sd.shape)],
        out_specs=row,
        out_shape=jax.ShapeDtypeStruct((t, d), BF16),
        compiler_params=_cparams(1),
        cost_estimate=pl.CostEstimate(flops=6 * t * d * sg.shape[1], transcendentals=t * sg.shape[1],
                                      bytes_accessed=4 * t * d + 6 * d * sg.shape[1]),
        name="shared_expert",
    )(f, sg, su, sd)


def _ffn_combine_kernel(ys_ref, yg_ref, gate_ref, x_ref, g3_ref, mod_ref, *rest, project):
    y = ys_ref[...].astype(F32)
    gates = gate_ref[...]
    for k in range(TOP_K):
        y = y + gates[:, k:k + 1] * yg_ref[k].astype(F32)
    x_new = x_ref[...] + mod_ref[0, 5:6, :] * (_rms(y) * g3_ref[...])
    if not project:
        (o_ref,) = rest
        o_ref[...] = x_new
        return
    gn_ref, modn_ref, w_ref, o_ref, p_ref = rest
    o_ref[...] = x_new
    h = _rms(x_new) * gn_ref[...]
    h = h * (1.0 + modn_ref[0, 1:2, :]) + modn_ref[0, 0:1, :]
    p_ref[...] = _dot(h.astype(BF16), w_ref[...]).astype(p_ref.dtype)


def ffn_combine(y_shared, y_gathered, gates, x_all, g3, mod, tm, mod_idx, next_proj=None):
    t, d = y_shared.shape
    row = pl.BlockSpec((tm, d), lambda i: (i, 0))
    vec = pl.BlockSpec((1, d), lambda i: (0, 0))
    mod_spec = pl.BlockSpec((1, N_MOD, d), lambda i: (mod_idx(i), 0, 0))
    in_specs = [row, pl.BlockSpec((TOP_K, tm, d), lambda i: (0, i, 0)),
                pl.BlockSpec((tm, gates.shape[1]), lambda i: (i, 0)), row, vec, mod_spec]
    out_specs, out_shape, extra = [row], [jax.ShapeDtypeStruct((t, d), F32)], ()
    if next_proj is not None:
        n_out = next_proj[2].shape[1]
        in_specs += [vec, mod_spec, _resident((d, n_out))]
        out_specs.append(pl.BlockSpec((tm, n_out), lambda i: (i, 0)))
        out_shape.append(jax.ShapeDtypeStruct((t, n_out), BF16))
        extra = tuple(next_proj)
    return pl.pallas_call(
        functools.partial(_ffn_combine_kernel, project=next_proj is not None),
        grid=(t // tm,),
        in_specs=in_specs,
        out_specs=out_specs,
        out_shape=out_shape,
        compiler_params=_cparams(1),
        name="ffn_combine",
    )(y_shared, y_gathered, gates, x_all, g3, mod, *extra)


def _first_max(vals, idx, n):
    m = jnp.max(vals, axis=0, keepdims=True)
    return m, jnp.min(jnp.where(vals == m, idx, float(n)), axis=0, keepdims=True)


def _router_kernel(lg_ref, bias_ref, e_ref, g_ref, r_ref, cnt_ref, carry_ref):
    @pl.when(pl.program_id(0) == 0)
    def _():
        carry_ref[...] = jnp.zeros(carry_ref.shape, F32)

    lg = lg_ref[...]
    tt = lg.shape[1]
    gsz = N_EXPERTS // N_GROUPS
    neg = -jnp.inf
    scores = 1.0 / (1.0 + jnp.exp(-lg))
    biased = scores + bias_ref[...]
    sub = lax.broadcasted_iota(jnp.int32, (gsz, tt), 0).astype(F32)

    grp_rows = []
    for g in range(N_GROUPS):
        blk = biased[g * gsz:(g + 1) * gsz, :]
        m1, i1 = _first_max(blk, sub, gsz)
        m2 = jnp.max(jnp.where(sub == i1, neg, blk), axis=0, keepdims=True)
        grp_rows.append(m1 + m2)
    cur = jnp.concatenate(grp_rows, axis=0)
    gidx = lax.broadcasted_iota(jnp.int32, (N_GROUPS, tt), 0).astype(F32)
    gsel = jnp.zeros((N_GROUPS, tt), F32)
    for _ in range(TOPK_GROUPS):
        _, gi = _first_max(cur, gidx, N_GROUPS)
        hit = gidx == gi
        gsel = jnp.where(hit, 1.0, gsel)
        cur = jnp.where(hit, neg, cur)
    emask = jnp.concatenate([jnp.broadcast_to(gsel[g:g + 1, :], (gsz, tt)) for g in range(N_GROUPS)], axis=0)

    cand = jnp.where(emask > 0.0, biased, neg)
    eidx = lax.broadcasted_iota(jnp.int32, (N_EXPERTS, tt), 0).astype(F32)
    chosen = jnp.zeros((N_EXPERTS, tt), F32)
    e_rows, g_rows = [], []
    for _ in range(TOP_K):
        _, ei = _first_max(cand, eidx, N_EXPERTS)
        hit = eidx == ei
        e_rows.append(ei)
        g_rows.append(jnp.sum(jnp.where(hit, scores, 0.0), axis=0, keepdims=True))
        chosen = jnp.where(hit, 1.0, chosen)
        cand = jnp.where(hit, neg, cand)
    gsum = functools.reduce(jnp.add, g_rows)
    g_rows = [g / gsum * ROUTED_SCALE for g in g_rows]

    before = (lax.broadcasted_iota(jnp.int32, (tt, tt), 0) < lax.broadcasted_iota(jnp.int32, (tt, tt), 1))
    prefix = _dot(chosen.astype(BF16), jnp.where(before, 1.0, 0.0).astype(BF16))
    rank_all = prefix + carry_ref[...]
    r_rows = [jnp.sum(jnp.where(eidx == ei, rank_all, 0.0), axis=0, keepdims=True) for ei in e_rows]
    carry_ref[...] = carry_ref[...] + jnp.sum(chosen, axis=1, keepdims=True)
    cnt_ref[...] = carry_ref[...]

    pad = [jnp.zeros((8 - TOP_K, tt), F32)]
    e_ref[...] = jnp.concatenate(e_rows + pad, axis=0).astype(jnp.int32)
    g_ref[...] = jnp.concatenate(g_rows + pad, axis=0)
    r_ref[...] = jnp.concatenate(r_rows + pad, axis=0).astype(jnp.int32)


def router(logits_t, bias, tt):
    n_exp, t = logits_t.shape
    col = pl.BlockSpec((8, tt), lambda i: (0, i))
    return pl.pallas_call(
        _router_kernel,
        grid=(t // tt,),
        in_specs=[pl.BlockSpec((n_exp, tt), lambda i: (0, i)),
                  pl.BlockSpec((n_exp, 1), lambda i: (0, 0))],
        out_specs=[col, col, col, pl.BlockSpec((n_exp, 1), lambda i: (0, 0))],
        out_shape=[jax.ShapeDtypeStruct((8, t), jnp.int32), jax.ShapeDtypeStruct((8, t), F32),
                   jax.ShapeDtypeStruct((8, t), jnp.int32), jax.ShapeDtypeStruct((n_exp, 1), F32)],
        scratch_shapes=[pltpu.VMEM((n_exp, 1), F32)],
        compiler_params=_cparams(1),
        name="router",
    )(logits_t, bias.reshape(n_exp, 1).astype(F32))


def rope_tables(n, rot_dim, tm):
    rows = n // GRID_W
    row = jnp.repeat(jnp.arange(rows, dtype=F32), GRID_W)
    col = jnp.tile(jnp.arange(GRID_W, dtype=F32), rows)
    axis_dim = rot_dim // 2
    inv = ROPE_THETA ** (-jnp.arange(0, axis_dim, 2, dtype=F32) / axis_dim)
    ar, ac = row[:, None] * inv, col[:, None] * inv
    zero = jnp.zeros_like(ar)
    c = jnp.concatenate([jnp.cos(ar), jnp.cos(ar), jnp.cos(ac), jnp.cos(ac)], axis=-1)
    s1 = jnp.concatenate([-jnp.sin(ar), zero, -jnp.sin(ac), zero], axis=-1)
    s2 = jnp.concatenate([zero, jnp.sin(ar), zero, jnp.sin(ac)], axis=-1)

    def finish(tab, fill):
        tab = jnp.pad(tab, ((0, 0), (0, LANES - rot_dim)), constant_values=fill)
        return jnp.concatenate([tab, jnp.full((tm, LANES), fill, F32)], axis=0)

    return finish(c, 1.0), finish(s1, 0.0), finish(s2, 0.0)


def pack_w_in(w):
    hy = 3 * HY_W
    kv0 = hy + MLA_Q_RANK + GQA_HEADS * GQA_HEAD_DIM
    ckv = w[:, kv0:kv0 + MLA_KV_RANK]
    kr = w[:, kv0 + MLA_KV_RANK:kv0 + MLA_KV_RANK + MLA_ROPE]
    gk0 = kv0 + MLA_KV_RANK + MLA_ROPE
    gkv = w[:, gk0:]
    kr = jnp.pad(kr, ((0, 0), (0, LANES - MLA_ROPE)))
    return jnp.concatenate([w[:, :kv0], ckv, gkv, kr], axis=1).astype(BF16)


def pack_w_uq(w):
    w = w.reshape(MLA_Q_RANK, MLA_HEADS, MLA_NOPE + MLA_ROPE)
    w = jnp.pad(w, ((0, 0), (0, 0), (0, MLA_QK_PAD - MLA_NOPE - MLA_ROPE)))
    return w.reshape(MLA_Q_RANK, MLA_HEADS * MLA_QK_PAD).astype(BF16)


def pack_w_ukv(w):
    w = w.reshape(MLA_KV_RANK, MLA_HEADS, MLA_NOPE + MLA_V)
    k = w[:, :, :MLA_NOPE].reshape(MLA_KV_RANK, MLA_HEADS * MLA_NOPE)
    v = w[:, :, MLA_NOPE:].reshape(MLA_KV_RANK, MLA_HEADS * MLA_V)
    return jnp.concatenate([k, v], axis=1).astype(BF16)


def dispatch_plan(e_t, rank_t, counts):
    n_tok = e_t.shape[1]
    n_pairs = n_tok * TOP_K
    counts = counts.astype(jnp.int32)
    padded = (counts + MOE_BLOCK - 1) // MOE_BLOCK * MOE_BLOCK
    p_ends = jnp.cumsum(padded)
    p_starts = p_ends - padded
    n_blocks = -(-n_pairs // MOE_BLOCK) + N_EXPERTS
    n_rows = n_blocks * MOE_BLOCK
    blk = jnp.arange(n_blocks, dtype=jnp.int32)
    n_used = p_ends[-1] // MOE_BLOCK
    blk_src = jnp.minimum(blk, n_used - 1)
    blk_expert = jnp.sum((blk_src * MOE_BLOCK)[:, None] >= p_ends[None, :], axis=1).astype(jnp.int32)
    first = jnp.concatenate([jnp.ones((1,), bool), blk_expert[1:] != blk_expert[:-1]])
    blk_flag = jnp.where(blk < n_used, jnp.where(first, 2, 1), 0).astype(jnp.int32)

    experts = jnp.arange(N_EXPERTS, dtype=jnp.int32)
    used = counts > 0
    later = jnp.where(used[None, :] & (experts[None, :] > experts[:, None]), experts[None, :], N_EXPERTS)
    next_used = jnp.min(later, axis=1)
    next_used = jnp.where(next_used == N_EXPERTS, -1, next_used).astype(jnp.int32)
    slot_of = ((jnp.cumsum(used.astype(jnp.int32)) - 1) % 2).astype(jnp.int32)
    blk_onehot = blk_expert[:, None] == experts[None, :]
    blk_next = jnp.sum(jnp.where(blk_onehot, next_used[None, :], 0), axis=1).astype(jnp.int32)
    blk_slot = jnp.sum(jnp.where(blk_onehot, slot_of[None, :], 0), axis=1).astype(jnp.int32)

    onehot = e_t[:, :, None] == experts
    pos = jnp.sum(jnp.where(onehot, p_starts, 0), axis=-1) + rank_t

    stride = n_tok + 1
    tok = jnp.arange(n_tok, dtype=jnp.int32)
    real_keys = (e_t * stride + tok[None, :]).reshape(-1)
    cum_fill = jnp.cumsum(padded - counts)
    filler = jnp.arange(n_rows - n_pairs, dtype=jnp.int32)
    filler_exp = jnp.sum(filler[:, None] >= cum_fill[None, :], axis=1).astype(jnp.int32)
    filler_keys = filler_exp * stride + n_tok
    slot_tok = jnp.sort(jnp.concatenate([real_keys, filler_keys])) % stride
    spread = jnp.arange(n_rows, dtype=jnp.int32) % n_tok
    slot_tok = jnp.where(slot_tok == n_tok, spread, slot_tok)
    return (blk_expert, blk_src, blk_flag, blk_next, blk_slot), slot_tok, pos


def _pick_tile(*sizes):
    for tile in (512, 256, 128):
        if all(s % tile == 0 for s in sizes):
            return tile
    raise ValueError("row counts must be multiples of 128: %r" % (sizes,))


def kernel(x, c, ctx, c_ctx, ada_w, ada_b, norm_g, w_in, w_out, hy_conv, hy_w1, hy_b1, hy_w2, hy_b2,
           hy_w3, hy_freq, hy_skip, mla_q_norm, mla_kv_norm, mla_w_uq, mla_w_ukv, gqa_q_norm, gqa_k_norm,
           router_w, router_bias, exp_w_gate, exp_w_up, exp_w_down, sh_w_gate, sh_w_up, sh_w_down):
    bsz, n, d = x.shape
    n_ctx = ctx.shape[1]
    depth = ada_w.shape[0]
    assert d == D_MODEL and n % GRID_W == 0
    t_lat, t_ctx = bsz * n, bsz * n_ctx
    assert t_lat % n_ctx == 0
    tm = _pick_tile(n, t_ctx)
    tq = max(t for t in (2048, 1024, 512, 256) if n % t == 0)
    lat_tiles = t_lat // tm
    tiles_per_seq = n // tm
    tm_ffn = min(tm, 256)

    def mod_idx(tile):
        return lambda i: jnp.minimum(i // (n // tile), bsz)

    def rope_idx(i):
        return jnp.where(i < lat_tiles, i % tiles_per_seq, tiles_per_seq)

    x_parts = [x.reshape(t_lat, d), ctx.reshape(t_ctx, d)]
    mod_rows = -(-(bsz + 1) // 16) * 16
    s_in = jnp.concatenate([c, c_ctx[None], jnp.zeros((mod_rows - bsz - 1, d), F32)], axis=0)
    mod_all = adaln_all(s_in, ada_w, ada_b).reshape(depth, mod_rows, N_MOD, d)

    tabs_mla = rope_tables(n, MLA_ROPE, tm)
    tabs_gqa = rope_tables(n, GQA_HEAD_DIM, tm)
    mla_scale = (MLA_NOPE + MLA_ROPE) ** -0.5 * math.log2(math.e)
    gqa_scale = GQA_HEAD_DIM ** -0.5 * math.log2(math.e)
    dft ={m: (dft_tables(m, False), dft_tables(m, True)) for m in (n, n_ctx)}

    p = None
    for l in range(depth):
        last = l == depth - 1
        mod = mod_all[l]
        vec = lambda a: a.reshape(1, -1)

        if p is None:
            p = in_proj(x_parts, vec(norm_g[l, 0]), mod, pack_w_in(w_in[l]), tm, mod_idx(tm))
        q_m = mla_q(p, vec(mla_q_norm[l]), pack_w_uq(mla_w_uq[l]), tabs_mla, tm, rope_idx, mla_scale)
        k_m, v_m = mla_kv(p, vec(mla_kv_norm[l]), pack_w_ukv(mla_w_ukv[l]), tabs_mla, tm, rope_idx)
        q_g, k_g, v_g = gqa_qkv(p, vec(gqa_q_norm[l]), vec(gqa_k_norm[l]), tabs_gqa, tm, rope_idx, gqa_scale)

        mla_args = dict(bsz=bsz, heads=MLA_HEADS, kv_group=1, dk=MLA_QK_PAD, dv=MLA_V)
        gqa_args = dict(bsz=bsz, heads=GQA_HEADS, kv_group=GQA_HEADS // GQA_KV_HEADS, dk=GQA_HEAD_DIM,
                        dv=GQA_HEAD_DIM)
        lat_q = dict(n_q=n, tq=tq, q_row0=0, kv_parts=[(t_lat, n_ctx), (0, n)])
        mo = attention(q_m, k_m, v_m, name="mla_attn", **lat_q, **mla_args)
        go = attention(q_g, k_g, v_g, name="gqa_attn", **lat_q, **gqa_args)

        filt = (hy_w1[l], hy_b1[l], hy_w2[l], hy_b2[l], hy_w3[l], hy_freq[l])

        def hyena(m, row0):
            lag_tabs, sym_tabs = dft[m]
            kr, ki = hyena_spectra(*hyena_time_filters(m, *filt), *lag_tabs)
            return hyena_mixer(p, hy_conv[l], hy_skip[l], kr, ki, *sym_tabs, bsz=bsz, n=m, row0=row0)

        mo, go, ho = [mo], [go], [hyena(n, 0)]

        if not last:
            ctx_q = dict(n_q=n_ctx, tq=n_ctx, q_row0=t_lat, kv_parts=[(t_lat, n_ctx)])
            mo.append(attention(q_m, k_m, v_m, name="mla_attn_ctx", **ctx_q, **mla_args))
            go.append(attention(q_g, k_g, v_g, name="gqa_attn_ctx", **ctx_q, **gqa_args))
            ho.append(hyena(n_ctx, t_lat))

        wo = w_out[l].astype(BF16)
        rw_t = router_w[l].T
        rw_hi = rw_t.astype(BF16)
        rw_lo = (rw_t - rw_hi.astype(F32)).astype(BF16)
        x_mid, f, logits_t = out_proj(ho, mo, go, wo[:HY_W], wo[HY_W:HY_W + MLA_HEADS * MLA_V],
                                      wo[HY_W + MLA_HEADS * MLA_V:], x_parts, vec(norm_g[l, 1]),
                                      vec(norm_g[l, 2]), mod, rw_hi, rw_lo, tm, mod_idx(tm))

        e_t, gate_t, rank_t, counts = router(logits_t, router_bias[l], tm)
        blocks, slot_tok, pos = dispatch_plan(e_t[:TOP_K], rank_t[:TOP_K], counts[:, 0])
        xs = f[slot_tok]
        y_shared = shared_expert(f, sh_w_gate[l].astype(BF16), sh_w_up[l].astype(BF16),
                                 sh_w_down[l].astype(BF16), tm)
        ys = moe_experts(*blocks, xs, exp_w_gate, exp_w_up, exp_w_down, l)
        next_proj = None if last else (vec(norm_g[l + 1, 0]), mod_all[l + 1], pack_w_in(w_in[l + 1]))
        outs = ffn_combine(y_shared, ys[pos], gate_t.T, x_mid, vec(norm_g[l, 3]), mod, tm_ffn,
                           mod_idx(tm_ffn), next_proj)
        x_parts, p = [outs[0]], (None if last else outs[1])

    return x_parts[0].reshape(bsz, n, d)
```

```python
import functools
import math

import jax
import jax.numpy as jnp
from jax import lax
from jax.experimental import pallas as pl
from jax.experimental.pallas import tpu as pltpu

F32 = jnp.float32
BF16 = jnp.bfloat16

D_MODEL = 2048
GRID_W = 64
EPS = 1e-6
N_MOD = 6
HY_W = D_MODEL // 4
HY_ORDER = 2
HY_BANDS = 16
HY_TARGET = 1e-2
HY_FAST = 0.3
HY_SLOW = 1.5
MLA_NOPE = 128
MLA_ROPE = 64
MLA_V = 128
MLA_HEADS = 6
MLA_Q_RANK = 768
MLA_KV_RANK = 256
GQA_HEAD_DIM = 128
GQA_HEADS = 6
GQA_KV_HEADS = 2
ROPE_THETA = 10000.0
N_EXPERTS = 64
TOP_K = 6
N_GROUPS = 8
TOPK_GROUPS = 4
EXPERT_HIDDEN = D_MODEL // 4
ROUTED_SCALE = 2.5

LANES = 128
VMEM_LIMIT_BYTES = 56 * 1024 * 1024

COL_HY = 0
COL_MQ = 3 * HY_W
COL_GQ = COL_MQ + MLA_Q_RANK
COL_CKV = COL_GQ + GQA_HEADS * GQA_HEAD_DIM
COL_GK = COL_CKV + MLA_KV_RANK
COL_GV = COL_GK + GQA_KV_HEADS * GQA_HEAD_DIM
COL_KR = COL_GV + GQA_KV_HEADS * GQA_HEAD_DIM
IN_COLS_PAD = COL_KR + LANES
MLA_QK_PAD = 2 * LANES
MOE_BLOCK = 256

NT_DIMS = (((1,), (1,)), ((), ()))


def _cparams(n_axes):
    return pltpu.CompilerParams(dimension_semantics=("arbitrary",) * n_axes,
                                vmem_limit_bytes=VMEM_LIMIT_BYTES)


def _resident(shape):
    nd = len(shape)
    return pl.BlockSpec(shape, lambda *_: (0,) * nd, pipeline_mode=pl.Buffered(1))


def _rms(x):
    return x * lax.rsqrt(jnp.mean(x * x, axis=-1, keepdims=True) + EPS)


def _silu(x):
    return x / (1.0 + jnp.exp(-x))


def _dot(a, b):
    return jnp.dot(a, b, preferred_element_type=F32)


def _dot_nt(a, b):
    return lax.dot_general(a, b, NT_DIMS, preferred_element_type=F32)


def _rope(x, c, s1, s2, shift):
    w = x.shape[-1]
    return x * c + pltpu.roll(x, w - shift, 1) * s1 + pltpu.roll(x, shift, 1) * s2


def _adaln_kernel(s_ref, w_ref, b_ref, o_ref):
    s = _silu(s_ref[...]).astype(BF16)
    o_ref[0] = _dot(s, w_ref[0].astype(BF16)) + b_ref[0]


def adaln_all(s_in, ada_w, ada_b):
    depth, d, n_out = ada_w.shape
    rows = s_in.shape[0]
    tn = 1024
    return pl.pallas_call(
        _adaln_kernel,
        grid=(depth, n_out // tn),
        in_specs=[pl.BlockSpec((rows, d), lambda l, j: (0, 0)),
                  pl.BlockSpec((1, d, tn), lambda l, j: (l, 0, j)),
                  pl.BlockSpec((1, 1, tn), lambda l, j: (l, 0, j))],
        out_specs=pl.BlockSpec((1, rows, tn), lambda l, j: (l, 0, j)),
        out_shape=jax.ShapeDtypeStruct((depth, rows, n_out), F32),
        compiler_params=_cparams(2),
        name="adaln",
    )(s_in, ada_w, ada_b.reshape(depth, 1, n_out))


def _part_tiles(parts, tm):
    return tuple(p.shape[0] // tm for p in parts)


def _part_specs(parts, tm):
    specs, start = [], 0
    for part in parts:
        tiles = part.shape[0] // tm
        specs.append(pl.BlockSpec((tm, part.shape[1]),
                                  lambda i, start=start, tiles=tiles: (jnp.clip(i - start, 0, tiles - 1), 0)))
        start += tiles
    return specs


def _read_part(refs, tiles):
    val, start = refs[0][...], tiles[0]
    for ref, n_tiles in zip(refs[1:], tiles[1:]):
        val = jnp.where(pl.program_id(0) >= start, ref[...], val)
        start += n_tiles
    return val


def _in_proj_kernel(*refs, x_tiles):
    x_refs = refs[:len(x_tiles)]
    g_ref, mod_ref, w_ref, o_ref = refs[len(x_tiles):]
    h = _rms(_read_part(x_refs, x_tiles)) * g_ref[...]
    h = h * (1.0 + mod_ref[0, 1:2, :]) + mod_ref[0, 0:1, :]
    o_ref[...] = _dot(h.astype(BF16), w_ref[...]).astype(o_ref.dtype)


def in_proj(x_parts, g, mod, w, tm, mod_idx):
    d, n_out = w.shape
    x_tiles = _part_tiles(x_parts, tm)
    t = sum(x_tiles) * tm
    return pl.pallas_call(
        functools.partial(_in_proj_kernel, x_tiles=x_tiles),
        grid=(t // tm,),
        in_specs=_part_specs(x_parts, tm) + [pl.BlockSpec((1, d), lambda i: (0, 0)),
                                             pl.BlockSpec((1, N_MOD, d), lambda i: (mod_idx(i), 0, 0)),
                                             _resident((d, n_out))],
        out_specs=pl.BlockSpec((tm, n_out), lambda i: (i, 0)),
        out_shape=jax.ShapeDtypeStruct((t, n_out), BF16),
        compiler_params=_cparams(1),
        name="in_proj",
    )(*x_parts, g, mod, w)


def _mla_q_kernel(cq_ref, qn_ref, w_ref, c_ref, s1_ref, s2_ref, o_ref, *, scale):
    hn = (_rms(cq_ref[...].astype(F32)) * qn_ref[...]).astype(BF16)
    q = _dot(hn, w_ref[...]) * scale
    c, s1, s2 = c_ref[...], s1_ref[...], s2_ref[...]
    for h in range(MLA_HEADS):
        lo = h * MLA_QK_PAD
        o_ref[:, lo:lo + MLA_NOPE] = q[:, lo:lo + MLA_NOPE].astype(BF16)
        r = q[:, lo + MLA_NOPE:lo + MLA_QK_PAD]
        o_ref[:, lo + MLA_NOPE:lo + MLA_QK_PAD] = _rope(r, c, s1, s2, MLA_ROPE // 4).astype(BF16)


def mla_q(p, q_norm, w_uq, tabs, tm, rope_idx, scale):
    t = p.shape[0]
    n_out = MLA_HEADS * MLA_QK_PAD
    tab_spec = pl.BlockSpec((tm, LANES), lambda i: (rope_idx(i), 0))
    return pl.pallas_call(
        functools.partial(_mla_q_kernel, scale=scale),
        grid=(t // tm,),
        in_specs=[pl.BlockSpec((tm, MLA_Q_RANK), lambda i: (i, COL_MQ // MLA_Q_RANK)),
                  pl.BlockSpec((1, MLA_Q_RANK), lambda i: (0, 0)),
                  _resident((MLA_Q_RANK, n_out)),
                  tab_spec, tab_spec, tab_spec],
        out_specs=pl.BlockSpec((tm, n_out), lambda i: (i, 0)),
        out_shape=jax.ShapeDtypeStruct((t, n_out), BF16),
        compiler_params=_cparams(1),
        name="mla_q",
    )(p, q_norm, w_uq, *tabs)


def _mla_kv_kernel(ckv_ref, kr_ref, kvn_ref, w_ref, c_ref, s1_ref, s2_ref, k_ref, v_ref):
    hn = (_rms(ckv_ref[...].astype(F32)) * kvn_ref[...]).astype(BF16)
    kv = _dot(hn, w_ref[...])
    kr = _rope(kr_ref[...].astype(F32), c_ref[...], s1_ref[...], s2_ref[...],
               MLA_ROPE // 4).astype(BF16)
    ones = jnp.ones((kv.shape[0], MLA_V), BF16)
    v0 = MLA_HEADS * MLA_NOPE
    for h in range(MLA_HEADS):
        lo = h * MLA_QK_PAD
        k_ref[:, lo:lo + MLA_NOPE] = kv[:, h * MLA_NOPE:(h + 1) * MLA_NOPE].astype(BF16)
        k_ref[:, lo + MLA_NOPE:lo + MLA_QK_PAD] = kr
        v_ref[:, 2 * h * MLA_V:(2 * h + 1) * MLA_V] = kv[:, v0 + h * MLA_V:v0 + (h + 1) * MLA_V].astype(BF16)
        v_ref[:, (2 * h + 1) * MLA_V:(2 * h + 2) * MLA_V] = ones


def mla_kv(p, kv_norm, w_ukv, tabs, tm, rope_idx):
    t = p.shape[0]
    nk = MLA_HEADS * MLA_QK_PAD
    nv = MLA_HEADS * 2 * MLA_V
    tab_spec = pl.BlockSpec((tm, LANES), lambda i: (rope_idx(i), 0))
    return pl.pallas_call(
        _mla_kv_kernel,
        grid=(t // tm,),
        in_specs=[pl.BlockSpec((tm, MLA_KV_RANK), lambda i: (i, COL_CKV // MLA_KV_RANK)),
                  pl.BlockSpec((tm, LANES), lambda i: (i, COL_KR // LANES)),
                  pl.BlockSpec((1, MLA_KV_RANK), lambda i: (0, 0)),
                  _resident((MLA_KV_RANK, MLA_HEADS * (MLA_NOPE + MLA_V))),
                  tab_spec, tab_spec, tab_spec],
        out_specs=[pl.BlockSpec((tm, nk), lambda i: (i, 0)),
                   pl.BlockSpec((tm, nv), lambda i: (i, 0))],
        out_shape=[jax.ShapeDtypeStruct((t, nk), BF16),
                   jax.ShapeDtypeStruct((t, nv), BF16)],
        compiler_params=_cparams(1),
        name="mla_kv",
    )(p, p, kv_norm, w_ukv, *tabs)


def _gqa_qkv_kernel(q_ref, k_ref, v_ref, qn_ref, kn_ref, c_ref, s_ref, qo_ref, ko_ref, vo_ref, *, scale):
    c, s = c_ref[...], s_ref[...]

    def rope(x):
        return x * c + pltpu.roll(x, GQA_HEAD_DIM // 2, 1) * s

    for h in range(GQA_HEADS):
        sl = slice(h * GQA_HEAD_DIM, (h + 1) * GQA_HEAD_DIM)
        x = _rms(q_ref[:, sl].astype(F32)) * qn_ref[...]
        qo_ref[:, sl] = (rope(x) * scale).astype(BF16)
    ones = jnp.ones((v_ref.shape[0], GQA_HEAD_DIM), BF16)
    for g in range(GQA_KV_HEADS):
        sl = slice(g * GQA_HEAD_DIM, (g + 1) * GQA_HEAD_DIM)
        x = _rms(k_ref[:, sl].astype(F32)) * kn_ref[...]
        ko_ref[:, sl] = rope(x).astype(BF16)
        vo_ref[:, 2 * g * GQA_HEAD_DIM:(2 * g + 1) * GQA_HEAD_DIM] = v_ref[:, sl]
        vo_ref[:, (2 * g + 1) * GQA_HEAD_DIM:(2 * g + 2) * GQA_HEAD_DIM] = ones


def gqa_qkv(p, q_norm, k_norm, tabs, tm, rope_idx, scale):
    t = p.shape[0]
    nq = GQA_HEADS * GQA_HEAD_DIM
    nk = GQA_KV_HEADS * GQA_HEAD_DIM
    tab_spec = pl.BlockSpec((tm, LANES), lambda i: (rope_idx(i), 0))
    return pl.pallas_call(
        functools.partial(_gqa_qkv_kernel, scale=scale),
        grid=(t // tm,),
        in_specs=[pl.BlockSpec((tm, nq), lambda i: (i, COL_GQ // nq)),
                  pl.BlockSpec((tm, nk), lambda i: (i, COL_GK // nk)),
                  pl.BlockSpec((tm, nk), lambda i: (i, COL_GV // nk)),
                  pl.BlockSpec((1, GQA_HEAD_DIM), lambda i: (0, 0)),
                  pl.BlockSpec((1, GQA_HEAD_DIM), lambda i: (0, 0)),
                  tab_spec, tab_spec],
        out_specs=[pl.BlockSpec((tm, nq), lambda i: (i, 0)),
                   pl.BlockSpec((tm, nk), lambda i: (i, 0)),
                   pl.BlockSpec((tm, 2 * nk), lambda i: (i, 0))],
        out_shape=[jax.ShapeDtypeStruct((t, nq), BF16),
                   jax.ShapeDtypeStruct((t, nk), BF16),
                   jax.ShapeDtypeStruct((t, 2 * nk), BF16)],
        compiler_params=_cparams(1),
        name="gqa_qkv",
    )(p, p, p, q_norm, k_norm, *tabs)


ATTN_SPLIT_ROWS = 256
ATTN_EXP_ROWS = 16


def _attn_kernel(*refs, n_parts):
    q_ref = refs[0]
    k_refs = refs[1:1 + n_parts]
    v_refs = refs[1 + n_parts:1 + 2 * n_parts]
    o_ref, s_scr, p_scr, m_scr = refs[1 + 2 * n_parts:]
    tq = q_ref.shape[0]
    dv = o_ref.shape[1]
    bounds = [0]
    for k in k_refs:
        bounds.append(bounds[-1] + k.shape[0])
    split = min(ATTN_SPLIT_ROWS, tq)
    groups = [slice(r, r + split) for r in range(0, tq, split)]

    for rs in groups:
        q = q_ref[rs, :]
        m = None
        for j, k in enumerate(k_refs):
            s = _dot_nt(q, k[...])
            s_scr[rs, bounds[j]:bounds[j + 1]] = s
            mj = jnp.max(s, axis=-1, keepdims=True)
            m = mj if m is None else jnp.maximum(m, mj)
        m_scr[rs, :] = m
    for r in range(0, tq, ATTN_EXP_ROWS):
        rows = slice(r, r + ATTN_EXP_ROWS)
        p_scr[rows, :] = jnp.exp2(s_scr[rows, :] - m_scr[rows, :]).astype(BF16)
    for rs in groups:
        o2 = functools.reduce(jnp.add, [_dot(p_scr[rs, bounds[j]:bounds[j + 1]], v[...])
                                        for j, v in enumerate(v_refs)])
        o_ref[rs, :] = (o2[:, :dv] / o2[:, dv:]).astype(o_ref.dtype)


def attention(q, k, v, *, bsz, n_q, tq, q_row0, kv_parts, heads, kv_group, dk, dv, name):
    assert tq % min(ATTN_SPLIT_ROWS, tq) == 0 and tq % ATTN_EXP_ROWS == 0
    nq_t = n_q // tq
    n_keys = sum(n_rows for _, n_rows in kv_parts)
    in_specs = [pl.BlockSpec((tq, dk), lambda b, h, i: (q_row0 // tq + b * nq_t + i, h))]
    for row0, n_rows in kv_parts:
        in_specs.append(pl.BlockSpec(
            (n_rows, dk), lambda b, h, i, row0=row0, n_rows=n_rows: (row0 // n_rows + b, h // kv_group)))
    for row0, n_rows in kv_parts:
        in_specs.append(pl.BlockSpec(
            (n_rows, 2 * dv), lambda b, h, i, row0=row0, n_rows=n_rows: (row0 // n_rows + b, h // kv_group)))
    n_parts = len(kv_parts)
    return pl.pallas_call(
        functools.partial(_attn_kernel, n_parts=n_parts),
        grid=(bsz, heads, nq_t),
        in_specs=in_specs,
        out_specs=pl.BlockSpec((tq, dv), lambda b, h, i: (b * nq_t + i, h)),
        out_shape=jax.ShapeDtypeStruct((bsz * n_q, heads * dv), BF16),
        scratch_shapes=[pltpu.VMEM((tq, n_keys), F32), pltpu.VMEM((tq, n_keys), BF16),
                        pltpu.VMEM((tq, 1), F32)],
        compiler_params=_cparams(3),
        name=name,
    )(q, *([k] * n_parts), *([v] * n_parts))


def _dft_kernel(ca_ref, sa_ref, cb_ref, sb_ref, cos_ref, sin_ref):
    cb, sb = cb_ref[...], sb_ref[...]
    for q in range(cos_ref.shape[1] // LANES):
        ca, sa = ca_ref[:, q:q + 1], sa_ref[:, q:q + 1]
        cols = slice(q * LANES, (q + 1) * LANES)
        cos_ref[:, cols] = (ca * cb - sa * sb).astype(BF16)
        sin_ref[:, cols] = (sa * cb + ca * sb).astype(BF16)


def dft_tables(n, half_shift):
    f = jnp.arange(n, dtype=jnp.int32)[:, None]
    q = jnp.arange(n // LANES, dtype=jnp.int32)[None, :]
    r = jnp.arange(LANES, dtype=jnp.int32)[None, :]
    if half_shift:
        period = 8 * n
        pa, pb = ((2 * f + 1) * (2 * LANES * q)) % period, ((2 * f + 1) * (2 * r + 1)) % period
    else:
        period = 4 * n
        pa, pb = ((2 * f + 1) * (LANES * q)) % period, ((2 * f + 1) * r) % period
    ang_a = pa.astype(F32) * (2.0 * math.pi / period)
    ang_b = pb.astype(F32) * (2.0 * math.pi / period)
    tr = min(n, 256)
    a_spec = pl.BlockSpec((tr, n // LANES), lambda i: (i, 0))
    b_spec = pl.BlockSpec((tr, LANES), lambda i: (i, 0))
    o_spec = pl.BlockSpec((tr, n), lambda i: (i, 0))
    return pl.pallas_call(
        _dft_kernel,
        grid=(n // tr,),
        in_specs=[a_spec, a_spec, b_spec, b_spec],
        out_specs=[o_spec, o_spec],
        out_shape=[jax.ShapeDtypeStruct((n, n), BF16)] * 2,
        compiler_params=_cparams(1),
        name="dft_tables",
    )(jnp.cos(ang_a), jnp.sin(ang_a), jnp.cos(ang_b), jnp.sin(ang_b))


def hyena_time_filters(n, w1, b1, w2, b2, w3, freq):
    t = jnp.linspace(0.0, 1.0, n, dtype=F32)[:, None]
    ang = 2.0 * math.pi * jnp.arange(n, dtype=F32)[:, None] / n
    bands = jnp.linspace(1e-4, HY_BANDS - 1, HY_BANDS, dtype=F32)
    feats = jnp.concatenate([t, jnp.cos(ang * bands), jnp.sin(ang * bands)], axis=-1)
    fr = freq.astype(F32)
    hp = lax.Precision.HIGHEST
    hdn = jnp.sin(fr * (jnp.dot(feats, w1, precision=hp) + b1))
    hdn = jnp.sin(fr * (jnp.dot(hdn, w2, precision=hp) + b2))
    h = jnp.dot(hdn, w3, precision=hp).reshape(n, HY_ORDER, 2, HY_W)
    deltas = jnp.abs(jnp.linspace(math.log(HY_TARGET) / HY_SLOW, math.log(HY_TARGET) / HY_FAST,
                                  HY_W, dtype=F32))
    h = h * jnp.exp(-t * deltas)[:, None, None, :]
    h_fwd = h[:, :, 0]
    h_bwd = h[:, :, 1].at[0].set(0.0)
    r = lax.rsqrt(jnp.sum(h_fwd * h_fwd, axis=0) + jnp.sum(h_bwd * h_bwd, axis=0) + EPS)
    kf = (h_fwd * r).reshape(n, HY_ORDER * HY_W)
    kb = (h_bwd * r).reshape(n, HY_ORDER * HY_W)
    return kf + kb, kb - kf


HY_CHUNK = 256


def _spectra_kernel(ks_ref, kd_ref, c_ref, s_ref, kr_ref, ki_ref, *, inv_n):
    kr_ref[...] = _dot(c_ref[...], ks_ref[...].astype(BF16)) * inv_n
    ki_ref[...] = _dot(s_ref[...], kd_ref[...].astype(BF16)) * inv_n


def hyena_spectra(ksum, kdiff, cmat, smat):
    n, cols = ksum.shape
    spec = pl.BlockSpec((n, HY_CHUNK), lambda i: (0, i))
    return pl.pallas_call(
        functools.partial(_spectra_kernel, inv_n=1.0 / n),
        grid=(cols // HY_CHUNK,),
        in_specs=[spec, spec, _resident((n, n)), _resident((n, n))],
        out_specs=[spec, spec],
        out_shape=[jax.ShapeDtypeStruct((n, cols), F32)] * 2,
        compiler_params=_cparams(1),
        name="hyena_spectra",
    )(ksum, kdiff, cmat, smat)


def _hyena_kernel(v_ref, x1_ref, x2_ref, wv_ref, w1_ref, w2_ref, skip_ref,
                  k1r_ref, k1i_ref, k2r_ref, k2i_ref, c_ref, s_ref, o_ref, zb_scr, y_scr, *, n, fc):
    lane_groups = [slice(h * LANES, (h + 1) * LANES) for h in range(o_ref.shape[1] // LANES)]

    def short_conv(u_ref, w_ref, cs):
        u = u_ref[:, cs].astype(F32)
        pos = lax.broadcasted_iota(jnp.int32, u.shape, 0)
        prev = jnp.where(pos == 0, 0.0, pltpu.roll(u, 1, 0))
        nxt = jnp.where(pos == n - 1, 0.0, pltpu.roll(u, n - 1, 0))
        return prev * w_ref[0:1, cs] + u * w_ref[1:2, cs] + nxt * w_ref[2:3, cs]

    def long_conv(kr_ref, ki_ref):
        zb = zb_scr[...]
        for j in range(n // fc):
            sl = slice(j * fc, (j + 1) * fc)
            a = _dot(c_ref[sl, :], zb)
            b = _dot(s_ref[sl, :], zb)
            kr = kr_ref[sl, :]
            ki = ki_ref[sl, :]
            yr = (a * kr + b * ki).astype(BF16)
            yi = (a * ki - b * kr).astype(BF16)
            y_scr[...] += _dot(c_ref[:, sl], yr) - _dot(s_ref[:, sl], yi)

    for cs in lane_groups:
        v = short_conv(v_ref, wv_ref, cs)
        zb_scr[:, cs] = v.astype(BF16)
        y_scr[:, cs] = v * skip_ref[0:1, cs]
    long_conv(k1r_ref, k1i_ref)
    for cs in lane_groups:
        z = short_conv(x1_ref, w1_ref, cs) * y_scr[:, cs]
        zb_scr[:, cs] = z.astype(BF16)
        y_scr[:, cs] = z * skip_ref[1:2, cs]
    long_conv(k2r_ref, k2i_ref)
    for cs in lane_groups:
        o_ref[:, cs] = (short_conv(x2_ref, w2_ref, cs) * y_scr[:, cs]).astype(o_ref.dtype)


def hyena_mixer(p, conv_w, skip, kr, ki, cmat, smat, *, bsz, n, row0):
    ch = HY_CHUNK
    nc = HY_W // ch
    fc = min(512, n)
    blk0 = row0 // n

    def u_spec(k):
        return pl.BlockSpec((n, ch), lambda c, b: (blk0 + b, k * nc + c))

    def w_spec(k):
        return pl.BlockSpec((3, ch), lambda c, b: (0, k * nc + c))

    def k_spec(order):
        return pl.BlockSpec((n, ch), lambda c, b: (0, order * nc + c), pipeline_mode=pl.Buffered(1))

    return pl.pallas_call(
        functools.partial(_hyena_kernel, n=n, fc=fc),
        grid=(nc, bsz),
        in_specs=[u_spec(0), u_spec(1), u_spec(2), w_spec(0), w_spec(1), w_spec(2),
                  pl.BlockSpec((HY_ORDER, ch), lambda c, b: (0, c)),
                  k_spec(0), k_spec(0), k_spec(1), k_spec(1),
                  _resident((n, n)), _resident((n, n))],
        out_specs=pl.BlockSpec((n, ch), lambda c, b: (b, c)),
        out_shape=jax.ShapeDtypeStruct((bsz * n, HY_W), BF16),
        scratch_shapes=[pltpu.VMEM((n, ch), BF16), pltpu.VMEM((n, ch), F32)],
        compiler_params=_cparams(2),
        name="hyena_n%d" % n,
    )(p, p, p, conv_w, conv_w, conv_w, skip, kr, ki, kr, ki, cmat, smat)


def _out_proj_kernel(*refs, tiles):
    refs = list(refs)
    take = lambda n: [refs.pop(0) for _ in range(n)]
    ho_refs, mo_refs, go_refs = take(len(tiles[0])), take(len(tiles[1])), take(len(tiles[2]))
    wh_ref, wm_ref, wg_ref = take(3)
    x_refs = take(len(tiles[3]))
    g1_ref, g2_ref, mod_ref, rwh_ref, rwl_ref, xo_ref, f_ref, lg_ref = refs
    mix = (_dot(_read_part(ho_refs, tiles[0]), wh_ref[...]) + _dot(_read_part(mo_refs, tiles[1]), wm_ref[...])
           + _dot(_read_part(go_refs, tiles[2]), wg_ref[...]))
    xn = _read_part(x_refs, tiles[3]) + mod_ref[0, 2:3, :] * (_rms(mix) * g1_ref[...])
    xo_ref[...] = xn
    f = (_rms(xn) * g2_ref[...]) * (1.0 + mod_ref[0, 4:5, :]) + mod_ref[0, 3:4, :]
    fh = f.astype(BF16)
    fl = (f - fh.astype(F32)).astype(BF16)
    f_ref[...] = fh
    lg_ref[...] = _dot_nt(rwh_ref[...], fh) + _dot_nt(rwh_ref[...], fl) + _dot_nt(rwl_ref[...], fh)


def out_proj(ho_parts, mo_parts, go_parts, wh, wm, wg, x_parts, g1, g2, mod, rwh, rwl, tm, mod_idx):
    tiles = tuple(_part_tiles(parts, tm) for parts in (ho_parts, mo_parts, go_parts, x_parts))
    t = sum(tiles[0]) * tm
    d = D_MODEL
    row = lambda w: pl.BlockSpec((tm, w), lambda i: (i, 0))
    vec = pl.BlockSpec((1, d), lambda i: (0, 0))
    return pl.pallas_call(
        functools.partial(_out_proj_kernel, tiles=tiles),
        grid=(t // tm,),
        in_specs=(_part_specs(ho_parts, tm) + _part_specs(mo_parts, tm) + _part_specs(go_parts, tm)
                  + [_resident(wh.shape), _resident(wm.shape), _resident(wg.shape)]
                  + _part_specs(x_parts, tm)
                  + [vec, vec, pl.BlockSpec((1, N_MOD, d), lambda i: (mod_idx(i), 0, 0)),
                     _resident(rwh.shape), _resident(rwl.shape)]),
        out_specs=[row(d), row(d), pl.BlockSpec((N_EXPERTS, tm), lambda i: (0, i))],
        out_shape=[jax.ShapeDtypeStruct((t, d), F32),
                   jax.ShapeDtypeStruct((t, d), BF16),
                   jax.ShapeDtypeStruct((N_EXPERTS, t), F32)],
        compiler_params=_cparams(1),
        name="out_proj",
    )(*ho_parts, *mo_parts, *go_parts, wh, wm, wg, *x_parts, g1, g2, mod, rwh, rwl)


def _moe_kernel(be_ref, bsrc_ref, bflag_ref, bnext_ref, bslot_ref, x_ref, wg_hbm, wu_hbm, wd_hbm, o_ref,
                wg_f, wu_f, wd_f, wg_s, wu_s, wd_s, sem, *, layer):
    i = pl.program_id(0)

    def weight_copies(expert, slot):
        pairs = ((wg_hbm, wg_f), (wu_hbm, wu_f), (wd_hbm, wd_f))
        return [pltpu.make_async_copy(src.at[layer, expert], dst.at[slot], sem.at[k, slot])
                for k, (src, dst) in enumerate(pairs)]

    @pl.when(bflag_ref[i] == 2)
    def _():
        expert, slot, nxt = be_ref[i], bslot_ref[i], bnext_ref[i]

        @pl.when(i == 0)
        def _():
            for cp in weight_copies(expert, slot):
                cp.start()

        for cp in weight_copies(expert, slot):
            cp.wait()

        @pl.when(nxt >= 0)
        def _():
            for cp in weight_copies(nxt, 1 - slot):
                cp.start()

        wg_s[...] = wg_f[slot].astype(BF16)
        wu_s[...] = wu_f[slot].astype(BF16)
        wd_s[...] = wd_f[slot].astype(BF16)

    @pl.when(bflag_ref[i] != 0)
    def _():
        x = x_ref[...]
        a = _silu(_dot(x, wg_s[...])) * _dot(x, wu_s[...])
        o_ref[...] = _dot(a.astype(BF16), wd_s[...]).astype(o_ref.dtype)

    @pl.when(bflag_ref[i] == 0)
    def _():
        o_ref[...] = jnp.zeros(o_ref.shape, o_ref.dtype)


def moe_experts(blk_expert, blk_src, blk_flag, blk_next, blk_slot, xs, wg, wu, wd, layer):
    n_rows, d = xs.shape
    hid = wg.shape[3]
    hbm = pl.BlockSpec(memory_space=pl.ANY)
    grid_spec = pltpu.PrefetchScalarGridSpec(
        num_scalar_prefetch=5,
        grid=(n_rows // MOE_BLOCK,),
        in_specs=[pl.BlockSpec((MOE_BLOCK, d), lambda i, be, bs, bf, bn, bl: (bs[i], 0)), hbm, hbm, hbm],
        out_specs=pl.BlockSpec((MOE_BLOCK, d), lambda i, be, bs, bf, bn, bl: (i, 0)),
        scratch_shapes=[pltpu.VMEM((2, d, hid), F32), pltpu.VMEM((2, d, hid), F32),
                        pltpu.VMEM((2, hid, d), F32),
                        pltpu.VMEM((d, hid), BF16), pltpu.VMEM((d, hid), BF16),
                        pltpu.VMEM((hid, d), BF16),
                        pltpu.SemaphoreType.DMA((3, 2))],
    )
    return pl.pallas_call(
        functools.partial(_moe_kernel, layer=layer),
        grid_spec=grid_spec,
        out_shape=jax.ShapeDtypeStruct((n_rows, d), BF16),
        compiler_params=_cparams(1),
        cost_estimate=pl.CostEstimate(flops=6 * n_rows * d * hid, transcendentals=n_rows * hid,
                                      bytes_accessed=4 * n_rows * d + 12 * N_EXPERTS * d * hid),
        name="moe_experts",
    )(blk_expert, blk_src, blk_flag, blk_next, blk_slot, xs, wg, wu, wd)


def _shared_expert_kernel(f_ref, sg_ref, su_ref, sd_ref, o_ref):
    f = f_ref[...]
    a = _silu(_dot(f, sg_ref[...])) * _dot(f, su_ref[...])
    o_ref[...] = _dot(a.astype(BF16), sd_ref[...]).astype(o_ref.dtype)


def shared_expert(f, sg, su, sd, tm):
    t, d = f.shape
    row = pl.BlockSpec((tm, d), lambda i: (i, 0))
    return pl.pallas_call(
        _shared_expert_kernel,
        grid=(t // tm,),
        in_specs=[row, _resident(sg.shape), _resident(su.shape), _resident(sd.shape)],
        out_specs=row,
        out_shape=jax.ShapeDtypeStruct((t, d), BF16),
        compiler_params=_cparams(1),
        cost_estimate=pl.CostEstimate(flops=6 * t * d * sg.shape[1], transcendentals=t * sg.shape[1],
                                      bytes_accessed=4 * t * d + 6 * d * sg.shape[1]),
        name="shared_expert",
    )(f, sg, su, sd)


def _ffn_combine_kernel(ys_ref, yg_ref, gate_ref, x_ref, g3_ref, mod_ref, *rest, project):
    y = ys_ref[...].astype(F32)
    gates = gate_ref[...]
    for k in range(TOP_K):
        y = y + gates[:, k:k + 1] * yg_ref[k].astype(F32)
    x_new = x_ref[...] + mod_ref[0, 5:6, :] * (_rms(y) * g3_ref[...])
    if not project:
        (o_ref,) = rest
        o_ref[...] = x_new
        return
    gn_ref, modn_ref, w_ref, o_ref, p_ref = rest
    o_ref[...] = x_new
    h = _rms(x_new) * gn_ref[...]
    h = h * (1.0 + modn_ref[0, 1:2, :]) + modn_ref[0, 0:1, :]
    p_ref[...] = _dot(h.astype(BF16), w_ref[...]).astype(p_ref.dtype)


def ffn_combine(y_shared, y_gathered, gates, x_all, g3, mod, tm, mod_idx, next_proj=None):
    t, d = y_shared.shape
    row = pl.BlockSpec((tm, d), lambda i: (i, 0))
    vec = pl.BlockSpec((1, d), lambda i: (0, 0))
    mod_spec = pl.BlockSpec((1, N_MOD, d), lambda i: (mod_idx(i), 0, 0))
    in_specs = [row, pl.BlockSpec((TOP_K, tm, d), lambda i: (0, i, 0)),
                pl.BlockSpec((tm, gates.shape[1]), lambda i: (i, 0)), row, vec, mod_spec]
    out_specs, out_shape, extra = [row], [jax.ShapeDtypeStruct((t, d), F32)], ()
    if next_proj is not None:
        n_out = next_proj[2].shape[1]
        in_specs += [vec, mod_spec, _resident((d, n_out))]
        out_specs.append(pl.BlockSpec((tm, n_out), lambda i: (i, 0)))
        out_shape.append(jax.ShapeDtypeStruct((t, n_out), BF16))
        extra = tuple(next_proj)
    return pl.pallas_call(
        functools.partial(_ffn_combine_kernel, project=next_proj is not None),
        grid=(t // tm,),
        in_specs=in_specs,
        out_specs=out_specs,
        out_shape=out_shape,
        compiler_params=_cparams(1),
        name="ffn_combine",
    )(y_shared, y_gathered, gates, x_all, g3, mod, *extra)


def _first_max(vals, idx, n):
    m = jnp.max(vals, axis=0, keepdims=True)
    return m, jnp.min(jnp.where(vals == m, idx, float(n)), axis=0, keepdims=True)


def _router_kernel(lg_ref, bias_ref, e_ref, g_ref, r_ref, cnt_ref, carry_ref):
    @pl.when(pl.program_id(0) == 0)
    def _():
        carry_ref[...] = jnp.zeros(carry_ref.shape, F32)

    lg = lg_ref[...]
    tt = lg.shape[1]
    gsz = N_EXPERTS // N_GROUPS
    neg = -jnp.inf
    scores = 1.0 / (1.0 + jnp.exp(-lg))
    biased = scores + bias_ref[...]
    sub = lax.broadcasted_iota(jnp.int32, (gsz, tt), 0).astype(F32)

    grp_rows = []
    for g in range(N_GROUPS):
        blk = biased[g * gsz:(g + 1) * gsz, :]
        m1, i1 = _first_max(blk, sub, gsz)
        m2 = jnp.max(jnp.where(sub == i1, neg, blk), axis=0, keepdims=True)
        grp_rows.append(m1 + m2)
    cur = jnp.concatenate(grp_rows, axis=0)
    gidx = lax.broadcasted_iota(jnp.int32, (N_GROUPS, tt), 0).astype(F32)
    gsel = jnp.zeros((N_GROUPS, tt), F32)
    for _ in range(TOPK_GROUPS):
        _, gi = _first_max(cur, gidx, N_GROUPS)
        hit = gidx == gi
        gsel = jnp.where(hit, 1.0, gsel)
        cur = jnp.where(hit, neg, cur)
    emask = jnp.concatenate([jnp.broadcast_to(gsel[g:g + 1, :], (gsz, tt)) for g in range(N_GROUPS)], axis=0)

    cand = jnp.where(emask > 0.0, biased, neg)
    eidx = lax.broadcasted_iota(jnp.int32, (N_EXPERTS, tt), 0).astype(F32)
    chosen = jnp.zeros((N_EXPERTS, tt), F32)
    e_rows, g_rows = [], []
    for _ in range(TOP_K):
        _, ei = _first_max(cand, eidx, N_EXPERTS)
        hit = eidx == ei
        e_rows.append(ei)
        g_rows.append(jnp.sum(jnp.where(hit, scores, 0.0), axis=0, keepdims=True))
        chosen = jnp.where(hit, 1.0, chosen)
        cand = jnp.where(hit, neg, cand)
    gsum = functools.reduce(jnp.add, g_rows)
    g_rows = [g / gsum * ROUTED_SCALE for g in g_rows]

    before = (lax.broadcasted_iota(jnp.int32, (tt, tt), 0) < lax.broadcasted_iota(jnp.int32, (tt, tt), 1))
    prefix = _dot(chosen.astype(BF16), jnp.where(before, 1.0, 0.0).astype(BF16))
    rank_all = prefix + carry_ref[...]
    r_rows = [jnp.sum(jnp.where(eidx == ei, rank_all, 0.0), axis=0, keepdims=True) for ei in e_rows]
    carry_ref[...] = carry_ref[...] + jnp.sum(chosen, axis=1, keepdims=True)
    cnt_ref[...] = carry_ref[...]

    pad = [jnp.zeros((8 - TOP_K, tt), F32)]
    e_ref[...] = jnp.concatenate(e_rows + pad, axis=0).astype(jnp.int32)
    g_ref[...] = jnp.concatenate(g_rows + pad, axis=0)
    r_ref[...] = jnp.concatenate(r_rows + pad, axis=0).astype(jnp.int32)


def router(logits_t, bias, tt):
    n_exp, t = logits_t.shape
    col = pl.BlockSpec((8, tt), lambda i: (0, i))
    return pl.pallas_call(
        _router_kernel,
        grid=(t // tt,),
        in_specs=[pl.BlockSpec((n_exp, tt), lambda i: (0, i)),
                  pl.BlockSpec((n_exp, 1), lambda i: (0, 0))],
        out_specs=[col, col, col, pl.BlockSpec((n_exp, 1), lambda i: (0, 0))],
        out_shape=[jax.ShapeDtypeStruct((8, t), jnp.int32), jax.ShapeDtypeStruct((8, t), F32),
                   jax.ShapeDtypeStruct((8, t), jnp.int32), jax.ShapeDtypeStruct((n_exp, 1), F32)],
        scratch_shapes=[pltpu.VMEM((n_exp, 1), F32)],
        compiler_params=_cparams(1),
        name="router",
    )(logits_t, bias.reshape(n_exp, 1).astype(F32))


def rope_tables(n, rot_dim, tm):
    rows = n // GRID_W
    row = jnp.repeat(jnp.arange(rows, dtype=F32), GRID_W)
    col = jnp.tile(jnp.arange(GRID_W, dtype=F32), rows)
    axis_dim = rot_dim // 2
    inv = ROPE_THETA ** (-jnp.arange(0, axis_dim, 2, dtype=F32) / axis_dim)
    ar, ac = row[:, None] * inv, col[:, None] * inv
    zero = jnp.zeros_like(ar)
    c = jnp.concatenate([jnp.cos(ar), jnp.cos(ar), jnp.cos(ac), jnp.cos(ac)], axis=-1)
    s1 = jnp.concatenate([-jnp.sin(ar), zero, -jnp.sin(ac), zero], axis=-1)
    s2 = jnp.concatenate([zero, jnp.sin(ar), zero, jnp.sin(ac)], axis=-1)

    def finish(tab, fill):
        tab = jnp.pad(tab, ((0, 0), (0, LANES - rot_dim)), constant_values=fill)
        return jnp.concatenate([tab, jnp.full((tm, LANES), fill, F32)], axis=0)

    return finish(c, 1.0), finish(s1, 0.0), finish(s2, 0.0)


def rope_tables_paired(n, tm):
    rows = n // GRID_W
    row = jnp.repeat(jnp.arange(rows, dtype=F32), GRID_W)
    col = jnp.tile(jnp.arange(GRID_W, dtype=F32), rows)
    axis_dim = GQA_HEAD_DIM // 2
    inv = ROPE_THETA ** (-jnp.arange(0, axis_dim, 2, dtype=F32) / axis_dim)
    ar, ac = row[:, None] * inv, col[:, None] * inv
    c = jnp.concatenate([jnp.cos(ar), jnp.cos(ac), jnp.cos(ar), jnp.cos(ac)], axis=-1)
    s = jnp.concatenate([-jnp.sin(ar), -jnp.sin(ac), jnp.sin(ar), jnp.sin(ac)], axis=-1)
    ident = lambda fill: jnp.full((tm, GQA_HEAD_DIM), fill, F32)
    return jnp.concatenate([c, ident(1.0)], axis=0), jnp.concatenate([s, ident(0.0)], axis=0)


def pair_halves(a):
    lead = a.shape[:-1]
    a = a.reshape(*lead, -1, 2, 2, GQA_HEAD_DIM // 4)
    return jnp.swapaxes(a, -3, -2).reshape(*lead, -1)


def pack_w_in(w):
    hy = 3 * HY_W
    gq0 = hy + MLA_Q_RANK
    kv0 = gq0 + GQA_HEADS * GQA_HEAD_DIM
    ckv = w[:, kv0:kv0 + MLA_KV_RANK]
    kr = w[:, kv0 + MLA_KV_RANK:kv0 + MLA_KV_RANK + MLA_ROPE]
    gk0 = kv0 + MLA_KV_RANK + MLA_ROPE
    gv0 = gk0 + GQA_KV_HEADS * GQA_HEAD_DIM
    kr = jnp.pad(kr, ((0, 0), (0, LANES - MLA_ROPE)))
    return jnp.concatenate([w[:, :gq0], pair_halves(w[:, gq0:kv0]), ckv, pair_halves(w[:, gk0:gv0]),
                            w[:, gv0:], kr], axis=1).astype(BF16)


def pack_w_uq(w):
    w = w.reshape(MLA_Q_RANK, MLA_HEADS, MLA_NOPE + MLA_ROPE)
    w = jnp.pad(w, ((0, 0), (0, 0), (0, MLA_QK_PAD - MLA_NOPE - MLA_ROPE)))
    return w.reshape(MLA_Q_RANK, MLA_HEADS * MLA_QK_PAD).astype(BF16)


def pack_w_ukv(w):
    w = w.reshape(MLA_KV_RANK, MLA_HEADS, MLA_NOPE + MLA_V)
    k = w[:, :, :MLA_NOPE].reshape(MLA_KV_RANK, MLA_HEADS * MLA_NOPE)
    v = w[:, :, MLA_NOPE:].reshape(MLA_KV_RANK, MLA_HEADS * MLA_V)
    return jnp.concatenate([k, v], axis=1).astype(BF16)


def dispatch_plan(e_t, rank_t, counts):
    n_tok = e_t.shape[1]
    n_pairs = n_tok * TOP_K
    counts = counts.astype(jnp.int32)
    padded = (counts + MOE_BLOCK - 1) // MOE_BLOCK * MOE_BLOCK
    p_ends = jnp.cumsum(padded)
    p_starts = p_ends - padded
    n_blocks = -(-n_pairs // MOE_BLOCK) + N_EXPERTS
    n_rows = n_blocks * MOE_BLOCK
    blk = jnp.arange(n_blocks, dtype=jnp.int32)
    n_used = p_ends[-1] // MOE_BLOCK
    blk_src = jnp.minimum(blk, n_used - 1)
    blk_expert = jnp.sum((blk_src * MOE_BLOCK)[:, None] >= p_ends[None, :], axis=1).astype(jnp.int32)
    first = jnp.concatenate([jnp.ones((1,), bool), blk_expert[1:] != blk_expert[:-1]])
    blk_flag = jnp.where(blk < n_used, jnp.where(first, 2, 1), 0).astype(jnp.int32)

    experts = jnp.arange(N_EXPERTS, dtype=jnp.int32)
    used = counts > 0
    later = jnp.where(used[None, :] & (experts[None, :] > experts[:, None]), experts[None, :], N_EXPERTS)
    next_used = jnp.min(later, axis=1)
    next_used = jnp.where(next_used == N_EXPERTS, -1, next_used).astype(jnp.int32)
    slot_of = ((jnp.cumsum(used.astype(jnp.int32)) - 1) % 2).astype(jnp.int32)
    blk_onehot = blk_expert[:, None] == experts[None, :]
    blk_next = jnp.sum(jnp.where(blk_onehot, next_used[None, :], 0), axis=1).astype(jnp.int32)
    blk_slot = jnp.sum(jnp.where(blk_onehot, slot_of[None, :], 0), axis=1).astype(jnp.int32)

    onehot = e_t[:, :, None] == experts
    pos = jnp.sum(jnp.where(onehot, p_starts, 0), axis=-1) + rank_t

    stride = n_tok + 1
    tok = jnp.arange(n_tok, dtype=jnp.int32)
    real_keys = (e_t * stride + tok[None, :]).reshape(-1)
    cum_fill = jnp.cumsum(padded - counts)
    filler = jnp.arange(n_rows - n_pairs, dtype=jnp.int32)
    filler_exp = jnp.sum(filler[:, None] >= cum_fill[None, :], axis=1).astype(jnp.int32)
    filler_keys = filler_exp * stride + n_tok
    slot_tok = jnp.sort(jnp.concatenate([real_keys, filler_keys])) % stride
    spread = jnp.arange(n_rows, dtype=jnp.int32) % n_tok
    slot_tok = jnp.where(slot_tok == n_tok, spread, slot_tok)
    return (blk_expert, blk_src, blk_flag, blk_next, blk_slot), slot_tok, pos


def _pick_tile(*sizes):
    for tile in (512, 256, 128):
        if all(s % tile == 0 for s in sizes):
            return tile
    raise ValueError("row counts must be multiples of 128: %r" % (sizes,))


def kernel(x, c, ctx, c_ctx, ada_w, ada_b, norm_g, w_in, w_out, hy_conv, hy_w1, hy_b1, hy_w2, hy_b2,
           hy_w3, hy_freq, hy_skip, mla_q_norm, mla_kv_norm, mla_w_uq, mla_w_ukv, gqa_q_norm, gqa_k_norm,
           router_w, router_bias, exp_w_gate, exp_w_up, exp_w_down, sh_w_gate, sh_w_up, sh_w_down):
    bsz, n, d = x.shape
    n_ctx = ctx.shape[1]
    depth = ada_w.shape[0]
    assert d == D_MODEL and n % GRID_W == 0
    t_lat, t_ctx = bsz * n, bsz * n_ctx
    assert t_lat % n_ctx == 0
    tm = _pick_tile(n, t_ctx)
    tq = max(t for t in (2048, 1024, 512, 256) if n % t == 0)
    lat_tiles = t_lat // tm
    tiles_per_seq = n // tm
    tm_ffn = min(tm, 256)

    def mod_idx(tile):
        return lambda i: jnp.minimum(i // (n // tile), bsz)

    def rope_idx(i):
        return jnp.where(i < lat_tiles, i % tiles_per_seq, tiles_per_seq)

    x_parts = [x.reshape(t_lat, d), ctx.reshape(t_ctx, d)]
    mod_rows = -(-(bsz + 1) // 16) * 16
    s_in = jnp.concatenate([c, c_ctx[None], jnp.zeros((mod_rows - bsz - 1, d), F32)], axis=0)
    mod_all = adaln_all(s_in, ada_w, ada_b).reshape(depth, mod_rows, N_MOD, d)

    tabs_mla = rope_tables(n, MLA_ROPE, tm)
    tabs_gqa = rope_tables_paired(n, tm)
    mla_scale = (MLA_NOPE + MLA_ROPE) ** -0.5 * math.log2(math.e)
    gqa_scale = GQA_HEAD_DIM ** -0.5 * math.log2(math.e)
    dft ={m: (dft_tables(m, False), dft_tables(m, True)) for m in (n, n_ctx)}

    p = None
    for l in range(depth):
        last = l == depth - 1
        mod = mod_all[l]
        vec = lambda a: a.reshape(1, -1)

        if p is None:
            p = in_proj(x_parts, vec(norm_g[l, 0]), mod, pack_w_in(w_in[l]), tm, mod_idx(tm))
        q_m = mla_q(p, vec(mla_q_norm[l]), pack_w_uq(mla_w_uq[l]), tabs_mla, tm, rope_idx, mla_scale)
        k_m, v_m = mla_kv(p, vec(mla_kv_norm[l]), pack_w_ukv(mla_w_ukv[l]), tabs_mla, tm, rope_idx)
        q_g, k_g, v_g = gqa_qkv(p, vec(pair_halves(gqa_q_norm[l])), vec(pair_halves(gqa_k_norm[l])), tabs_gqa,
                                tm, rope_idx, gqa_scale)

        mla_args = dict(bsz=bsz, heads=MLA_HEADS, kv_group=1, dk=MLA_QK_PAD, dv=MLA_V)
        gqa_args = dict(bsz=bsz, heads=GQA_HEADS, kv_group=GQA_HEADS // GQA_KV_HEADS, dk=GQA_HEAD_DIM,
                        dv=GQA_HEAD_DIM)
        lat_q = dict(n_q=n, tq=tq, q_row0=0, kv_parts=[(t_lat, n_ctx), (0, n)])
        mo = attention(q_m, k_m, v_m, name="mla_attn", **lat_q, **mla_args)
        go = attention(q_g, k_g, v_g, name="gqa_attn", **lat_q, **gqa_args)

        filt = (hy_w1[l], hy_b1[l], hy_w2[l], hy_b2[l], hy_w3[l], hy_freq[l])

        def hyena(m, row0):
            lag_tabs, sym_tabs = dft[m]
            kr, ki = hyena_spectra(*hyena_time_filters(m, *filt), *lag_tabs)
            return hyena_mixer(p, hy_conv[l], hy_skip[l], kr, ki, *sym_tabs, bsz=bsz, n=m, row0=row0)

        mo, go, ho = [mo], [go], [hyena(n, 0)]

        if not last:
            ctx_q = dict(n_q=n_ctx, tq=n_ctx, q_row0=t_lat, kv_parts=[(t_lat, n_ctx)])
            mo.append(attention(q_m, k_m, v_m, name="mla_attn_ctx", **ctx_q, **mla_args))
            go.append(attention(q_g, k_g, v_g, name="gqa_attn_ctx", **ctx_q, **gqa_args))
            ho.append(hyena(n_ctx, t_lat))

        wo = w_out[l].astype(BF16)
        rw_t = router_w[l].T
        rw_hi = rw_t.astype(BF16)
        rw_lo = (rw_t - rw_hi.astype(F32)).astype(BF16)
        x_mid, f, logits_t = out_proj(ho, mo, go, wo[:HY_W], wo[HY_W:HY_W + MLA_HEADS * MLA_V],
                                      wo[HY_W + MLA_HEADS * MLA_V:], x_parts, vec(norm_g[l, 1]),
                                      vec(norm_g[l, 2]), mod, rw_hi, rw_lo, tm, mod_idx(tm))

        e_t, gate_t, rank_t, counts = router(logits_t, router_bias[l], tm)
        blocks, slot_tok, pos = dispatch_plan(e_t[:TOP_K], rank_t[:TOP_K], counts[:, 0])
        xs = f[slot_tok]
        y_shared = shared_expert(f, sh_w_gate[l].astype(BF16), sh_w_up[l].astype(BF16),
                                 sh_w_down[l].astype(BF16), tm)
        ys = moe_experts(*blocks, xs, exp_w_gate, exp_w_up, exp_w_down, l)
        next_proj = None if last else (vec(norm_g[l + 1, 0]), mod_all[l + 1], pack_w_in(w_in[l + 1]))
        outs = ffn_combine(y_shared, ys[pos], gate_t.T, x_mid, vec(norm_g[l, 3]), mod, tm_ffn,
                           mod_idx(tm_ffn), next_proj)
        x_parts, p = [outs[0]], (None if last else outs[1])

    return x_parts[0].reshape(bsz, n, d)
```

```python
import functools
import math

import jax
import jax.numpy as jnp
from jax import lax
from jax.experimental import pallas as pl
from jax.experimental.pallas import tpu as pltpu

F32 = jnp.float32
BF16 = jnp.bfloat16

D_MODEL = 2048
GRID_W = 64
EPS = 1e-6
N_MOD = 6
HY_W = D_MODEL // 4
HY_ORDER = 2
HY_BANDS = 16
HY_TARGET = 1e-2
HY_FAST = 0.3
HY_SLOW = 1.5
MLA_NOPE = 128
MLA_ROPE = 64
MLA_V = 128
MLA_HEADS = 6
MLA_Q_RANK = 768
MLA_KV_RANK = 256
GQA_HEAD_DIM = 128
GQA_HEADS = 6
GQA_KV_HEADS = 2
ROPE_THETA = 10000.0
N_EXPERTS = 64
TOP_K = 6
N_GROUPS = 8
TOPK_GROUPS = 4
EXPERT_HIDDEN = D_MODEL // 4
ROUTED_SCALE = 2.5

LANES = 128
VMEM_LIMIT_BYTES = 56 * 1024 * 1024

COL_HY = 0
COL_MQ = 3 * HY_W
COL_GQ = COL_MQ + MLA_Q_RANK
COL_CKV = COL_GQ + GQA_HEADS * GQA_HEAD_DIM
COL_GK = COL_CKV + MLA_KV_RANK
COL_GV = COL_GK + GQA_KV_HEADS * GQA_HEAD_DIM
COL_KR = COL_GV + GQA_KV_HEADS * GQA_HEAD_DIM
IN_COLS_PAD = COL_KR + LANES
MLA_QK_PAD = 2 * LANES
MOE_BLOCK = 256

NT_DIMS = (((1,), (1,)), ((), ()))


def _cparams(n_axes):
    return pltpu.CompilerParams(dimension_semantics=("arbitrary",) * n_axes,
                                vmem_limit_bytes=VMEM_LIMIT_BYTES)


def _resident(shape):
    nd = len(shape)
    return pl.BlockSpec(shape, lambda *_: (0,) * nd, pipeline_mode=pl.Buffered(1))


def _rms(x):
    return x * lax.rsqrt(jnp.mean(x * x, axis=-1, keepdims=True) + EPS)


def _silu(x):
    return x / (1.0 + jnp.exp(-x))


def _dot(a, b):
    return jnp.dot(a, b, preferred_element_type=F32)


def _dot_nt(a, b):
    return lax.dot_general(a, b, NT_DIMS, preferred_element_type=F32)


def _rope(x, c, s1, s2, shift):
    w = x.shape[-1]
    return x * c + pltpu.roll(x, w - shift, 1) * s1 + pltpu.roll(x, shift, 1) * s2


def _adaln_kernel(s_ref, w_ref, b_ref, o_ref):
    s = _silu(s_ref[...]).astype(BF16)
    o_ref[0] = _dot(s, w_ref[0].astype(BF16)) + b_ref[0]


def adaln_all(s_in, ada_w, ada_b):
    depth, d, n_out = ada_w.shape
    rows = s_in.shape[0]
    tn = 1024
    return pl.pallas_call(
        _adaln_kernel,
        grid=(depth, n_out // tn),
        in_specs=[pl.BlockSpec((rows, d), lambda l, j: (0, 0)),
                  pl.BlockSpec((1, d, tn), lambda l, j: (l, 0, j)),
                  pl.BlockSpec((1, 1, tn), lambda l, j: (l, 0, j))],
        out_specs=pl.BlockSpec((1, rows, tn), lambda l, j: (l, 0, j)),
        out_shape=jax.ShapeDtypeStruct((depth, rows, n_out), F32),
        compiler_params=_cparams(2),
        name="adaln",
    )(s_in, ada_w, ada_b.reshape(depth, 1, n_out))


def _part_tiles(parts, tm):
    return tuple(p.shape[0] // tm for p in parts)


def _part_specs(parts, tm):
    specs, start = [], 0
    for part in parts:
        tiles = part.shape[0] // tm
        specs.append(pl.BlockSpec((tm, part.shape[1]),
                                  lambda i, start=start, tiles=tiles: (jnp.clip(i - start, 0, tiles - 1), 0)))
        start += tiles
    return specs


def _read_part(refs, tiles):
    val, start = refs[0][...], tiles[0]
    for ref, n_tiles in zip(refs[1:], tiles[1:]):
        val = jnp.where(pl.program_id(0) >= start, ref[...], val)
        start += n_tiles
    return val


def _in_proj_kernel(*refs, x_tiles):
    x_refs = refs[:len(x_tiles)]
    g_ref, mod_ref, w_ref, o_ref = refs[len(x_tiles):]
    h = _rms(_read_part(x_refs, x_tiles)) * g_ref[...]
    h = h * (1.0 + mod_ref[0, 1:2, :]) + mod_ref[0, 0:1, :]
    o_ref[...] = _dot(h.astype(BF16), w_ref[...]).astype(o_ref.dtype)


def in_proj(x_parts, g, mod, w, tm, mod_idx):
    d, n_out = w.shape
    x_tiles = _part_tiles(x_parts, tm)
    t = sum(x_tiles) * tm
    return pl.pallas_call(
        functools.partial(_in_proj_kernel, x_tiles=x_tiles),
        grid=(t // tm,),
        in_specs=_part_specs(x_parts, tm) + [pl.BlockSpec((1, d), lambda i: (0, 0)),
                                             pl.BlockSpec((1, N_MOD, d), lambda i: (mod_idx(i), 0, 0)),
                                             _resident((d, n_out))],
        out_specs=pl.BlockSpec((tm, n_out), lambda i: (i, 0)),
        out_shape=jax.ShapeDtypeStruct((t, n_out), BF16),
        compiler_params=_cparams(1),
        name="in_proj",
    )(*x_parts, g, mod, w)


def _mla_q_kernel(cq_ref, qn_ref, w_ref, c_ref, s1_ref, s2_ref, o_ref, *, scale):
    hn = (_rms(cq_ref[...].astype(F32)) * qn_ref[...]).astype(BF16)
    q = _dot(hn, w_ref[...]) * scale
    c, s1, s2 = c_ref[...], s1_ref[...], s2_ref[...]
    for h in range(MLA_HEADS):
        lo = h * MLA_QK_PAD
        o_ref[:, lo:lo + MLA_NOPE] = q[:, lo:lo + MLA_NOPE].astype(BF16)
        r = q[:, lo + MLA_NOPE:lo + MLA_QK_PAD]
        o_ref[:, lo + MLA_NOPE:lo + MLA_QK_PAD] = _rope(r, c, s1, s2, MLA_ROPE // 4).astype(BF16)


def mla_q(p, q_norm, w_uq, tabs, tm, rope_idx, scale):
    t = p.shape[0]
    n_out = MLA_HEADS * MLA_QK_PAD
    tab_spec = pl.BlockSpec((tm, LANES), lambda i: (rope_idx(i), 0))
    return pl.pallas_call(
        functools.partial(_mla_q_kernel, scale=scale),
        grid=(t // tm,),
        in_specs=[pl.BlockSpec((tm, MLA_Q_RANK), lambda i: (i, COL_MQ // MLA_Q_RANK)),
                  pl.BlockSpec((1, MLA_Q_RANK), lambda i: (0, 0)),
                  _resident((MLA_Q_RANK, n_out)),
                  tab_spec, tab_spec, tab_spec],
        out_specs=pl.BlockSpec((tm, n_out), lambda i: (i, 0)),
        out_shape=jax.ShapeDtypeStruct((t, n_out), BF16),
        compiler_params=_cparams(1),
        name="mla_q",
    )(p, q_norm, w_uq, *tabs)


def _mla_kv_kernel(ckv_ref, kr_ref, kvn_ref, w_ref, c_ref, s1_ref, s2_ref, k_ref, v_ref):
    hn = (_rms(ckv_ref[...].astype(F32)) * kvn_ref[...]).astype(BF16)
    kv = _dot(hn, w_ref[...])
    kr = _rope(kr_ref[...].astype(F32), c_ref[...], s1_ref[...], s2_ref[...],
               MLA_ROPE // 4).astype(BF16)
    ones = jnp.ones((kv.shape[0], MLA_V), BF16)
    v0 = MLA_HEADS * MLA_NOPE
    for h in range(MLA_HEADS):
        lo = h * MLA_QK_PAD
        k_ref[:, lo:lo + MLA_NOPE] = kv[:, h * MLA_NOPE:(h + 1) * MLA_NOPE].astype(BF16)
        k_ref[:, lo + MLA_NOPE:lo + MLA_QK_PAD] = kr
        v_ref[:, 2 * h * MLA_V:(2 * h + 1) * MLA_V] = kv[:, v0 + h * MLA_V:v0 + (h + 1) * MLA_V].astype(BF16)
        v_ref[:, (2 * h + 1) * MLA_V:(2 * h + 2) * MLA_V] = ones


def mla_kv(p, kv_norm, w_ukv, tabs, tm, rope_idx):
    t = p.shape[0]
    nk = MLA_HEADS * MLA_QK_PAD
    nv = MLA_HEADS * 2 * MLA_V
    tab_spec = pl.BlockSpec((tm, LANES), lambda i: (rope_idx(i), 0))
    return pl.pallas_call(
        _mla_kv_kernel,
        grid=(t // tm,),
        in_specs=[pl.BlockSpec((tm, MLA_KV_RANK), lambda i: (i, COL_CKV // MLA_KV_RANK)),
                  pl.BlockSpec((tm, LANES), lambda i: (i, COL_KR // LANES)),
                  pl.BlockSpec((1, MLA_KV_RANK), lambda i: (0, 0)),
                  _resident((MLA_KV_RANK, MLA_HEADS * (MLA_NOPE + MLA_V))),
                  tab_spec, tab_spec, tab_spec],
        out_specs=[pl.BlockSpec((tm, nk), lambda i: (i, 0)),
                   pl.BlockSpec((tm, nv), lambda i: (i, 0))],
        out_shape=[jax.ShapeDtypeStruct((t, nk), BF16),
                   jax.ShapeDtypeStruct((t, nv), BF16)],
        compiler_params=_cparams(1),
        name="mla_kv",
    )(p, p, kv_norm, w_ukv, *tabs)


def _gqa_qkv_kernel(q_ref, k_ref, v_ref, qn_ref, kn_ref, c_ref, s_ref, qo_ref, ko_ref, vo_ref, *, scale):
    c, s = c_ref[...], s_ref[...]

    def rope(x):
        return x * c + pltpu.roll(x, GQA_HEAD_DIM // 2, 1) * s

    for h in range(GQA_HEADS):
        sl = slice(h * GQA_HEAD_DIM, (h + 1) * GQA_HEAD_DIM)
        x = _rms(q_ref[:, sl].astype(F32)) * qn_ref[...]
        qo_ref[:, sl] = (rope(x) * scale).astype(BF16)
    ones = jnp.ones((v_ref.shape[0], GQA_HEAD_DIM), BF16)
    for g in range(GQA_KV_HEADS):
        sl = slice(g * GQA_HEAD_DIM, (g + 1) * GQA_HEAD_DIM)
        x = _rms(k_ref[:, sl].astype(F32)) * kn_ref[...]
        ko_ref[:, sl] = rope(x).astype(BF16)
        vo_ref[:, 2 * g * GQA_HEAD_DIM:(2 * g + 1) * GQA_HEAD_DIM] = v_ref[:, sl]
        vo_ref[:, (2 * g + 1) * GQA_HEAD_DIM:(2 * g + 2) * GQA_HEAD_DIM] = ones


def gqa_qkv(p, q_norm, k_norm, tabs, tm, rope_idx, scale):
    t = p.shape[0]
    nq = GQA_HEADS * GQA_HEAD_DIM
    nk = GQA_KV_HEADS * GQA_HEAD_DIM
    tab_spec = pl.BlockSpec((tm, LANES), lambda i: (rope_idx(i), 0))
    return pl.pallas_call(
        functools.partial(_gqa_qkv_kernel, scale=scale),
        grid=(t // tm,),
        in_specs=[pl.BlockSpec((tm, nq), lambda i: (i, COL_GQ // nq)),
                  pl.BlockSpec((tm, nk), lambda i: (i, COL_GK // nk)),
                  pl.BlockSpec((tm, nk), lambda i: (i, COL_GV // nk)),
                  pl.BlockSpec((1, GQA_HEAD_DIM), lambda i: (0, 0)),
                  pl.BlockSpec((1, GQA_HEAD_DIM), lambda i: (0, 0)),
                  tab_spec, tab_spec],
        out_specs=[pl.BlockSpec((tm, nq), lambda i: (i, 0)),
                   pl.BlockSpec((tm, nk), lambda i: (i, 0)),
                   pl.BlockSpec((tm, 2 * nk), lambda i: (i, 0))],
        out_shape=[jax.ShapeDtypeStruct((t, nq), BF16),
                   jax.ShapeDtypeStruct((t, nk), BF16),
                   jax.ShapeDtypeStruct((t, 2 * nk), BF16)],
        compiler_params=_cparams(1),
        name="gqa_qkv",
    )(p, p, p, q_norm, k_norm, *tabs)


ATTN_SPLIT_ROWS = 256
ATTN_EXP_ROWS = 16


def _attn_kernel(*refs, n_parts):
    q_ref = refs[0]
    k_refs = refs[1:1 + n_parts]
    v_refs = refs[1 + n_parts:1 + 2 * n_parts]
    o_ref, s_scr, p_scr, m_scr = refs[1 + 2 * n_parts:]
    tq = q_ref.shape[0]
    dv = o_ref.shape[1]
    bounds = [0]
    for k in k_refs:
        bounds.append(bounds[-1] + k.shape[0])
    split = min(ATTN_SPLIT_ROWS, tq)
    groups = [slice(r, r + split) for r in range(0, tq, split)]

    for rs in groups:
        q = q_ref[rs, :]
        m = None
        for j, k in enumerate(k_refs):
            s = _dot_nt(q, k[...])
            s_scr[rs, bounds[j]:bounds[j + 1]] = s
            mj = jnp.max(s, axis=-1, keepdims=True)
            m = mj if m is None else jnp.maximum(m, mj)
        m_scr[rs, :] = m
    for r in range(0, tq, ATTN_EXP_ROWS):
        rows = slice(r, r + ATTN_EXP_ROWS)
        p_scr[rows, :] = jnp.exp2(s_scr[rows, :] - m_scr[rows, :]).astype(BF16)
    for rs in groups:
        o2 = functools.reduce(jnp.add, [_dot(p_scr[rs, bounds[j]:bounds[j + 1]], v[...])
                                        for j, v in enumerate(v_refs)])
        o_ref[rs, :] = (o2[:, :dv] / o2[:, dv:]).astype(o_ref.dtype)


def attention(q, k, v, *, bsz, n_q, tq, q_row0, kv_parts, heads, kv_group, dk, dv, name):
    assert tq % min(ATTN_SPLIT_ROWS, tq) == 0 and tq % ATTN_EXP_ROWS == 0
    nq_t = n_q // tq
    n_keys = sum(n_rows for _, n_rows in kv_parts)
    in_specs = [pl.BlockSpec((tq, dk), lambda b, h, i: (q_row0 // tq + b * nq_t + i, h))]
    for row0, n_rows in kv_parts:
        in_specs.append(pl.BlockSpec(
            (n_rows, dk), lambda b, h, i, row0=row0, n_rows=n_rows: (row0 // n_rows + b, h // kv_group)))
    for row0, n_rows in kv_parts:
        in_specs.append(pl.BlockSpec(
            (n_rows, 2 * dv), lambda b, h, i, row0=row0, n_rows=n_rows: (row0 // n_rows + b, h // kv_group)))
    n_parts = len(kv_parts)
    return pl.pallas_call(
        functools.partial(_attn_kernel, n_parts=n_parts),
        grid=(bsz, heads, nq_t),
        in_specs=in_specs,
        out_specs=pl.BlockSpec((tq, dv), lambda b, h, i: (b * nq_t + i, h)),
        out_shape=jax.ShapeDtypeStruct((bsz * n_q, heads * dv), BF16),
        scratch_shapes=[pltpu.VMEM((tq, n_keys), F32), pltpu.VMEM((tq, n_keys), BF16),
                        pltpu.VMEM((tq, 1), F32)],
        compiler_params=_cparams(3),
        name=name,
    )(q, *([k] * n_parts), *([v] * n_parts))


def _dft_kernel(ca_ref, sa_ref, cb_ref, sb_ref, cos_ref, sin_ref):
    cb, sb = cb_ref[...], sb_ref[...]
    for q in range(cos_ref.shape[1] // LANES):
        ca, sa = ca_ref[:, q:q + 1], sa_ref[:, q:q + 1]
        cols = slice(q * LANES, (q + 1) * LANES)
        cos_ref[:, cols] = (ca * cb - sa * sb).astype(BF16)
        sin_ref[:, cols] = (sa * cb + ca * sb).astype(BF16)


def dft_tables(n, half_shift):
    f = jnp.arange(n, dtype=jnp.int32)[:, None]
    q = jnp.arange(n // LANES, dtype=jnp.int32)[None, :]
    r = jnp.arange(LANES, dtype=jnp.int32)[None, :]
    if half_shift:
        period = 8 * n
        pa, pb = ((2 * f + 1) * (2 * LANES * q)) % period, ((2 * f + 1) * (2 * r + 1)) % period
    else:
        period = 4 * n
        pa, pb = ((2 * f + 1) * (LANES * q)) % period, ((2 * f + 1) * r) % period
    ang_a = pa.astype(F32) * (2.0 * math.pi / period)
    ang_b = pb.astype(F32) * (2.0 * math.pi / period)
    tr = min(n, 256)
    a_spec = pl.BlockSpec((tr, n // LANES), lambda i: (i, 0))
    b_spec = pl.BlockSpec((tr, LANES), lambda i: (i, 0))
    o_spec = pl.BlockSpec((tr, n), lambda i: (i, 0))
    return pl.pallas_call(
        _dft_kernel,
        grid=(n // tr,),
        in_specs=[a_spec, a_spec, b_spec, b_spec],
        out_specs=[o_spec, o_spec],
        out_shape=[jax.ShapeDtypeStruct((n, n), BF16)] * 2,
        compiler_params=_cparams(1),
        name="dft_tables",
    )(jnp.cos(ang_a), jnp.sin(ang_a), jnp.cos(ang_b), jnp.sin(ang_b))


def hyena_time_filters(n, w1, b1, w2, b2, w3, freq):
    t = jnp.linspace(0.0, 1.0, n, dtype=F32)[:, None]
    ang = 2.0 * math.pi * jnp.arange(n, dtype=F32)[:, None] / n
    bands = jnp.linspace(1e-4, HY_BANDS - 1, HY_BANDS, dtype=F32)
    feats = jnp.concatenate([t, jnp.cos(ang * bands), jnp.sin(ang * bands)], axis=-1)
    fr = freq.astype(F32)
    hp = lax.Precision.HIGHEST
    hdn = jnp.sin(fr * (jnp.dot(feats, w1, precision=hp) + b1))
    hdn = jnp.sin(fr * (jnp.dot(hdn, w2, precision=hp) + b2))
    h = jnp.dot(hdn, w3, precision=hp).reshape(n, HY_ORDER, 2, HY_W)
    deltas = jnp.abs(jnp.linspace(math.log(HY_TARGET) / HY_SLOW, math.log(HY_TARGET) / HY_FAST,
                                  HY_W, dtype=F32))
    h = h * jnp.exp(-t * deltas)[:, None, None, :]
    h_fwd = h[:, :, 0]
    h_bwd = h[:, :, 1].at[0].set(0.0)
    r = lax.rsqrt(jnp.sum(h_fwd * h_fwd, axis=0) + jnp.sum(h_bwd * h_bwd, axis=0) + EPS)
    kf = (h_fwd * r).reshape(n, HY_ORDER * HY_W)
    kb = (h_bwd * r).reshape(n, HY_ORDER * HY_W)
    return kf + kb, kb - kf


HY_CHUNK = 256


def _spectra_kernel(ks_ref, kd_ref, c_ref, s_ref, kr_ref, ki_ref, *, inv_n):
    kr_ref[...] = _dot(c_ref[...], ks_ref[...].astype(BF16)) * inv_n
    ki_ref[...] = _dot(s_ref[...], kd_ref[...].astype(BF16)) * inv_n


def hyena_spectra(ksum, kdiff, cmat, smat):
    n, cols = ksum.shape
    spec = pl.BlockSpec((n, HY_CHUNK), lambda i: (0, i))
    return pl.pallas_call(
        functools.partial(_spectra_kernel, inv_n=1.0 / n),
        grid=(cols // HY_CHUNK,),
        in_specs=[spec, spec, _resident((n, n)), _resident((n, n))],
        out_specs=[spec, spec],
        out_shape=[jax.ShapeDtypeStruct((n, cols), F32)] * 2,
        compiler_params=_cparams(1),
        name="hyena_spectra",
    )(ksum, kdiff, cmat, smat)


def _hyena_kernel(v_ref, x1_ref, x2_ref, wv_ref, w1_ref, w2_ref, skip_ref,
                  k1r_ref, k1i_ref, k2r_ref, k2i_ref, c_ref, s_ref, o_ref, zb_scr, y_scr, *, n, fc):
    lane_groups = [slice(h * LANES, (h + 1) * LANES) for h in range(o_ref.shape[1] // LANES)]

    def short_conv(u_ref, w_ref, cs):
        u = u_ref[:, cs].astype(F32)
        pos = lax.broadcasted_iota(jnp.int32, u.shape, 0)
        prev = jnp.where(pos == 0, 0.0, pltpu.roll(u, 1, 0))
        nxt = jnp.where(pos == n - 1, 0.0, pltpu.roll(u, n - 1, 0))
        return prev * w_ref[0:1, cs] + u * w_ref[1:2, cs] + nxt * w_ref[2:3, cs]

    def long_conv(kr_ref, ki_ref):
        zb = zb_scr[...]
        for j in range(n // fc):
            sl = slice(j * fc, (j + 1) * fc)
            a = _dot(c_ref[sl, :], zb)
            b = _dot(s_ref[sl, :], zb)
            kr = kr_ref[sl, :]
            ki = ki_ref[sl, :]
            yr = (a * kr + b * ki).astype(BF16)
            yi = (a * ki - b * kr).astype(BF16)
            y_scr[...] += _dot(c_ref[:, sl], yr) - _dot(s_ref[:, sl], yi)

    for cs in lane_groups:
        v = short_conv(v_ref, wv_ref, cs)
        zb_scr[:, cs] = v.astype(BF16)
        y_scr[:, cs] = v * skip_ref[0:1, cs]
    long_conv(k1r_ref, k1i_ref)
    for cs in lane_groups:
        z = short_conv(x1_ref, w1_ref, cs) * y_scr[:, cs]
        zb_scr[:, cs] = z.astype(BF16)
        y_scr[:, cs] = z * skip_ref[1:2, cs]
    long_conv(k2r_ref, k2i_ref)
    for cs in lane_groups:
        o_ref[:, cs] = (short_conv(x2_ref, w2_ref, cs) * y_scr[:, cs]).astype(o_ref.dtype)


def hyena_mixer(p, conv_w, skip, kr, ki, cmat, smat, *, bsz, n, row0):
    ch = HY_CHUNK
    nc = HY_W // ch
    fc = min(512, n)
    blk0 = row0 // n

    def u_spec(k):
        return pl.BlockSpec((n, ch), lambda c, b: (blk0 + b, k * nc + c))

    def w_spec(k):
        return pl.BlockSpec((3, ch), lambda c, b: (0, k * nc + c))

    def k_spec(order):
        return pl.BlockSpec((n, ch), lambda c, b: (0, order * nc + c), pipeline_mode=pl.Buffered(1))

    return pl.pallas_call(
        functools.partial(_hyena_kernel, n=n, fc=fc),
        grid=(nc, bsz),
        in_specs=[u_spec(0), u_spec(1), u_spec(2), w_spec(0), w_spec(1), w_spec(2),
                  pl.BlockSpec((HY_ORDER, ch), lambda c, b: (0, c)),
                  k_spec(0), k_spec(0), k_spec(1), k_spec(1),
                  _resident((n, n)), _resident((n, n))],
        out_specs=pl.BlockSpec((n, ch), lambda c, b: (b, c)),
        out_shape=jax.ShapeDtypeStruct((bsz * n, HY_W), BF16),
        scratch_shapes=[pltpu.VMEM((n, ch), BF16), pltpu.VMEM((n, ch), F32)],
        compiler_params=_cparams(2),
        name="hyena_n%d" % n,
    )(p, p, p, conv_w, conv_w, conv_w, skip, kr, ki, kr, ki, cmat, smat)


def _out_proj_kernel(*refs, tiles):
    refs = list(refs)
    take = lambda n: [refs.pop(0) for _ in range(n)]
    ho_refs, mo_refs, go_refs = take(len(tiles[0])), take(len(tiles[1])), take(len(tiles[2]))
    wh_ref, wm_ref, wg_ref = take(3)
    x_refs = take(len(tiles[3]))
    g1_ref, g2_ref, mod_ref, rwh_ref, rwl_ref, xo_ref, f_ref, lg_ref = refs
    mix = (_dot(_read_part(ho_refs, tiles[0]), wh_ref[...]) + _dot(_read_part(mo_refs, tiles[1]), wm_ref[...])
           + _dot(_read_part(go_refs, tiles[2]), wg_ref[...]))
    xn = _read_part(x_refs, tiles[3]) + mod_ref[0, 2:3, :] * (_rms(mix) * g1_ref[...])
    xo_ref[...] = xn
    f = (_rms(xn) * g2_ref[...]) * (1.0 + mod_ref[0, 4:5, :]) + mod_ref[0, 3:4, :]
    fh = f.astype(BF16)
    fl = (f - fh.astype(F32)).astype(BF16)
    f_ref[...] = fh
    lg_ref[...] = _dot_nt(rwh_ref[...], fh) + _dot_nt(rwh_ref[...], fl) + _dot_nt(rwl_ref[...], fh)


def out_proj(ho_parts, mo_parts, go_parts, wh, wm, wg, x_parts, g1, g2, mod, rwh, rwl, tm, mod_idx):
    tiles = tuple(_part_tiles(parts, tm) for parts in (ho_parts, mo_parts, go_parts, x_parts))
    t = sum(tiles[0]) * tm
    d = D_MODEL
    row = lambda w: pl.BlockSpec((tm, w), lambda i: (i, 0))
    vec = pl.BlockSpec((1, d), lambda i: (0, 0))
    return pl.pallas_call(
        functools.partial(_out_proj_kernel, tiles=tiles),
        grid=(t // tm,),
        in_specs=(_part_specs(ho_parts, tm) + _part_specs(mo_parts, tm) + _part_specs(go_parts, tm)
                  + [_resident(wh.shape), _resident(wm.shape), _resident(wg.shape)]
                  + _part_specs(x_parts, tm)
                  + [vec, vec, pl.BlockSpec((1, N_MOD, d), lambda i: (mod_idx(i), 0, 0)),
                     _resident(rwh.shape), _resident(rwl.shape)]),
        out_specs=[row(d), row(d), pl.BlockSpec((N_EXPERTS, tm), lambda i: (0, i))],
        out_shape=[jax.ShapeDtypeStruct((t, d), F32),
                   jax.ShapeDtypeStruct((t, d), BF16),
                   jax.ShapeDtypeStruct((N_EXPERTS, t), F32)],
        compiler_params=_cparams(1),
        name="out_proj",
    )(*ho_parts, *mo_parts, *go_parts, wh, wm, wg, *x_parts, g1, g2, mod, rwh, rwl)


def _moe_kernel(be_ref, bsrc_ref, bflag_ref, bnext_ref, bslot_ref, x_ref, wg_hbm, wu_hbm, wd_hbm, after_hbm,
                o_ref, wg_f, wu_f, wd_f, wg_s, wu_s, wd_s, sem, *, layer):
    i = pl.program_id(0)

    def weight_copies(expert, slot):
        pairs = ((wg_hbm, wg_f), (wu_hbm, wu_f), (wd_hbm, wd_f))
        return [pltpu.make_async_copy(src.at[layer, expert], dst.at[slot], sem.at[k, slot])
                for k, (src, dst) in enumerate(pairs)]

    @pl.when(bflag_ref[i] == 2)
    def _():
        expert, slot, nxt = be_ref[i], bslot_ref[i], bnext_ref[i]

        @pl.when(i == 0)
        def _():
            for cp in weight_copies(expert, slot):
                cp.start()

        for cp in weight_copies(expert, slot):
            cp.wait()

        @pl.when(nxt >= 0)
        def _():
            for cp in weight_copies(nxt, 1 - slot):
                cp.start()

        wg_s[...] = wg_f[slot].astype(BF16)
        wu_s[...] = wu_f[slot].astype(BF16)
        wd_s[...] = wd_f[slot].astype(BF16)

    @pl.when(bflag_ref[i] != 0)
    def _():
        x = x_ref[...]
        a = _silu(_dot(x, wg_s[...])) * _dot(x, wu_s[...])
        o_ref[...] = _dot(a.astype(BF16), wd_s[...]).astype(o_ref.dtype)

    @pl.when(bflag_ref[i] == 0)
    def _():
        o_ref[...] = jnp.zeros(o_ref.shape, o_ref.dtype)


def moe_experts(blk_expert, blk_src, blk_flag, blk_next, blk_slot, xs, wg, wu, wd, layer, after):
    n_rows, d = xs.shape
    hid = wg.shape[3]
    hbm = pl.BlockSpec(memory_space=pl.ANY)
    grid_spec = pltpu.PrefetchScalarGridSpec(
        num_scalar_prefetch=5,
        grid=(n_rows // MOE_BLOCK,),
        in_specs=[pl.BlockSpec((MOE_BLOCK, d), lambda i, be, bs, bf, bn, bl: (bs[i], 0)), hbm, hbm, hbm, hbm],
        out_specs=pl.BlockSpec((MOE_BLOCK, d), lambda i, be, bs, bf, bn, bl: (i, 0)),
        scratch_shapes=[pltpu.VMEM((2, d, hid), F32), pltpu.VMEM((2, d, hid), F32),
                        pltpu.VMEM((2, hid, d), F32),
                        pltpu.VMEM((d, hid), BF16), pltpu.VMEM((d, hid), BF16),
                        pltpu.VMEM((hid, d), BF16),
                        pltpu.SemaphoreType.DMA((3, 2))],
    )
    return pl.pallas_call(
        functools.partial(_moe_kernel, layer=layer),
        grid_spec=grid_spec,
        out_shape=jax.ShapeDtypeStruct((n_rows, d), BF16),
        compiler_params=_cparams(1),
        cost_estimate=pl.CostEstimate(flops=6 * n_rows * d * hid, transcendentals=n_rows * hid,
                                      bytes_accessed=4 * n_rows * d + 12 * N_EXPERTS * d * hid),
        name="moe_experts",
    )(blk_expert, blk_src, blk_flag, blk_next, blk_slot, xs, wg, wu, wd, after)


def _shared_expert_kernel(f_ref, sg_ref, su_ref, sd_ref, o_ref):
    f = f_ref[...]
    a = _silu(_dot(f, sg_ref[...])) * _dot(f, su_ref[...])
    o_ref[...] = _dot(a.astype(BF16), sd_ref[...]).astype(o_ref.dtype)


def shared_expert(f, sg, su, sd, tm):
    t, d = f.shape
    row = pl.BlockSpec((tm, d), lambda i: (i, 0))
    return pl.pallas_call(
        _shared_expert_kernel,
        grid=(t // tm,),
        in_specs=[row, _resident(sg.shape), _resident(su.shape), _resident(sd.shape)],
        out_specs=row,
        out_shape=jax.ShapeDtypeStruct((t, d), BF16),
        compiler_params=_cparams(1),
        cost_estimate=pl.CostEstimate(flops=6 * t * d * sg.shape[1], transcendentals=t * sg.shape[1],
                                      bytes_accessed=4 * t * d + 6 * d * sg.shape[1]),
        name="shared_expert",
    )(f, sg, su, sd)


def _ffn_combine_kernel(ys_ref, yg_ref, gate_ref, x_ref, g3_ref, mod_ref, *rest, project):
    y = ys_ref[...].astype(F32)
    gates = gate_ref[...]
    for k in range(TOP_K):
        y = y + gates[:, k:k + 1] * yg_ref[k].astype(F32)
    x_new = x_ref[...] + mod_ref[0, 5:6, :] * (_rms(y) * g3_ref[...])
    if not project:
        (o_ref,) = rest
        o_ref[...] = x_new
        return
    gn_ref, modn_ref, w_ref, o_ref, p_ref = rest
    o_ref[...] = x_new
    h = _rms(x_new) * gn_ref[...]
    h = h * (1.0 + modn_ref[0, 1:2, :]) + modn_ref[0, 0:1, :]
    p_ref[...] = _dot(h.astype(BF16), w_ref[...]).astype(p_ref.dtype)


def ffn_combine(y_shared, y_gathered, gates, x_all, g3, mod, tm, mod_idx, next_proj=None):
    t, d = y_shared.shape
    row = pl.BlockSpec((tm, d), lambda i: (i, 0))
    vec = pl.BlockSpec((1, d), lambda i: (0, 0))
    mod_spec = pl.BlockSpec((1, N_MOD, d), lambda i: (mod_idx(i), 0, 0))
    in_specs = [row, pl.BlockSpec((TOP_K, tm, d), lambda i: (0, i, 0)),
                pl.BlockSpec((tm, gates.shape[1]), lambda i: (i, 0)), row, vec, mod_spec]
    out_specs, out_shape, extra = [row], [jax.ShapeDtypeStruct((t, d), F32)], ()
    if next_proj is not None:
        n_out = next_proj[2].shape[1]
        in_specs += [vec, mod_spec, _resident((d, n_out))]
        out_specs.append(pl.BlockSpec((tm, n_out), lambda i: (i, 0)))
        out_shape.append(jax.ShapeDtypeStruct((t, n_out), BF16))
        extra = tuple(next_proj)
    return pl.pallas_call(
        functools.partial(_ffn_combine_kernel, project=next_proj is not None),
        grid=(t // tm,),
        in_specs=in_specs,
        out_specs=out_specs,
        out_shape=out_shape,
        compiler_params=_cparams(1),
        name="ffn_combine",
    )(y_shared, y_gathered, gates, x_all, g3, mod, *extra)


def _first_max(vals, idx, n):
    m = jnp.max(vals, axis=0, keepdims=True)
    return m, jnp.min(jnp.where(vals == m, idx, float(n)), axis=0, keepdims=True)


def _router_kernel(lg_ref, bias_ref, e_ref, g_ref, r_ref, cnt_ref, carry_ref):
    @pl.when(pl.program_id(0) == 0)
    def _():
        carry_ref[...] = jnp.zeros(carry_ref.shape, F32)

    lg = lg_ref[...]
    tt = lg.shape[1]
    gsz = N_EXPERTS // N_GROUPS
    neg = -jnp.inf
    scores = 1.0 / (1.0 + jnp.exp(-lg))
    biased = scores + bias_ref[...]
    sub = lax.broadcasted_iota(jnp.int32, (gsz, tt), 0).astype(F32)

    grp_rows = []
    for g in range(N_GROUPS):
        blk = biased[g * gsz:(g + 1) * gsz, :]
        m1, i1 = _first_max(blk, sub, gsz)
        m2 = jnp.max(jnp.where(sub == i1, neg, blk), axis=0, keepdims=True)
        grp_rows.append(m1 + m2)
    cur = jnp.concatenate(grp_rows, axis=0)
    gidx = lax.broadcasted_iota(jnp.int32, (N_GROUPS, tt), 0).astype(F32)
    gsel = jnp.zeros((N_GROUPS, tt), F32)
    for _ in range(TOPK_GROUPS):
        _, gi = _first_max(cur, gidx, N_GROUPS)
        hit = gidx == gi
        gsel = jnp.where(hit, 1.0, gsel)
        cur = jnp.where(hit, neg, cur)
    emask = jnp.concatenate([jnp.broadcast_to(gsel[g:g + 1, :], (gsz, tt)) for g in range(N_GROUPS)], axis=0)

    cand = jnp.where(emask > 0.0, biased, neg)
    eidx = lax.broadcasted_iota(jnp.int32, (N_EXPERTS, tt), 0).astype(F32)
    chosen = jnp.zeros((N_EXPERTS, tt), F32)
    e_rows, g_rows = [], []
    for _ in range(TOP_K):
        _, ei = _first_max(cand, eidx, N_EXPERTS)
        hit = eidx == ei
        e_rows.append(ei)
        g_rows.append(jnp.sum(jnp.where(hit, scores, 0.0), axis=0, keepdims=True))
        chosen = jnp.where(hit, 1.0, chosen)
        cand = jnp.where(hit, neg, cand)
    gsum = functools.reduce(jnp.add, g_rows)
    g_rows = [g / gsum * ROUTED_SCALE for g in g_rows]

    before = (lax.broadcasted_iota(jnp.int32, (tt, tt), 0) < lax.broadcasted_iota(jnp.int32, (tt, tt), 1))
    prefix = _dot(chosen.astype(BF16), jnp.where(before, 1.0, 0.0).astype(BF16))
    rank_all = prefix + carry_ref[...]
    r_rows = [jnp.sum(jnp.where(eidx == ei, rank_all, 0.0), axis=0, keepdims=True) for ei in e_rows]
    carry_ref[...] = carry_ref[...] + jnp.sum(chosen, axis=1, keepdims=True)
    cnt_ref[...] = carry_ref[...]

    pad = [jnp.zeros((8 - TOP_K, tt), F32)]
    e_ref[...] = jnp.concatenate(e_rows + pad, axis=0).astype(jnp.int32)
    g_ref[...] = jnp.concatenate(g_rows + pad, axis=0)
    r_ref[...] = jnp.concatenate(r_rows + pad, axis=0).astype(jnp.int32)


def router(logits_t, bias, tt):
    n_exp, t = logits_t.shape
    col = pl.BlockSpec((8, tt), lambda i: (0, i))
    return pl.pallas_call(
        _router_kernel,
        grid=(t // tt,),
        in_specs=[pl.BlockSpec((n_exp, tt), lambda i: (0, i)),
                  pl.BlockSpec((n_exp, 1), lambda i: (0, 0))],
        out_specs=[col, col, col, pl.BlockSpec((n_exp, 1), lambda i: (0, 0))],
        out_shape=[jax.ShapeDtypeStruct((8, t), jnp.int32), jax.ShapeDtypeStruct((8, t), F32),
                   jax.ShapeDtypeStruct((8, t), jnp.int32), jax.ShapeDtypeStruct((n_exp, 1), F32)],
        scratch_shapes=[pltpu.VMEM((n_exp, 1), F32)],
        compiler_params=_cparams(1),
        name="router",
    )(logits_t, bias.reshape(n_exp, 1).astype(F32))


def rope_tables(n, rot_dim, tm):
    rows = n // GRID_W
    row = jnp.repeat(jnp.arange(rows, dtype=F32), GRID_W)
    col = jnp.tile(jnp.arange(GRID_W, dtype=F32), rows)
    axis_dim = rot_dim // 2
    inv = ROPE_THETA ** (-jnp.arange(0, axis_dim, 2, dtype=F32) / axis_dim)
    ar, ac = row[:, None] * inv, col[:, None] * inv
    zero = jnp.zeros_like(ar)
    c = jnp.concatenate([jnp.cos(ar), jnp.cos(ar), jnp.cos(ac), jnp.cos(ac)], axis=-1)
    s1 = jnp.concatenate([-jnp.sin(ar), zero, -jnp.sin(ac), zero], axis=-1)
    s2 = jnp.concatenate([zero, jnp.sin(ar), zero, jnp.sin(ac)], axis=-1)

    def finish(tab, fill):
        tab = jnp.pad(tab, ((0, 0), (0, LANES - rot_dim)), constant_values=fill)
        return jnp.concatenate([tab, jnp.full((tm, LANES), fill, F32)], axis=0)

    return finish(c, 1.0), finish(s1, 0.0), finish(s2, 0.0)


def rope_tables_paired(n, tm):
    rows = n // GRID_W
    row = jnp.repeat(jnp.arange(rows, dtype=F32), GRID_W)
    col = jnp.tile(jnp.arange(GRID_W, dtype=F32), rows)
    axis_dim = GQA_HEAD_DIM // 2
    inv = ROPE_THETA ** (-jnp.arange(0, axis_dim, 2, dtype=F32) / axis_dim)
    ar, ac = row[:, None] * inv, col[:, None] * inv
    c = jnp.concatenate([jnp.cos(ar), jnp.cos(ac), jnp.cos(ar), jnp.cos(ac)], axis=-1)
    s = jnp.concatenate([-jnp.sin(ar), -jnp.sin(ac), jnp.sin(ar), jnp.sin(ac)], axis=-1)
    ident = lambda fill: jnp.full((tm, GQA_HEAD_DIM), fill, F32)
    return jnp.concatenate([c, ident(1.0)], axis=0), jnp.concatenate([s, ident(0.0)], axis=0)


def pair_halves(a):
    lead = a.shape[:-1]
    a = a.reshape(*lead, -1, 2, 2, GQA_HEAD_DIM // 4)
    return jnp.swapaxes(a, -3, -2).reshape(*lead, -1)


def pack_w_in(w):
    hy = 3 * HY_W
    gq0 = hy + MLA_Q_RANK
    kv0 = gq0 + GQA_HEADS * GQA_HEAD_DIM
    ckv = w[:, kv0:kv0 + MLA_KV_RANK]
    kr = w[:, kv0 + MLA_KV_RANK:kv0 + MLA_KV_RANK + MLA_ROPE]
    gk0 = kv0 + MLA_KV_RANK + MLA_ROPE
    gv0 = gk0 + GQA_KV_HEADS * GQA_HEAD_DIM
    kr = jnp.pad(kr, ((0, 0), (0, LANES - MLA_ROPE)))
    return jnp.concatenate([w[:, :gq0], pair_halves(w[:, gq0:kv0]), ckv, pair_halves(w[:, gk0:gv0]),
                            w[:, gv0:], kr], axis=1).astype(BF16)


def pack_w_uq(w):
    w = w.reshape(MLA_Q_RANK, MLA_HEADS, MLA_NOPE + MLA_ROPE)
    w = jnp.pad(w, ((0, 0), (0, 0), (0, MLA_QK_PAD - MLA_NOPE - MLA_ROPE)))
    return w.reshape(MLA_Q_RANK, MLA_HEADS * MLA_QK_PAD).astype(BF16)


def pack_w_ukv(w):
    w = w.reshape(MLA_KV_RANK, MLA_HEADS, MLA_NOPE + MLA_V)
    k = w[:, :, :MLA_NOPE].reshape(MLA_KV_RANK, MLA_HEADS * MLA_NOPE)
    v = w[:, :, MLA_NOPE:].reshape(MLA_KV_RANK, MLA_HEADS * MLA_V)
    return jnp.concatenate([k, v], axis=1).astype(BF16)


def dispatch_plan(e_t, rank_t, counts):
    n_tok = e_t.shape[1]
    n_pairs = n_tok * TOP_K
    counts = counts.astype(jnp.int32)
    padded = (counts + MOE_BLOCK - 1) // MOE_BLOCK * MOE_BLOCK
    p_ends = jnp.cumsum(padded)
    p_starts = p_ends - padded
    n_blocks = -(-n_pairs // MOE_BLOCK) + N_EXPERTS
    n_rows = n_blocks * MOE_BLOCK
    blk = jnp.arange(n_blocks, dtype=jnp.int32)
    n_used = p_ends[-1] // MOE_BLOCK
    blk_src = jnp.minimum(blk, n_used - 1)
    blk_expert = jnp.sum((blk_src * MOE_BLOCK)[:, None] >= p_ends[None, :], axis=1).astype(jnp.int32)
    first = jnp.concatenate([jnp.ones((1,), bool), blk_expert[1:] != blk_expert[:-1]])
    blk_flag = jnp.where(blk < n_used, jnp.where(first, 2, 1), 0).astype(jnp.int32)

    experts = jnp.arange(N_EXPERTS, dtype=jnp.int32)
    used = counts > 0
    later = jnp.where(used[None, :] & (experts[None, :] > experts[:, None]), experts[None, :], N_EXPERTS)
    next_used = jnp.min(later, axis=1)
    next_used = jnp.where(next_used == N_EXPERTS, -1, next_used).astype(jnp.int32)
    slot_of = ((jnp.cumsum(used.astype(jnp.int32)) - 1) % 2).astype(jnp.int32)
    blk_onehot = blk_expert[:, None] == experts[None, :]
    blk_next = jnp.sum(jnp.where(blk_onehot, next_used[None, :], 0), axis=1).astype(jnp.int32)
    blk_slot = jnp.sum(jnp.where(blk_onehot, slot_of[None, :], 0), axis=1).astype(jnp.int32)

    onehot = e_t[:, :, None] == experts
    pos = jnp.sum(jnp.where(onehot, p_starts, 0), axis=-1) + rank_t

    stride = n_tok + 1
    tok = jnp.arange(n_tok, dtype=jnp.int32)
    real_keys = (e_t * stride + tok[None, :]).reshape(-1)
    cum_fill = jnp.cumsum(padded - counts)
    filler = jnp.arange(n_rows - n_pairs, dtype=jnp.int32)
    filler_exp = jnp.sum(filler[:, None] >= cum_fill[None, :], axis=1).astype(jnp.int32)
    filler_keys = filler_exp * stride + n_tok
    slot_tok = jnp.sort(jnp.concatenate([real_keys, filler_keys])) % stride
    spread = jnp.arange(n_rows, dtype=jnp.int32) % n_tok
    slot_tok = jnp.where(slot_tok == n_tok, spread, slot_tok)
    return (blk_expert, blk_src, blk_flag, blk_next, blk_slot), slot_tok, pos


def _pick_tile(*sizes):
    for tile in (512, 256, 128):
        if all(s % tile == 0 for s in sizes):
            return tile
    raise ValueError("row counts must be multiples of 128: %r" % (sizes,))


def kernel(x, c, ctx, c_ctx, ada_w, ada_b, norm_g, w_in, w_out, hy_conv, hy_w1, hy_b1, hy_w2, hy_b2,
           hy_w3, hy_freq, hy_skip, mla_q_norm, mla_kv_norm, mla_w_uq, mla_w_ukv, gqa_q_norm, gqa_k_norm,
           router_w, router_bias, exp_w_gate, exp_w_up, exp_w_down, sh_w_gate, sh_w_up, sh_w_down):
    bsz, n, d = x.shape
    n_ctx = ctx.shape[1]
    depth = ada_w.shape[0]
    assert d == D_MODEL and n % GRID_W == 0
    t_lat, t_ctx = bsz * n, bsz * n_ctx
    assert t_lat % n_ctx == 0
    tm = _pick_tile(n, t_ctx)
    tq = max(t for t in (2048, 1024, 512, 256) if n % t == 0)
    lat_tiles = t_lat // tm
    tiles_per_seq = n // tm
    tm_ffn = min(tm, 256)

    def mod_idx(tile):
        return lambda i: jnp.minimum(i // (n // tile), bsz)

    def rope_idx(i):
        return jnp.where(i < lat_tiles, i % tiles_per_seq, tiles_per_seq)

    x_parts = [x.reshape(t_lat, d), ctx.reshape(t_ctx, d)]
    mod_rows = -(-(bsz + 1) // 16) * 16
    s_in = jnp.concatenate([c, c_ctx[None], jnp.zeros((mod_rows - bsz - 1, d), F32)], axis=0)
    mod_all = adaln_all(s_in, ada_w, ada_b).reshape(depth, mod_rows, N_MOD, d)

    tabs_mla = rope_tables(n, MLA_ROPE, tm)
    tabs_gqa = rope_tables_paired(n, tm)
    mla_scale = (MLA_NOPE + MLA_ROPE) ** -0.5 * math.log2(math.e)
    gqa_scale = GQA_HEAD_DIM ** -0.5 * math.log2(math.e)
    dft ={m: (dft_tables(m, False), dft_tables(m, True)) for m in (n, n_ctx)}

    p = None
    for l in range(depth):
        last = l == depth - 1
        mod = mod_all[l]
        vec = lambda a: a.reshape(1, -1)

        if p is None:
            p = in_proj(x_parts, vec(norm_g[l, 0]), mod, pack_w_in(w_in[l]), tm, mod_idx(tm))
        q_m = mla_q(p, vec(mla_q_norm[l]), pack_w_uq(mla_w_uq[l]), tabs_mla, tm, rope_idx, mla_scale)
        k_m, v_m = mla_kv(p, vec(mla_kv_norm[l]), pack_w_ukv(mla_w_ukv[l]), tabs_mla, tm, rope_idx)
        q_g, k_g, v_g = gqa_qkv(p, vec(pair_halves(gqa_q_norm[l])), vec(pair_halves(gqa_k_norm[l])), tabs_gqa,
                                tm, rope_idx, gqa_scale)

        mla_args = dict(bsz=bsz, heads=MLA_HEADS, kv_group=1, dk=MLA_QK_PAD, dv=MLA_V)
        gqa_args = dict(bsz=bsz, heads=GQA_HEADS, kv_group=GQA_HEADS // GQA_KV_HEADS, dk=GQA_HEAD_DIM,
                        dv=GQA_HEAD_DIM)
        lat_q = dict(n_q=n, tq=tq, q_row0=0, kv_parts=[(t_lat, n_ctx), (0, n)])
        mo = attention(q_m, k_m, v_m, name="mla_attn", **lat_q, **mla_args)
        go = attention(q_g, k_g, v_g, name="gqa_attn", **lat_q, **gqa_args)

        filt = (hy_w1[l], hy_b1[l], hy_w2[l], hy_b2[l], hy_w3[l], hy_freq[l])

        def hyena(m, row0):
            lag_tabs, sym_tabs = dft[m]
            kr, ki = hyena_spectra(*hyena_time_filters(m, *filt), *lag_tabs)
            return hyena_mixer(p, hy_conv[l], hy_skip[l], kr, ki, *sym_tabs, bsz=bsz, n=m, row0=row0)

        mo, go, ho = [mo], [go], [hyena(n, 0)]

        if not last:
            ctx_q = dict(n_q=n_ctx, tq=n_ctx, q_row0=t_lat, kv_parts=[(t_lat, n_ctx)])
            mo.append(attention(q_m, k_m, v_m, name="mla_attn_ctx", **ctx_q, **mla_args))
            go.append(attention(q_g, k_g, v_g, name="gqa_attn_ctx", **ctx_q, **gqa_args))
            ho.append(hyena(n_ctx, t_lat))

        wo = w_out[l].astype(BF16)
        rw_t = router_w[l].T
        rw_hi = rw_t.astype(BF16)
        rw_lo = (rw_t - rw_hi.astype(F32)).astype(BF16)
        x_mid, f, logits_t = out_proj(ho, mo, go, wo[:HY_W], wo[HY_W:HY_W + MLA_HEADS * MLA_V],
                                      wo[HY_W + MLA_HEADS * MLA_V:], x_parts, vec(norm_g[l, 1]),
                                      vec(norm_g[l, 2]), mod, rw_hi, rw_lo, tm, mod_idx(tm))

        e_t, gate_t, rank_t, counts = router(logits_t, router_bias[l], tm)
        blocks, slot_tok, pos = dispatch_plan(e_t[:TOP_K], rank_t[:TOP_K], counts[:, 0])
        xs = f[slot_tok]
        y_shared = shared_expert(f, sh_w_gate[l].astype(BF16), sh_w_up[l].astype(BF16),
                                 sh_w_down[l].astype(BF16), tm)
        ys = moe_experts(*blocks, xs, exp_w_gate, exp_w_up, exp_w_down, l, y_shared)
        next_proj = None if last else (vec(norm_g[l + 1, 0]), mod_all[l + 1], pack_w_in(w_in[l + 1]))
        outs = ffn_combine(y_shared, ys[pos], gate_t.T, x_mid, vec(norm_g[l, 3]), mod, tm_ffn,
                           mod_idx(tm_ffn), next_proj)
        x_parts, p = [outs[0]], (None if last else outs[1])

    return x_parts[0].reshape(bsz, n, d)
```
